```python
import math
import jax, jax.numpy as jnp
from jax import lax
import numpy as np

D_MODEL = 4096
BATCH = 2
SEQ = 8192
DEPTH = 1
DEC_BATCH = 8
DEC_SEQ = 16
PAST_LEN = 2048

CHUNK = 64
MIX_W = D_MODEL
DN_HEADS = 16
DN_DK = 128
DN_DV = 128
DN_QK_W = DN_HEADS * DN_DK
DN_V_W = DN_HEADS * DN_DV
DN_QKV_W = 2 * DN_QK_W + DN_V_W
CONV_W = 4
RW_HEAD = 64
RW_W = MIX_W - DN_V_W
RW_HEADS = RW_W // RW_HEAD
RW_LORA_W = 96
RW_LORA_A = 96
RW_LORA_G = 256
RW_SHIFT_W = 3 * RW_W + RW_LORA_W + RW_LORA_A + RW_LORA_G
IN_W = DN_QKV_W + DN_V_W + 2 * DN_HEADS + RW_SHIFT_W
D_FF = -(-8 * D_MODEL // (3 * 256)) * 256
NORM_EPS = 1e-6
RW_GN_EPS = 64e-5

kernel_name = 'hymba_gdn_rwkv7_streaming_step'


def rms_norm(x, w, eps=NORM_EPS):
    x32 = x.astype(jnp.float32)
    y = x32 * lax.rsqrt(jnp.mean(x32 * x32, axis=-1, keepdims=True) + eps)
    return (y * w.astype(jnp.float32)).astype(x.dtype)


def l2_normalize(x, eps=1e-6):
    x32 = x.astype(jnp.float32)
    return x32 * lax.rsqrt(jnp.sum(x32 * x32, axis=-1, keepdims=True) + eps)


def causal_conv(u, buf, w):
    t = u.shape[1]
    xp = jnp.concatenate([buf.astype(u.dtype), u], axis=1)
    out = xp[:, 0:t] * w[0]
    for j in range(1, CONV_W):
        out = out + xp[:, j:j + t] * w[j]
    return out, xp[:, t:]


def token_shift(u, buf, mu):
    t = u.shape[1]
    prev = jnp.concatenate([buf.astype(u.dtype), u], axis=1)[:, :t]
    return u + mu * (prev - u), u[:, t - 1:]


def gated_delta_rule(q, k, v, g, beta, s0):
    b, t, h, _ = q.shape
    dv = v.shape[-1]
    n = -(-t // CHUNK)
    pad = n * CHUNK - t

    def blocks(a):
        a = jnp.pad(a, [(0, 0), (0, pad)] + [(0, 0)] * (a.ndim - 2))
        a = a.reshape((b, n, CHUNK) + a.shape[2:])
        return jnp.moveaxis(jnp.moveaxis(a, 3, 2), 1, 0)

    q, k, v, g, beta = blocks(q), blocks(k), blocks(v), blocks(g), blocks(beta)
    gc = jnp.cumsum(g, axis=-1)
    idx = jnp.arange(CHUNK)
    causal = idx[:, None] >= idx[None, :]
    strict = idx[:, None] > idx[None, :]
    decay = jnp.exp(jnp.where(causal, gc[..., :, None] - gc[..., None, :], -jnp.inf))
    kb = k * beta[..., None]
    lower = jnp.where(strict, jnp.einsum('nbhik,nbhjk->nbhij', kb, k) * decay, 0.0)
    eye = jnp.eye(CHUNK, dtype=q.dtype)
    rhs = jnp.concatenate([v * beta[..., None], kb * jnp.exp(gc)[..., None]], axis=-1)
    sol = lax.linalg.triangular_solve(eye + lower, rhs, left_side=True, lower=True, unit_diagonal=True)
    u, wk = sol[..., :dv], sol[..., dv:]
    qk = jnp.einsum('nbhik,nbhjk->nbhij', q, k) * decay
    q_dec = q * jnp.exp(gc)[..., None]
    k_dec = k * jnp.exp(gc[..., -1:] - gc)[..., None]
    g_last = jnp.exp(gc[..., -1])

    def step(S, inp):
        u_c, w_c, qd_c, qk_c, kd_c, gl_c = inp
        v_new = u_c - jnp.einsum('bhck,bhkv->bhcv', w_c, S)
        o = jnp.einsum('bhck,bhkv->bhcv', qd_c, S) + jnp.einsum('bhij,bhjv->bhiv', qk_c, v_new)
        S = S * gl_c[..., None, None] + jnp.einsum('bhck,bhcv->bhkv', kd_c, v_new)
        return S, o

    S, o = lax.scan(step, s0, (u, wk, q_dec, qk, k_dec, g_last))
    o = jnp.moveaxis(jnp.moveaxis(o, 0, 1), 2, 3).reshape(b, n * CHUNK, h, dv)[:, :t]
    return o, S


def rwkv7_recurrence(r, w, k, v, a_vec, b_vec, s0):
    def step(S, inp):
        r_t, w_t, k_t, v_t, a_t, b_t = inp
        sa = jnp.einsum('bhvk,bhk->bhv', S, a_t)
        S = S * w_t[:, :, None, :] + sa[..., None] * b_t[:, :, None, :] + v_t[..., None] * k_t[:, :, None, :]
        return S, jnp.einsum('bhvk,bhk->bhv', S, r_t)

    xs = (jnp.moveaxis(r, 1, 0), jnp.moveaxis(w, 1, 0), jnp.moveaxis(k, 1, 0),
          jnp.moveaxis(v, 1, 0), jnp.moveaxis(a_vec, 1, 0), jnp.moveaxis(b_vec, 1, 0))
    S, ys = lax.scan(step, s0, xs)
    return jnp.moveaxis(ys, 0, 1), S


def layer(x, dn_state, dn_conv, rw_state, rw_shift, params):
    (g_mix_pre, g_mix_post, w_in, dn_conv_w, dn_a_log, dn_dt_bias, dn_norm_w,
     rw_mu, rw_w0, rw_w2, rw_a0, rw_a2, rw_g2, rw_k_k, rw_k_a, rw_r_k, rw_ln_w, rw_ln_b,
     w_out, g_ffn_pre, g_ffn_post, w_gate, w_up, w_down) = params
    f32 = jnp.float32
    b, t, _ = x.shape
    h = rms_norm(x, g_mix_pre)
    proj = h @ w_in
    o1 = DN_QKV_W
    o2 = o1 + DN_V_W
    o3 = o2 + DN_HEADS
    o4 = o3 + DN_HEADS
    dn_qkv, dn_z, dn_b, dn_a, rw_in = proj[..., :o1], proj[..., o1:o2], proj[..., o2:o3], proj[..., o3:o4], proj[..., o4:]

    qkv, new_dn_conv = causal_conv(dn_qkv, dn_conv, dn_conv_w)
    qkv = jax.nn.silu(qkv)
    q = l2_normalize(qkv[..., :DN_QK_W].reshape(b, t, DN_HEADS, DN_DK)) * (DN_DK ** -0.5)
    k = l2_normalize(qkv[..., DN_QK_W:2 * DN_QK_W].reshape(b, t, DN_HEADS, DN_DK))
    v = qkv[..., 2 * DN_QK_W:].reshape(b, t, DN_HEADS, DN_DV).astype(f32)
    beta = jax.nn.sigmoid(dn_b.astype(f32))
    g = -jnp.exp(dn_a_log.astype(f32)) * jax.nn.softplus(dn_a.astype(f32) + dn_dt_bias.astype(f32))
    o, new_dn_state = gated_delta_rule(q, k, v, g, beta, dn_state.astype(f32))
    o = rms_norm(o, dn_norm_w) * jax.nn.silu(dn_z.reshape(b, t, DN_HEADS, DN_DV).astype(f32))
    y_a = o.reshape(b, t, DN_V_W)

    rw, new_rw_shift = token_shift(rw_in, rw_shift, rw_mu)
    rw = rw.astype(f32)
    s1, s2, s3 = RW_W, 2 * RW_W, 3 * RW_W
    s4, s5 = s3 + RW_LORA_W, s3 + RW_LORA_W + RW_LORA_A
    r, kr, vr = rw[..., :s1], rw[..., s1:s2], rw[..., s2:s3]
    xw, xa, xg = rw[..., s3:s4], rw[..., s4:s5], rw[..., s5:]
    w_log = -jax.nn.softplus(-(rw_w0 + jnp.tanh(xw) @ rw_w2)) - 0.5
    w_dec = jnp.exp(-jnp.exp(w_log))
    a = jax.nn.sigmoid(rw_a0 + xa @ rw_a2)
    gate = jax.nn.sigmoid(xg) @ rw_g2
    heads = lambda z: z.reshape(b, t, RW_HEADS, RW_HEAD)
    kk = l2_normalize(heads(kr * rw_k_k))
    kr = kr * (1.0 + (a - 1.0) * rw_k_a)
    r_h, k_h, v_h, a_h = heads(r), heads(kr), heads(vr), heads(a)
    y, new_rw_state = rwkv7_recurrence(r_h, heads(w_dec), k_h, v_h, -kk, kk * a_h, rw_state.astype(f32))
    mean = jnp.mean(y, axis=-1, keepdims=True)
    var = jnp.mean(jnp.square(y - mean), axis=-1, keepdims=True)
    y = ((y - mean) * lax.rsqrt(var + RW_GN_EPS)).reshape(b, t, RW_W) * rw_ln_w + rw_ln_b
    bonus = jnp.sum(r_h * k_h * rw_r_k, axis=-1, keepdims=True) * v_h
    y_b = (y + bonus.reshape(b, t, RW_W)) * gate

    mix = jnp.concatenate([y_a, y_b], axis=-1).astype(x.dtype)
    x = x + rms_norm(mix @ w_out, g_mix_post)

    h2 = rms_norm(x, g_ffn_pre)
    f = (jax.nn.silu(h2 @ w_gate) * (h2 @ w_up)) @ w_down
    x = x + rms_norm(f, g_ffn_post)
    return x, (new_dn_state.astype(dn_state.dtype), new_dn_conv.astype(dn_conv.dtype),
               new_rw_state.astype(rw_state.dtype), new_rw_shift.astype(rw_shift.dtype))


def setup_inputs(seed: int = 0) -> dict:
    key = jax.random.key(seed)
    ks = jax.random.split(key, 32)
    f32 = jnp.float32
    L = DEPTH

    def nrm(k, shape, scale):
        return jax.random.normal(k, shape, f32) * scale

    dt = jnp.exp(jax.random.uniform(ks[11], (L, DN_HEADS), f32, math.log(1e-3), math.log(1e-1)))
    return {
        'x_prompt': nrm(ks[0], (BATCH, SEQ, D_MODEL), 1.0),
        'x_sample': nrm(ks[1], (DEC_BATCH, DEC_SEQ, D_MODEL), 1.0),
        'state_dn': nrm(ks[2], (L, DEC_BATCH, DN_HEADS, DN_DK, DN_DV), 0.1),
        'cache_dn_conv': nrm(ks[3], (L, DEC_BATCH, CONV_W - 1, DN_QKV_W), 1.0),
        'state_rwkv': nrm(ks[4], (L, DEC_BATCH, RW_HEADS, RW_HEAD, RW_HEAD), 0.1),
        'cache_rwkv_shift': nrm(ks[5], (L, DEC_BATCH, 1, RW_SHIFT_W), 1.0),
        'g_mix_pre': 1.0 + nrm(ks[6], (L, D_MODEL), 0.05),
        'g_mix_post': 1.0 + nrm(ks[7], (L, D_MODEL), 0.05),
        'w_in': nrm(ks[8], (L, D_MODEL, IN_W), D_MODEL ** -0.5),
        'dn_conv_w': nrm(ks[9], (L, CONV_W, DN_QKV_W), CONV_W ** -0.5),
        'dn_a_log': jnp.log(jax.random.uniform(ks[10], (L, DN_HEADS), f32, 1.0, 16.0)),
        'dn_dt_bias': dt + jnp.log(-jnp.expm1(-dt)),
        'dn_norm_w': 1.0 + nrm(ks[12], (L, DN_DV), 0.05),
        'rw_mu': jax.random.uniform(ks[13], (L, RW_SHIFT_W), f32, 0.0, 1.0),
        'rw_w0': jax.random.uniform(ks[14], (L, RW_W), f32, -6.0, 1.0),
        'rw_w2': nrm(ks[15], (L, RW_LORA_W, RW_W), 0.5 * RW_LORA_W ** -0.5),
        'rw_a0': nrm(ks[16], (L, RW_W), 0.5),
        'rw_a2': nrm(ks[17], (L, RW_LORA_A, RW_W), RW_LORA_A ** -0.5),
        'rw_g2': nrm(ks[18], (L, RW_LORA_G, RW_W), RW_LORA_G ** -0.5),
        'rw_k_k': 0.85 + nrm(ks[19], (L, RW_W), 0.05),
        'rw_k_a': 1.0 + nrm(ks[20], (L, RW_W), 0.05),
        'rw_r_k': nrm(ks[21], (L, RW_HEADS, RW_HEAD), 0.1),
        'rw_ln_w': 1.0 + nrm(ks[22], (L, RW_W), 0.05),
        'rw_ln_b': nrm(ks[23], (L, RW_W), 0.02),
        'w_out': nrm(ks[24], (L, MIX_W, D_MODEL), MIX_W ** -0.5),
        'g_ffn_pre': 1.0 + nrm(ks[25], (L, D_MODEL), 0.05),
        'g_ffn_post': 1.0 + nrm(ks[26], (L, D_MODEL), 0.05),
        'w_gate': nrm(ks[27], (L, D_MODEL, D_FF), D_MODEL ** -0.5),
        'w_up': nrm(ks[28], (L, D_MODEL, D_FF), D_MODEL ** -0.5),
        'w_down': nrm(ks[29], (L, D_FF, D_MODEL), D_FF ** -0.5),
    }


def reference(x_prompt, x_sample, state_dn, cache_dn_conv, state_rwkv, cache_rwkv_shift,
              g_mix_pre, g_mix_post, w_in, dn_conv_w, dn_a_log, dn_dt_bias, dn_norm_w,
              rw_mu, rw_w0, rw_w2, rw_a0, rw_a2, rw_g2, rw_k_k, rw_k_a, rw_r_k, rw_ln_w, rw_ln_b,
              w_out, g_ffn_pre, g_ffn_post, w_gate, w_up, w_down):
    bp = x_prompt.shape[0]
    dt = x_prompt.dtype
    yp, ys = x_prompt, x_sample
    p_dn, p_conv, p_rw, p_shift = [], [], [], []
    s_dn, s_conv, s_rw, s_shift = [], [], [], []
    for l in range(DEPTH):
        params = (g_mix_pre[l], g_mix_post[l], w_in[l], dn_conv_w[l], dn_a_log[l], dn_dt_bias[l], dn_norm_w[l],
                  rw_mu[l], rw_w0[l], rw_w2[l], rw_a0[l], rw_a2[l], rw_g2[l], rw_k_k[l], rw_k_a[l], rw_r_k[l],
                  rw_ln_w[l], rw_ln_b[l], w_out[l], g_ffn_pre[l], g_ffn_post[l], w_gate[l], w_up[l], w_down[l])
        yp, (a0, a1, a2, a3) = layer(
            yp,
            jnp.zeros((bp, DN_HEADS, DN_DK, DN_DV), dt),
            jnp.zeros((bp, CONV_W - 1, DN_QKV_W), dt),
            jnp.zeros((bp, RW_HEADS, RW_HEAD, RW_HEAD), dt),
            jnp.zeros((bp, 1, RW_SHIFT_W), dt),
            params)
        p_dn.append(a0); p_conv.append(a1); p_rw.append(a2); p_shift.append(a3)
        ys, (c0, c1, c2, c3) = layer(ys, state_dn[l], cache_dn_conv[l], state_rwkv[l], cache_rwkv_shift[l], params)
        s_dn.append(c0); s_conv.append(c1); s_rw.append(c2); s_shift.append(c3)
    return (yp, ys,
            jnp.stack(p_dn), jnp.stack(p_conv), jnp.stack(p_rw), jnp.stack(p_shift),
            jnp.stack(s_dn), jnp.stack(s_conv), jnp.stack(s_rw), jnp.stack(s_shift))
```

```python
import functools
import math

import jax
import jax.numpy as jnp
from jax import lax
from jax.experimental import pallas as pl
from jax.experimental.pallas import tpu as pltpu

F32 = jnp.float32
BF16 = jnp.bfloat16

LANE = 128
SUBLANE = 8
VMEM_LIMIT_BYTES = 56 * 2**20
NORM_EPS = 1e-6
L2_EPS = 1e-6
RW_GN_EPS = 64e-5
IN_TN = 512


def _cparams(sem):
    return pltpu.CompilerParams(dimension_semantics=sem, vmem_limit_bytes=VMEM_LIMIT_BYTES)


def _round_up(x, m):
    return -(-x // m) * m


def _pick(n, cands):
    for c in cands:
        if n % c == 0:
            return c
    return n


def _dot(a, b):
    return jnp.dot(a.astype(BF16), b.astype(BF16), preferred_element_type=F32)


def _dot_nt(a, b):
    return lax.dot_general(a.astype(BF16), b.astype(BF16), (((1,), (1,)), ((), ())),
                           preferred_element_type=F32)


def _dot_tn(a, b):
    return lax.dot_general(a.astype(BF16), b.astype(BF16), (((0,), (0,)), ((), ())),
                           preferred_element_type=F32)


def _dot_f32(a, b):
    return jnp.dot(a, b, preferred_element_type=F32, precision=lax.Precision.HIGHEST)


def _sigmoid(x):
    return 1.0 / (1.0 + jnp.exp(-x))


def _softplus(x):
    return jnp.maximum(x, 0.0) + jnp.log(1.0 + jnp.exp(-jnp.abs(x)))


def _iota2(shape, dim):
    return lax.broadcasted_iota(jnp.int32, shape, dim)


def _inv_unit_lower(low):
    c = low.shape[0]
    eye = (_iota2((c, c), 0) == _iota2((c, c), 1)).astype(F32)
    x = eye - low
    p = low
    for _ in range(c.bit_length() - 2):
        p = _dot(p, p)
        x = x + _dot(x, p)
    return x


def _rms(x, g):
    return x * lax.rsqrt(jnp.mean(x * x, axis=-1, keepdims=True) + NORM_EPS) * g


def _rms_kernel(x_ref, g_ref, o_ref):
    o_ref[...] = _rms(x_ref[...], g_ref[...]).astype(o_ref.dtype)


def _rms_bf16(x, g):
    m, d = x.shape
    tm = _pick(m, (256, 128))
    return pl.pallas_call(
        _rms_kernel,
        grid=(m // tm,),
        in_specs=[pl.BlockSpec((tm, d), lambda i: (i, 0)), pl.BlockSpec((1, d), lambda i: (0, 0))],
        out_specs=pl.BlockSpec((tm, d), lambda i: (i, 0)),
        out_shape=jax.ShapeDtypeStruct((m, d), BF16),
        compiler_params=_cparams(("parallel",)),
        name="rms_pre",
    )(x, g.reshape(1, d))


def _resid_rms_kernel(x_ref, y_ref, gp_ref, gn_ref, x1_ref, h_ref):
    x1 = x_ref[...] + _rms(y_ref[...], gp_ref[...])
    x1_ref[...] = x1
    h_ref[...] = _rms(x1, gn_ref[...]).astype(h_ref.dtype)


def _resid_rms(x, y, g_post, g_next):
    m, d = x.shape
    tm = _pick(m, (256, 128))
    row = pl.BlockSpec((tm, d), lambda i: (i, 0))
    par = pl.BlockSpec((1, d), lambda i: (0, 0))
    return pl.pallas_call(
        _resid_rms_kernel,
        grid=(m // tm,),
        in_specs=[row, row, par, par],
        out_specs=[row, row],
        out_shape=[jax.ShapeDtypeStruct((m, d), F32), jax.ShapeDtypeStruct((m, d), BF16)],
        compiler_params=_cparams(("parallel",)),
        name="resid_rms",
    )(x, y, g_post.reshape(1, d), g_next.reshape(1, d))


def _resid_final_kernel(x_ref, y_ref, g_ref, o_ref):
    o_ref[...] = x_ref[...] + _rms(y_ref[...], g_ref[...])


def _resid_final(x, y, g):
    m, d = x.shape
    tm = _pick(m, (256, 128))
    row = pl.BlockSpec((tm, d), lambda i: (i, 0))
    return pl.pallas_call(
        _resid_final_kernel,
        grid=(m // tm,),
        in_specs=[row, row, pl.BlockSpec((1, d), lambda i: (0, 0))],
        out_specs=row,
        out_shape=jax.ShapeDtypeStruct((m, d), F32),
        compiler_params=_cparams(("parallel",)),
        name="resid_final",
    )(x, y, g.reshape(1, d))


def _mm_kernel(x_ref, w_ref, o_ref):
    o_ref[...] = jnp.dot(x_ref[...], w_ref[...], preferred_element_type=F32).astype(o_ref.dtype)


def _matmul(x, w, *, tm, tn, out_dtype, name):
    m, k = x.shape
    n = w.shape[1]
    return pl.pallas_call(
        _mm_kernel,
        grid=(m // tm, n // tn),
        in_specs=[pl.BlockSpec((tm, k), lambda i, j: (i, 0)), pl.BlockSpec((k, tn), lambda i, j: (0, j))],
        out_specs=pl.BlockSpec((tm, tn), lambda i, j: (i, j)),
        out_shape=jax.ShapeDtypeStruct((m, n), out_dtype),
        compiler_params=_cparams(("parallel", "arbitrary")),
        name=name,
    )(x, w)


def _mm2_kernel(xa_ref, xb_ref, wa_ref, wb_ref, o_ref):
    acc = jnp.dot(xa_ref[...], wa_ref[...], preferred_element_type=F32)
    acc = acc + jnp.dot(xb_ref[...], wb_ref[...], preferred_element_type=F32)
    o_ref[...] = acc.astype(o_ref.dtype)


def _matmul2(xa, xb, wa, wb, *, tm, tn, name):
    m, ka = xa.shape
    kb = xb.shape[1]
    n = wa.shape[1]
    return pl.pallas_call(
        _mm2_kernel,
        grid=(m // tm, n // tn),
        in_specs=[pl.BlockSpec((tm, ka), lambda i, j: (i, 0)), pl.BlockSpec((tm, kb), lambda i, j: (i, 0)),
                  pl.BlockSpec((ka, tn), lambda i, j: (0, j)), pl.BlockSpec((kb, tn), lambda i, j: (0, j))],
        out_specs=pl.BlockSpec((tm, tn), lambda i, j: (i, j)),
        out_shape=jax.ShapeDtypeStruct((m, n), F32),
        compiler_params=_cparams(("parallel", "arbitrary")),
        name=name,
    )(xa, xb, wa, wb)


def _swiglu_kernel(x_ref, wg_ref, wu_ref, o_ref):
    x = x_ref[...]
    g = jnp.dot(x, wg_ref[...], preferred_element_type=F32)
    u = jnp.dot(x, wu_ref[...], preferred_element_type=F32)
    o_ref[...] = (g * _sigmoid(g) * u).astype(o_ref.dtype)


def _swiglu_up(x, wg, wu, *, tm, tn):
    m, k = x.shape
    n = wg.shape[1]
    wspec = pl.BlockSpec((k, tn), lambda i, j: (0, j))
    return pl.pallas_call(
        _swiglu_kernel,
        grid=(m // tm, n // tn),
        in_specs=[pl.BlockSpec((tm, k), lambda i, j: (i, 0)), wspec, wspec],
        out_specs=pl.BlockSpec((tm, tn), lambda i, j: (i, j)),
        out_shape=jax.ShapeDtypeStruct((m, n), BF16),
        compiler_params=_cparams(("parallel", "arbitrary")),
        name="ffn_up",
    )(x, wg, wu)


def _shift_rows(u, halo, s):
    ru = pltpu.roll(u, s, axis=0)
    rh = pltpu.roll(halo, s, axis=0)
    top = jnp.where(_iota2(halo.shape, 0) < s, rh, ru[:SUBLANE])
    if u.shape[0] == SUBLANE:
        return top
    return jnp.concatenate([top, ru[SUBLANE:]], axis=0)


def _dn_pre_kernel(p_ref, halo_ref, cache_ref, cw_ref, ba_ref, alog_ref, dtb_ref,
                   qkv_ref, gb_ref, *, heads, dk, hb):
    i = pl.program_id(1)
    j = pl.program_id(2)
    u = p_ref[0]
    halo = jnp.where(i == 0, cache_ref[0], halo_ref[0])
    cw = cw_ref[...]
    conv = _shift_rows(u, halo, 3) * cw[0:1]
    conv = conv + _shift_rows(u, halo, 2) * cw[1:2]
    conv = conv + _shift_rows(u, halo, 1) * cw[2:3]
    conv = conv + u * cw[3:4]
    s = conv * _sigmoid(conv)

    @pl.when(j < 2)
    def _():
        scale = jnp.where(j == 0, dk ** -0.5, 1.0).astype(F32)
        for h in range(heads):
            sh = s[:, h * dk:(h + 1) * dk]
            inv = lax.rsqrt(jnp.sum(sh * sh, axis=-1, keepdims=True) + L2_EPS) * scale
            qkv_ref[0, :, h * dk:(h + 1) * dk] = sh * inv

    @pl.when(j == 2)
    def _():
        qkv_ref[0] = s

    @pl.when(j == 0)
    def _():
        x = ba_ref[0]
        lane = _iota2(x.shape, 1)
        g = -jnp.exp(alog_ref[...]) * _softplus(x + dtb_ref[...])
        full = jnp.where(lane < heads, _sigmoid(x), g)
        for hg in range(heads // hb):
            gb_ref[0, hg] = full if hg == 0 else pltpu.roll(full, LANE - hg * hb, axis=1)


def _dn_pre(p, cache8, conv_w, alog_row, dtb_row, *, heads, dk, hb, c_ba):
    b, t, _ = p.shape
    w = heads * dk
    tm = _pick(t, (256, 128, 64, 32, 16))
    groups = heads // hb
    kern = functools.partial(_dn_pre_kernel, heads=heads, dk=dk, hb=hb)
    return pl.pallas_call(
        kern,
        grid=(b, t // tm, 3),
        in_specs=[
            pl.BlockSpec((1, tm, w), lambda bb, i, j: (bb, i, j)),
            pl.BlockSpec((1, SUBLANE, w), lambda bb, i, j: (bb, jnp.maximum(i * (tm // SUBLANE) - 1, 0), j)),
            pl.BlockSpec((1, SUBLANE, w), lambda bb, i, j: (bb, 0, j)),
            pl.BlockSpec((4, w), lambda bb, i, j: (0, j)),
            pl.BlockSpec((1, tm, LANE), lambda bb, i, j: (bb, i, c_ba // LANE)),
            pl.BlockSpec((1, LANE), lambda bb, i, j: (0, 0)),
            pl.BlockSpec((1, LANE), lambda bb, i, j: (0, 0)),
        ],
        out_specs=[
            pl.BlockSpec((1, tm, w), lambda bb, i, j: (bb, i, j)),
            pl.BlockSpec((1, groups, tm, LANE), lambda bb, i, j: (bb, 0, i, 0)),
        ],
        out_shape=[jax.ShapeDtypeStruct((b, t, 3 * w), F32),
                   jax.ShapeDtypeStruct((b, groups, t, LANE), F32)],
        compiler_params=_cparams(("parallel", "parallel", "arbitrary")),
        name="dn_pre",
    )(p, p, cache8, conv_w, p, alog_row, dtb_row)


def _dn_kernel(q_ref, k_ref, v_ref, gb_ref, z_ref, nw_ref, s0_ref, y_ref, sout_ref, s_scr,
               *, heads, hb, dk, chunk):
    c = pl.program_id(2)
    nc = pl.num_programs(2)

    @pl.when(c == 0)
    def _():
        s_scr[...] = s0_ref[0]

    gbt = gb_ref[0, 0]
    row = _iota2((chunk, chunk), 0)
    col = _iota2((chunk, chunk), 1)
    causal = row >= col
    strict = row > col
    tri = causal.astype(F32)
    gc_all = _dot_f32(tri, gbt)
    gc_t = jnp.transpose(gc_all)
    nw = nw_ref[...]
    for h in range(hb):
        sl = slice(h * dk, (h + 1) * dk)
        q = q_ref[0, :, sl]
        k = k_ref[0, :, sl]
        v = v_ref[0, :, sl]
        beta = gbt[:, h:h + 1]
        gcol = gc_all[:, heads + h:heads + h + 1]
        grow = gc_t[heads + h:heads + h + 1, :]
        glast = gc_all[chunk - 1:chunk, heads + h:heads + h + 1]
        decay = jnp.where(causal, jnp.exp(gcol - grow), 0.0)
        kb = k * beta
        low = jnp.where(strict, _dot_nt(kb, k) * decay, 0.0)
        tinv = _inv_unit_lower(low)
        eg = jnp.exp(gcol)
        rhs = jnp.concatenate([v * beta, kb * eg], axis=1)
        sol = _dot(tinv, rhs)
        u = sol[:, :dk]
        w = sol[:, dk:]
        qk = jnp.where(causal, _dot_nt(q, k) * decay, 0.0)
        q_dec = q * eg
        k_dec = k * jnp.exp(glast - gcol)
        s = s_scr[h]
        v_new = u - _dot(w, s)
        o = _dot(q_dec, s) + _dot(qk, v_new)
        s_scr[h] = s * jnp.exp(glast) + _dot_tn(k_dec, v_new)
        z = z_ref[0, :, sl]
        y = _rms(o, nw) * (z * _sigmoid(z))
        y_ref[0, :, sl] = y.astype(y_ref.dtype)

    @pl.when(c == nc - 1)
    def _():
        sout_ref[0] = s_scr[...]


def _dn_recurrence(qkv, gb, p, norm_w, s0, *, heads, hb, dk, chunk, c_z):
    b, t, _ = qkv.shape
    groups = heads // hb
    wb = hb * dk
    nqk = heads * dk // wb
    kern = functools.partial(_dn_kernel, heads=heads, hb=hb, dk=dk, chunk=chunk)
    return pl.pallas_call(
        kern,
        grid=(b, groups, t // chunk),
        in_specs=[
            pl.BlockSpec((1, chunk, wb), lambda bb, g, c: (bb, c, g)),
            pl.BlockSpec((1, chunk, wb), lambda bb, g, c: (bb, c, nqk + g)),
            pl.BlockSpec((1, chunk, wb), lambda bb, g, c: (bb, c, 2 * nqk + g)),
            pl.BlockSpec((1, 1, chunk, LANE), lambda bb, g, c: (bb, g, c, 0)),
            pl.BlockSpec((1, chunk, wb), lambda bb, g, c: (bb, c, c_z // wb + g)),
            pl.BlockSpec((1, dk), lambda bb, g, c: (0, 0)),
            pl.BlockSpec((1, hb, dk, dk), lambda bb, g, c: (bb, g, 0, 0)),
        ],
        out_specs=[
            pl.BlockSpec((1, chunk, wb), lambda bb, g, c: (bb, c, g)),
            pl.BlockSpec((1, hb, dk, dk), lambda bb, g, c: (bb, g, 0, 0)),
        ],
        out_shape=[jax.ShapeDtypeStruct((b, t, heads * dk), BF16),
                   jax.ShapeDtypeStruct((b, heads, dk, dk), F32)],
        scratch_shapes=[pltpu.VMEM((hb, dk, dk), F32)],
        compiler_params=_cparams(("parallel", "parallel", "arbitrary")),
        name="dn_recurrence",
    )(qkv, qkv, qkv, gb, p, norm_w.reshape(1, dk), s0)


def _pair_sums(x, m0):
    s0 = jnp.sum(jnp.where(m0, x, 0.0), axis=-1, keepdims=True)
    s1 = jnp.sum(jnp.where(m0, 0.0, x), axis=-1, keepdims=True)
    return jnp.where(m0, s0, s1)


def _token_shift(p_ref, halo_ref, cache_ref, mu_ref, first):
    u = p_ref[0]
    halo = jnp.where(first, cache_ref[0], halo_ref[0])
    return u + mu_ref[...] * (_shift_rows(u, halo, 1) - u)


def _rw_pre_kernel(p_ref, halo_ref, cache_ref, mu_ref, pl_ref, halol_ref, cachel_ref, mul_ref,
                   w0_ref, w2_ref, a0_ref, a2_ref, g2_ref, kk_ref, ka_ref,
                   r_ref, k_ref, v_ref, lw_ref, av_ref, bv_ref, gate_ref, *, rw, lw_pad, la_pad, lg_pad):
    first = pl.program_id(1) == 0
    x = _token_shift(p_ref, halo_ref, cache_ref, mu_ref, first)
    xl = _token_shift(pl_ref, halol_ref, cachel_ref, mul_ref, first)
    r = x[:, :rw]
    kr = x[:, rw:2 * rw]
    vr = x[:, 2 * rw:3 * rw]
    xw = xl[:, :lw_pad]
    xa = xl[:, lw_pad:lw_pad + la_pad]
    xg = xl[:, lw_pad + la_pad:lw_pad + la_pad + lg_pad]
    w_log = -_softplus(-(w0_ref[...] + _dot(jnp.tanh(xw), w2_ref[...]))) - 0.5
    a = _sigmoid(a0_ref[...] + _dot(xa, a2_ref[...]))
    gate_ref[0] = _dot(_sigmoid(xg), g2_ref[...])
    r_ref[0] = r
    v_ref[0] = vr
    lw_ref[0] = -jnp.exp(w_log)
    k_ref[0] = kr * (1.0 + (a - 1.0) * ka_ref[...])
    kkr = kr * kk_ref[...]
    m0 = _iota2((1, LANE), 1) < LANE // 2
    for jb in range(rw // LANE):
        sl = slice(jb * LANE, (jb + 1) * LANE)
        blk = kkr[:, sl]
        kk = blk * lax.rsqrt(_pair_sums(blk * blk, m0) + L2_EPS)
        av_ref[0, :, sl] = -kk
        bv_ref[0, :, sl] = kk * a[:, sl]


def _rw_pre(p, cache_rkv8, cache_lo8, mu_rkv, mu_lo, w0, w2p, a0, a2p, g2p, k_k, k_a, *, rw, c_rkv, c_lo):
    b, t, _ = p.shape
    tm = _pick(t, (128, 64, 32, 16))
    lw_pad, la_pad, lg_pad = w2p.shape[0], a2p.shape[0], g2p.shape[0]
    lblk = lw_pad + la_pad + lg_pad
    kern = functools.partial(_rw_pre_kernel, rw=rw, lw_pad=lw_pad, la_pad=la_pad, lg_pad=lg_pad)
    full = lambda shape: pl.BlockSpec(shape, lambda bb, i: (0,) * len(shape))
    row = pl.BlockSpec((1, tm, rw), lambda bb, i: (bb, i, 0))

    def shifted(width, cb):
        return [pl.BlockSpec((1, tm, width), lambda bb, i: (bb, i, cb)),
                pl.BlockSpec((1, SUBLANE, width),
                             lambda bb, i: (bb, jnp.maximum(i * (tm // SUBLANE) - 1, 0), cb)),
                pl.BlockSpec((1, SUBLANE, width), lambda bb, i: (bb, 0, 0)),
                full((1, width))]

    return pl.pallas_call(
        kern,
        grid=(b, t // tm),
        in_specs=shifted(3 * rw, c_rkv // (3 * rw)) + shifted(lblk, c_lo // lblk) + [
            full((1, rw)), full(w2p.shape), full((1, rw)), full(a2p.shape), full(g2p.shape),
            full((1, rw)), full((1, rw)),
        ],
        out_specs=[row] * 7,
        out_shape=[jax.ShapeDtypeStruct((b, t, rw), F32)] * 7,
        compiler_params=_cparams(("parallel", "arbitrary")),
        name="rw_pre",
    )(p, p, cache_rkv8, mu_rkv, p, p, cache_lo8, mu_lo, w0, w2p, a0, a2p, g2p, k_k, k_a)


def _rw_kernel(r_ref, k_ref, v_ref, lw_ref, av_ref, bv_ref, gate_ref, rk_ref, lnw_ref, lnb_ref, s0_ref,
               y_ref, sout_ref, s_scr, *, pb, chunk):
    c = pl.program_id(2)
    nc = pl.num_programs(2)
    hn = LANE // 2

    @pl.when(c == 0)
    def _():
        s_scr[...] = s0_ref[0]

    row = _iota2((chunk, chunk), 0)
    col = _iota2((chunk, chunk), 1)
    incl = row >= col
    strict = row > col
    tri = incl.astype(F32)
    m0 = _iota2((1, LANE), 1) < hn
    blockmask = (_iota2((LANE, LANE), 0) < hn) == (_iota2((LANE, LANE), 1) < hn)
    for pidx in range(pb):
        sl = slice(pidx * LANE, (pidx + 1) * LANE)
        r = r_ref[0, :, sl]
        k = k_ref[0, :, sl]
        v = v_ref[0, :, sl]
        lw = lw_ref[0, :, sl]
        av = av_ref[0, :, sl]
        bv = bv_ref[0, :, sl]
        cw = _dot_f32(tri, lw)
        tot = cw[chunk - 1:chunk, :]
        e_neg = jnp.exp(-cw)
        a_t = av * jnp.exp(cw - lw)
        r_t = r * jnp.exp(cw)
        b_t = bv * e_neg
        k_t = k * e_neg
        e_end = jnp.exp(tot - cw)
        b_h = bv * e_end
        k_h = k * e_end
        rhs_g = jnp.concatenate([b_t, k_t], axis=0)
        s = s_scr[pidx]
        rhs1 = _dot_nt(a_t, s)
        tinv, a_rb, a_rk, av_sel = [], [], [], []
        for e in range(2):
            me = m0 if e == 0 else jnp.logical_not(m0)
            lhs = jnp.concatenate([jnp.where(me, a_t, 0.0), jnp.where(me, r_t, 0.0)], axis=0)
            g = _dot_nt(lhs, rhs_g)
            a_ab = jnp.where(strict, g[:chunk, :chunk], 0.0)
            a_ak = jnp.where(strict, g[:chunk, chunk:], 0.0)
            a_rb.append(jnp.where(incl, g[chunk:, :chunk], 0.0))
            a_rk.append(jnp.where(incl, g[chunk:, chunk:], 0.0))
            tinv.append(_inv_unit_lower(-a_ab))
            av_sel.append(_dot(a_ak, v))
        rhs_u = rhs1 + jnp.where(m0, av_sel[0], av_sel[1])
        u = jnp.where(m0, _dot(tinv[0], rhs_u), _dot(tinv[1], rhs_u))
        y0 = _dot(a_rb[0], u) + _dot(a_rk[0], v)
        y1 = _dot(a_rb[1], u) + _dot(a_rk[1], v)
        y = _dot_nt(r_t, s) + jnp.where(m0, y0, y1)
        s_new = s * jnp.exp(tot) + _dot_tn(u, b_h) + _dot_tn(v, k_h)
        s_scr[pidx] = jnp.where(blockmask, s_new, 0.0)
        mean = _pair_sums(y, m0) * (1.0 / hn)
        d = y - mean
        var = _pair_sums(d * d, m0) * (1.0 / hn)
        yn = d * lax.rsqrt(var + RW_GN_EPS) * lnw_ref[:, sl] + lnb_ref[:, sl]
        bonus = _pair_sums(r * k * rk_ref[:, sl], m0) * v
        y_ref[0, :, sl] = ((yn + bonus) * gate_ref[0, :, sl]).astype(y_ref.dtype)

    @pl.when(c == nc - 1)
    def _():
        sout_ref[0] = s_scr[...]


def _rw_recurrence(r, k, v, lw, av, bv, gate, r_k, ln_w, ln_b, s0p, *, pb, chunk):
    b, t, rw = r.shape
    pairs = rw // LANE
    groups = pairs // pb
    wb = pb * LANE
    kern = functools.partial(_rw_kernel, pb=pb, chunk=chunk)
    tile = pl.BlockSpec((1, chunk, wb), lambda bb, g, c: (bb, c, g))
    par = pl.BlockSpec((1, wb), lambda bb, g, c: (0, g))
    st = pl.BlockSpec((1, pb, LANE, LANE), lambda bb, g, c: (bb, g, 0, 0))
    return pl.pallas_call(
        kern,
        grid=(b, groups, t // chunk),
        in_specs=[tile] * 7 + [par] * 3 + [st],
        out_specs=[tile, st],
        out_shape=[jax.ShapeDtypeStruct((b, t, rw), BF16),
                   jax.ShapeDtypeStruct((b, pairs, LANE, LANE), F32)],
        scratch_shapes=[pltpu.VMEM((pb, LANE, LANE), F32)],
        compiler_params=_cparams(("parallel", "parallel", "arbitrary")),
        name="rw_recurrence",
    )(r, k, v, lw, av, bv, gate, r_k, ln_w, ln_b, s0p)


def _pad_cols(a, n):
    return jnp.pad(a, [(0, 0)] * (a.ndim - 1) + [(0, n - a.shape[-1])])


def _pad_rows(a, n):
    return jnp.pad(a, [(0, n - a.shape[0])] + [(0, 0)] * (a.ndim - 1))


def _prepare(w):
    heads = w["dn_a_log"].shape[-1]
    dk = w["dn_norm_w"].shape[-1]
    qkv_w = w["dn_conv_w"].shape[-1]
    v_w = heads * dk
    assert qkv_w == 3 * v_w and dk == LANE
    rw_heads, rw_head = w["rw_r_k"].shape
    assert rw_head == LANE // 2
    rw = rw_heads * rw_head
    lw_n, la_n, lg_n = w["rw_w2"].shape[0], w["rw_a2"].shape[0], w["rw_g2"].shape[0]
    lw_pad, la_pad, lg_pad = (_round_up(n, LANE) for n in (lw_n, la_n, lg_n))
    o1 = qkv_w
    o2 = o1 + v_w
    o3 = o2 + heads
    o4 = o3 + heads
    s3 = o4 + 3 * rw
    s4 = s3 + lw_n
    s5 = s4 + la_n
    w_in = w["w_in"]
    lblk = lw_pad + la_pad + lg_pad
    assert o1 % (3 * rw) == 0
    c_rkv = o1
    c_z = c_rkv + 3 * rw
    c_lo = _round_up(c_z + v_w, lblk)
    c_ba = c_lo + lblk
    n_pad = _round_up(c_ba + LANE, IN_TN)
    d = w_in.shape[0]
    parts = [w_in[:, :o1], w_in[:, o4:s3], w_in[:, o1:o2], jnp.zeros((d, c_lo - c_z - v_w), F32),
             _pad_cols(w_in[:, s3:s4], lw_pad), _pad_cols(w_in[:, s4:s5], la_pad),
             _pad_cols(w_in[:, s5:], lg_pad), _pad_cols(w_in[:, o2:o4], n_pad - c_ba)]
    w_in_p = jnp.concatenate(parts, axis=1).astype(BF16)

    def lora_cols(a):
        return jnp.concatenate([_pad_cols(a[..., :lw_n], lw_pad), _pad_cols(a[..., lw_n:lw_n + la_n], la_pad),
                                _pad_cols(a[..., lw_n + la_n:], lg_pad)], axis=-1)

    lane_pad = lambda a: jnp.pad(a, (heads, LANE - 2 * heads)).reshape(1, LANE)
    return dict(
        heads=heads, dk=dk, rw=rw, rw_heads=rw_heads, c_rkv=c_rkv, c_z=c_z, c_lo=c_lo, c_ba=c_ba,
        dims=(o1, lw_n, la_n, lg_n, lw_pad, la_pad, lblk), lora_cols=lora_cols,
        w_in_p=w_in_p,
        alog_row=lane_pad(w["dn_a_log"]), dtb_row=lane_pad(w["dn_dt_bias"]),
        mu_rkv=w["rw_mu"][:3 * rw].reshape(1, 3 * rw),
        mu_lo=lora_cols(w["rw_mu"][3 * rw:]).reshape(1, lblk),
        w2p=_pad_rows(w["rw_w2"], lw_pad).astype(BF16),
        a2p=_pad_rows(w["rw_a2"], la_pad).astype(BF16),
        g2p=_pad_rows(w["rw_g2"], lg_pad).astype(BF16),
        w_out_a=w["w_out"][:v_w].astype(BF16), w_out_b=w["w_out"][v_w:].astype(BF16),
        w_gate=w["w_gate"].astype(BF16), w_up=w["w_up"].astype(BF16), w_down=w["w_down"].astype(BF16),
    )


def _layer(x, dn_state, dn_conv, rw_state, rw_shift, w, pp):
    b, t, d = x.shape
    m = b * t
    heads, dk, rw, rw_heads = pp["heads"], pp["dk"], pp["rw"], pp["rw_heads"]
    c_rkv, c_z, c_lo, c_ba = pp["c_rkv"], pp["c_z"], pp["c_lo"], pp["c_ba"]
    o1, lw_n, la_n, lg_n, lw_pad, la_pad, lblk = pp["dims"]
    chunk = 64 if t % 64 == 0 else t
    assert chunk & (chunk - 1) == 0 and chunk >= 2 * SUBLANE
    hb = 4 if heads % 4 == 0 else heads
    pairs = rw // LANE
    pb = 2 if pairs % 2 == 0 else 1
    tm = _pick(m, (1024, 512, 256, 128))

    xf = x.reshape(m, d)
    h = _rms_bf16(xf, w["g_mix_pre"])
    n_pad = pp["w_in_p"].shape[1]
    p = _matmul(h, pp["w_in_p"], tm=tm, tn=IN_TN, out_dtype=F32, name="in_proj").reshape(b, t, n_pad)

    cache8 = jnp.pad(dn_conv.astype(F32), ((0, 0), (SUBLANE - dn_conv.shape[1], 0), (0, 0)))
    qkv, gb = _dn_pre(p, cache8, w["dn_conv_w"], pp["alog_row"], pp["dtb_row"],
                      heads=heads, dk=dk, hb=hb, c_ba=c_ba)
    y_a, new_dn_state = _dn_recurrence(qkv, gb, p, w["dn_norm_w"], dn_state.astype(F32),
                                       heads=heads, hb=hb, dk=dk, chunk=chunk, c_z=c_z)
    new_dn_conv = p[:, t - dn_conv.shape[1]:, :o1]

    shift = rw_shift.astype(F32)
    front = ((0, 0), (SUBLANE - 1, 0), (0, 0))
    r, k, v, lw, av, bv, gate = _rw_pre(p, jnp.pad(shift[..., :3 * rw], front),
                                        jnp.pad(pp["lora_cols"](shift[..., 3 * rw:]), front),
                                        pp["mu_rkv"], pp["mu_lo"], w["rw_w0"].reshape(1, rw), pp["w2p"],
                                        w["rw_a0"].reshape(1, rw), pp["a2p"], pp["g2p"],
                                        w["rw_k_k"].reshape(1, rw), w["rw_k_a"].reshape(1, rw),
                                        rw=rw, c_rkv=c_rkv, c_lo=c_lo)
    hn = LANE // 2
    s4 = rw_state.astype(F32).reshape(b, pairs, 2, hn, hn)
    zeros = jnp.zeros_like(s4[:, :, 0])
    s0p = jnp.concatenate([jnp.concatenate([s4[:, :, 0], zeros], axis=-1),
                           jnp.concatenate([zeros, s4[:, :, 1]], axis=-1)], axis=-2)
    y_b, sp = _rw_recurrence(r, k, v, lw, av, bv, gate, w["rw_r_k"].reshape(1, rw),
                             w["rw_ln_w"].reshape(1, rw), w["rw_ln_b"].reshape(1, rw), s0p,
                             pb=pb, chunk=chunk)
    new_rw_state = jnp.stack([sp[:, :, :hn, :hn], sp[:, :, hn:, hn:]], axis=2).reshape(b, rw_heads, hn, hn)
    last = p[:, t - 1:]
    new_rw_shift = jnp.concatenate(
        [last[..., c_rkv:c_rkv + 3 * rw], last[..., c_lo:c_lo + lw_n],
         last[..., c_lo + lw_pad:c_lo + lw_pad + la_n],
         last[..., c_lo + lw_pad + la_pad:c_lo + lw_pad + la_pad + lg_n]], axis=-1)

    mixo = _matmul2(y_a.reshape(m, heads * dk), y_b.reshape(m, rw), pp["w_out_a"], pp["w_out_b"],
                    tm=_pick(m, (512, 256, 128)), tn=_pick(d, (512, 256, 128)), name="out_proj")
    x1, h2 = _resid_rms(xf, mixo, w["g_mix_post"], w["g_ffn_pre"])
    dff = pp["w_gate"].shape[1]
    f = _swiglu_up(h2, pp["w_gate"], pp["w_up"], tm=tm, tn=_pick(dff, (256, 128)))
    fo = _matmul(f, pp["w_down"], tm=_pick(m, (512, 256, 128)), tn=_pick(d, (256, 128)),
                 out_dtype=F32, name="ffn_down")
    out = _resid_final(x1, fo, w["g_ffn_post"]).reshape(b, t, d)
    return out, (new_dn_state, new_dn_conv, new_rw_state, new_rw_shift)


_WEIGHT_NAMES = ("g_mix_pre", "g_mix_post", "w_in", "dn_conv_w", "dn_a_log", "dn_dt_bias", "dn_norm_w",
                 "rw_mu", "rw_w0", "rw_w2", "rw_a0", "rw_a2", "rw_g2", "rw_k_k", "rw_k_a", "rw_r_k",
                 "rw_ln_w", "rw_ln_b", "w_out", "g_ffn_pre", "g_ffn_post", "w_gate", "w_up", "w_down")


def kernel(x_prompt, x_sample, state_dn, cache_dn_conv, state_rwkv, cache_rwkv_shift,
           g_mix_pre, g_mix_post, w_in, dn_conv_w, dn_a_log, dn_dt_bias, dn_norm_w,
           rw_mu, rw_w0, rw_w2, rw_a0, rw_a2, rw_g2, rw_k_k, rw_k_a, rw_r_k, rw_ln_w, rw_ln_b,
           w_out, g_ffn_pre, g_ffn_post, w_gate, w_up, w_down):
    stacked = (g_mix_pre, g_mix_post, w_in, dn_conv_w, dn_a_log, dn_dt_bias, dn_norm_w,
               rw_mu, rw_w0, rw_w2, rw_a0, rw_a2, rw_g2, rw_k_k, rw_k_a, rw_r_k, rw_ln_w, rw_ln_b,
               w_out, g_ffn_pre, g_ffn_post, w_gate, w_up, w_down)
    depth = w_in.shape[0]
    bp = x_prompt.shape[0]
    dt = x_prompt.dtype
    yp, ys = x_prompt, x_sample
    outs_p, outs_s = [], []
    for l in range(depth):
        w = {n: a[l] for n, a in zip(_WEIGHT_NAMES, stacked)}
        pp = _prepare(w)
        heads, dk, rw_heads = pp["heads"], pp["dk"], pp["rw_heads"]
        hn = LANE // 2
        yp, st_p = _layer(yp,
                          jnp.zeros((bp, heads, dk, dk), dt),
                          jnp.zeros((bp, cache_dn_conv.shape[2], cache_dn_conv.shape[3]), dt),
                          jnp.zeros((bp, rw_heads, hn, hn), dt),
                          jnp.zeros((bp, 1, cache_rwkv_shift.shape[3]), dt), w, pp)
        ys, st_s = _layer(ys, state_dn[l], cache_dn_conv[l], state_rwkv[l], cache_rwkv_shift[l], w, pp)
        outs_p.append(st_p)
        outs_s.append(st_s)
    stack = lambda outs, i: jnp.stack([o[i] for o in outs])
    return (yp, ys,
            stack(outs_p, 0), stack(outs_p, 1), stack(outs_p, 2), stack(outs_p, 3),
            stack(outs_s, 0), stack(outs_s, 1), stack(outs_s, 2), stack(outs_s, 3))
```

```python
import functools
import math

import jax
import jax.numpy as jnp
from jax import lax
from jax.experimental import pallas as pl
from jax.experimental.pallas import tpu as pltpu

F32 = jnp.float32
BF16 = jnp.bfloat16

LANE = 128
SUBLANE = 8
VMEM_LIMIT_BYTES = 56 * 2**20
NORM_EPS = 1e-6
L2_EPS = 1e-6
RW_GN_EPS = 64e-5
IN_TN = 512


def _cparams(sem):
    return pltpu.CompilerParams(dimension_semantics=sem, vmem_limit_bytes=VMEM_LIMIT_BYTES)


def _round_up(x, m):
    return -(-x // m) * m


def _pick(n, cands):
    for c in cands:
        if n % c == 0:
            return c
    return n


def _dot(a, b):
    return jnp.dot(a.astype(BF16), b.astype(BF16), preferred_element_type=F32)


def _dot_nt(a, b):
    return lax.dot_general(a.astype(BF16), b.astype(BF16), (((1,), (1,)), ((), ())),
                           preferred_element_type=F32)


def _dot_tn(a, b):
    return lax.dot_general(a.astype(BF16), b.astype(BF16), (((0,), (0,)), ((), ())),
                           preferred_element_type=F32)


def _dot_f32(a, b):
    return jnp.dot(a, b, preferred_element_type=F32, precision=lax.Precision.HIGHEST)


def _sigmoid(x):
    return 1.0 / (1.0 + jnp.exp(-x))


def _softplus(x):
    return jnp.maximum(x, 0.0) + jnp.log(1.0 + jnp.exp(-jnp.abs(x)))


def _iota2(shape, dim):
    return lax.broadcasted_iota(jnp.int32, shape, dim)


def _inv_unit_lower_many(lows):
    c = lows[0].shape[0]
    eye = (_iota2((c, c), 0) == _iota2((c, c), 1)).astype(F32)
    xs = [eye - low for low in lows]
    ps = list(lows)
    for _ in range(c.bit_length() - 2):
        ps = [_dot(p, p) for p in ps]
        xs = [x + _dot(x, p) for x, p in zip(xs, ps)]
    return xs


def _rms(x, g):
    return x * lax.rsqrt(jnp.mean(x * x, axis=-1, keepdims=True) + NORM_EPS) * g


def _rms_kernel(x_ref, g_ref, o_ref):
    o_ref[...] = _rms(x_ref[...], g_ref[...]).astype(o_ref.dtype)


def _rms_bf16(x, g):
    m, d = x.shape
    tm = _pick(m, (256, 128))
    return pl.pallas_call(
        _rms_kernel,
        grid=(m // tm,),
        in_specs=[pl.BlockSpec((tm, d), lambda i: (i, 0)), pl.BlockSpec((1, d), lambda i: (0, 0))],
        out_specs=pl.BlockSpec((tm, d), lambda i: (i, 0)),
        out_shape=jax.ShapeDtypeStruct((m, d), BF16),
        compiler_params=_cparams(("parallel",)),
        name="rms_pre",
    )(x, g.reshape(1, d))


def _resid_rms_kernel(x_ref, y_ref, gp_ref, gn_ref, x1_ref, h_ref):
    x1 = x_ref[...] + _rms(y_ref[...], gp_ref[...])
    x1_ref[...] = x1
    h_ref[...] = _rms(x1, gn_ref[...]).astype(h_ref.dtype)


def _resid_rms(x, y, g_post, g_next):
    m, d = x.shape
    tm = _pick(m, (256, 128))
    row = pl.BlockSpec((tm, d), lambda i: (i, 0))
    par = pl.BlockSpec((1, d), lambda i: (0, 0))
    return pl.pallas_call(
        _resid_rms_kernel,
        grid=(m // tm,),
        in_specs=[row, row, par, par],
        out_specs=[row, row],
        out_shape=[jax.ShapeDtypeStruct((m, d), F32), jax.ShapeDtypeStruct((m, d), BF16)],
        compiler_params=_cparams(("parallel",)),
        name="resid_rms",
    )(x, y, g_post.reshape(1, d), g_next.reshape(1, d))


def _resid_final_kernel(x_ref, y_ref, g_ref, o_ref):
    o_ref[...] = x_ref[...] + _rms(y_ref[...], g_ref[...])


def _resid_final(x, y, g):
    m, d = x.shape
    tm = _pick(m, (256, 128))
    row = pl.BlockSpec((tm, d), lambda i: (i, 0))
    return pl.pallas_call(
        _resid_final_kernel,
        grid=(m // tm,),
        in_specs=[row, row, pl.BlockSpec((1, d), lambda i: (0, 0))],
        out_specs=row,
        out_shape=jax.ShapeDtypeStruct((m, d), F32),
        compiler_params=_cparams(("parallel",)),
        name="resid_final",
    )(x, y, g.reshape(1, d))


def _mm_kernel(x_ref, w_ref, o_ref):
    o_ref[...] = jnp.dot(x_ref[...], w_ref[...], preferred_element_type=F32).astype(o_ref.dtype)


def _matmul(x, w, *, tm, tn, out_dtype, name):
    m, k = x.shape
    n = w.shape[1]
    return pl.pallas_call(
        _mm_kernel,
        grid=(m // tm, n // tn),
        in_specs=[pl.BlockSpec((tm, k), lambda i, j: (i, 0)), pl.BlockSpec((k, tn), lambda i, j: (0, j))],
        out_specs=pl.BlockSpec((tm, tn), lambda i, j: (i, j)),
        out_shape=jax.ShapeDtypeStruct((m, n), out_dtype),
        compiler_params=_cparams(("parallel", "arbitrary")),
        name=name,
    )(x, w)


def _mm2_kernel(xa_ref, xb_ref, wa_ref, wb_ref, o_ref):
    acc = jnp.dot(xa_ref[...], wa_ref[...], preferred_element_type=F32)
    acc = acc + jnp.dot(xb_ref[...], wb_ref[...], preferred_element_type=F32)
    o_ref[...] = acc.astype(o_ref.dtype)


def _matmul2(xa, xb, wa, wb, *, tm, tn, name):
    m, ka = xa.shape
    kb = xb.shape[1]
    n = wa.shape[1]
    return pl.pallas_call(
        _mm2_kernel,
        grid=(m // tm, n // tn),
        in_specs=[pl.BlockSpec((tm, ka), lambda i, j: (i, 0)), pl.BlockSpec((tm, kb), lambda i, j: (i, 0)),
                  pl.BlockSpec((ka, tn), lambda i, j: (0, j)), pl.BlockSpec((kb, tn), lambda i, j: (0, j))],
        out_specs=pl.BlockSpec((tm, tn), lambda i, j: (i, j)),
        out_shape=jax.ShapeDtypeStruct((m, n), F32),
        compiler_params=_cparams(("parallel", "arbitrary")),
        name=name,
    )(xa, xb, wa, wb)


def _swiglu_kernel(x_ref, wg_ref, wu_ref, o_ref):
    x = x_ref[...]
    g = jnp.dot(x, wg_ref[...], preferred_element_type=F32)
    u = jnp.dot(x, wu_ref[...], preferred_element_type=F32)
    o_ref[...] = (g * _sigmoid(g) * u).astype(o_ref.dtype)


def _swiglu_up(x, wg, wu, *, tm, tn):
    m, k = x.shape
    n = wg.shape[1]
    wspec = pl.BlockSpec((k, tn), lambda i, j: (0, j))
    return pl.pallas_call(
        _swiglu_kernel,
        grid=(m // tm, n // tn),
        in_specs=[pl.BlockSpec((tm, k), lambda i, j: (i, 0)), wspec, wspec],
        out_specs=pl.BlockSpec((tm, tn), lambda i, j: (i, j)),
        out_shape=jax.ShapeDtypeStruct((m, n), BF16),
        compiler_params=_cparams(("parallel", "arbitrary")),
        name="ffn_up",
    )(x, wg, wu)


def _shift_rows(u, halo, s):
    ru = pltpu.roll(u, s, axis=0)
    rh = pltpu.roll(halo, s, axis=0)
    top = jnp.where(_iota2(halo.shape, 0) < s, rh, ru[:SUBLANE])
    if u.shape[0] == SUBLANE:
        return top
    return jnp.concatenate([top, ru[SUBLANE:]], axis=0)


def _dn_pre_kernel(p_ref, halo_ref, cache_ref, cw_ref, ba_ref, alog_ref, dtb_ref,
                   qkv_ref, gb_ref, *, heads, dk, hb):
    i = pl.program_id(1)
    j = pl.program_id(2)
    u = p_ref[0]
    halo = jnp.where(i == 0, cache_ref[0], halo_ref[0])
    cw = cw_ref[...]
    conv = _shift_rows(u, halo, 3) * cw[0:1]
    conv = conv + _shift_rows(u, halo, 2) * cw[1:2]
    conv = conv + _shift_rows(u, halo, 1) * cw[2:3]
    conv = conv + u * cw[3:4]
    s = conv * _sigmoid(conv)

    @pl.when(j < 2)
    def _():
        scale = jnp.where(j == 0, dk ** -0.5, 1.0).astype(F32)
        for h in range(heads):
            sh = s[:, h * dk:(h + 1) * dk]
            inv = lax.rsqrt(jnp.sum(sh * sh, axis=-1, keepdims=True) + L2_EPS) * scale
            qkv_ref[0, :, h * dk:(h + 1) * dk] = sh * inv

    @pl.when(j == 2)
    def _():
        qkv_ref[0] = s

    @pl.when(j == 0)
    def _():
        x = ba_ref[0]
        lane = _iota2(x.shape, 1)
        g = -jnp.exp(alog_ref[...]) * _softplus(x + dtb_ref[...])
        full = jnp.where(lane < heads, _sigmoid(x), g)
        for hg in range(heads // hb):
            gb_ref[0, hg] = full if hg == 0 else pltpu.roll(full, LANE - hg * hb, axis=1)


def _dn_pre(p, cache8, conv_w, alog_row, dtb_row, *, heads, dk, hb, c_ba):
    b, t, _ = p.shape
    w = heads * dk
    tm = _pick(t, (256, 128, 64, 32, 16))
    groups = heads // hb
    kern = functools.partial(_dn_pre_kernel, heads=heads, dk=dk, hb=hb)
    return pl.pallas_call(
        kern,
        grid=(b, t // tm, 3),
        in_specs=[
            pl.BlockSpec((1, tm, w), lambda bb, i, j: (bb, i, j)),
            pl.BlockSpec((1, SUBLANE, w), lambda bb, i, j: (bb, jnp.maximum(i * (tm // SUBLANE) - 1, 0), j)),
            pl.BlockSpec((1, SUBLANE, w), lambda bb, i, j: (bb, 0, j)),
            pl.BlockSpec((4, w), lambda bb, i, j: (0, j)),
            pl.BlockSpec((1, tm, LANE), lambda bb, i, j: (bb, i, c_ba // LANE)),
            pl.BlockSpec((1, LANE), lambda bb, i, j: (0, 0)),
            pl.BlockSpec((1, LANE), lambda bb, i, j: (0, 0)),
        ],
        out_specs=[
            pl.BlockSpec((1, tm, w), lambda bb, i, j: (bb, i, j)),
            pl.BlockSpec((1, groups, tm, LANE), lambda bb, i, j: (bb, 0, i, 0)),
        ],
        out_shape=[jax.ShapeDtypeStruct((b, t, 3 * w), F32),
                   jax.ShapeDtypeStruct((b, groups, t, LANE), F32)],
        compiler_params=_cparams(("parallel", "parallel", "arbitrary")),
        name="dn_pre",
    )(p, p, cache8, conv_w, p, alog_row, dtb_row)


def _dn_kernel(q_ref, k_ref, v_ref, gb_ref, z_ref, nw_ref, s0_ref, y_ref, sout_ref, s_scr,
               *, heads, hb, dk, chunk):
    c = pl.program_id(2)
    nc = pl.num_programs(2)

    @pl.when(c == 0)
    def _():
        s_scr[...] = s0_ref[0]

    gbt = gb_ref[0, 0]
    row = _iota2((chunk, chunk), 0)
    col = _iota2((chunk, chunk), 1)
    causal = row >= col
    strict = row > col
    tri = causal.astype(F32)
    gc_all = _dot_f32(tri, gbt)
    gc_t = jnp.transpose(gc_all)
    nw = nw_ref[...]
    hs = range(hb)
    sl = [slice(h * dk, (h + 1) * dk) for h in hs]
    q = [q_ref[0, :, sl[h]] for h in hs]
    k = [k_ref[0, :, sl[h]] for h in hs]
    v = [v_ref[0, :, sl[h]] for h in hs]
    s = [s_scr[h] for h in hs]
    beta = [gbt[:, h:h + 1] for h in hs]
    gcol = [gc_all[:, heads + h:heads + h + 1] for h in hs]
    glast = [gc_all[chunk - 1:chunk, heads + h:heads + h + 1] for h in hs]
    decay = [jnp.where(causal, jnp.exp(gcol[h] - gc_t[heads + h:heads + h + 1, :]), 0.0) for h in hs]
    kb = [k[h] * beta[h] for h in hs]
    eg = [jnp.exp(gcol[h]) for h in hs]
    low = [jnp.where(strict, _dot_nt(kb[h], k[h]) * decay[h], 0.0) for h in hs]
    qk = [jnp.where(causal, _dot_nt(q[h], k[h]) * decay[h], 0.0) for h in hs]
    os = [_dot(q[h] * eg[h], s[h]) for h in hs]
    tinv = _inv_unit_lower_many(low)
    sol = [_dot(tinv[h], jnp.concatenate([v[h] * beta[h], kb[h] * eg[h]], axis=1)) for h in hs]
    ws = [_dot(sol[h][:, dk:], s[h]) for h in hs]
    v_new = [sol[h][:, :dk] - ws[h] for h in hs]
    o = [os[h] + _dot(qk[h], v_new[h]) for h in hs]
    s_new = [s[h] * jnp.exp(glast[h]) + _dot_tn(k[h] * jnp.exp(glast[h] - gcol[h]), v_new[h]) for h in hs]
    for h in hs:
        z = z_ref[0, :, sl[h]]
        y_ref[0, :, sl[h]] = (_rms(o[h], nw) * (z * _sigmoid(z))).astype(y_ref.dtype)
        s_scr[h] = s_new[h]

    @pl.when(c == nc - 1)
    def _():
        sout_ref[0] = s_scr[...]


def _dn_recurrence(qkv, gb, p, norm_w, s0, *, heads, hb, dk, chunk, c_z):
    b, t, _ = qkv.shape
    groups = heads // hb
    wb = hb * dk
    nqk = heads * dk // wb
    kern = functools.partial(_dn_kernel, heads=heads, hb=hb, dk=dk, chunk=chunk)
    return pl.pallas_call(
        kern,
        grid=(b, groups, t // chunk),
        in_specs=[
            pl.BlockSpec((1, chunk, wb), lambda bb, g, c: (bb, c, g)),
            pl.BlockSpec((1, chunk, wb), lambda bb, g, c: (bb, c, nqk + g)),
            pl.BlockSpec((1, chunk, wb), lambda bb, g, c: (bb, c, 2 * nqk + g)),
            pl.BlockSpec((1, 1, chunk, LANE), lambda bb, g, c: (bb, g, c, 0)),
            pl.BlockSpec((1, chunk, wb), lambda bb, g, c: (bb, c, c_z // wb + g)),
            pl.BlockSpec((1, dk), lambda bb, g, c: (0, 0)),
            pl.BlockSpec((1, hb, dk, dk), lambda bb, g, c: (bb, g, 0, 0)),
        ],
        out_specs=[
            pl.BlockSpec((1, chunk, wb), lambda bb, g, c: (bb, c, g)),
            pl.BlockSpec((1, hb, dk, dk), lambda bb, g, c: (bb, g, 0, 0)),
        ],
        out_shape=[jax.ShapeDtypeStruct((b, t, heads * dk), BF16),
                   jax.ShapeDtypeStruct((b, heads, dk, dk), F32)],
        scratch_shapes=[pltpu.VMEM((hb, dk, dk), F32)],
        compiler_params=_cparams(("parallel", "parallel", "arbitrary")),
        name="dn_recurrence",
    )(qkv, qkv, qkv, gb, p, norm_w.reshape(1, dk), s0)


def _pair_sums(x, m0):
    s0 = jnp.sum(jnp.where(m0, x, 0.0), axis=-1, keepdims=True)
    s1 = jnp.sum(jnp.where(m0, 0.0, x), axis=-1, keepdims=True)
    return jnp.where(m0, s0, s1)


def _token_shift(p_ref, halo_ref, cache_ref, mu_ref, first):
    u = p_ref[0]
    halo = jnp.where(first, cache_ref[0], halo_ref[0])
    return u + mu_ref[...] * (_shift_rows(u, halo, 1) - u)


def _rw_pre_kernel(p_ref, halo_ref, cache_ref, mu_ref, pl_ref, halol_ref, cachel_ref, mul_ref,
                   w0_ref, w2_ref, a0_ref, a2_ref, g2_ref, kk_ref, ka_ref,
                   r_ref, k_ref, v_ref, lw_ref, av_ref, bv_ref, gate_ref, *, rw, lw_pad, la_pad, lg_pad):
    first = pl.program_id(1) == 0
    x = _token_shift(p_ref, halo_ref, cache_ref, mu_ref, first)
    xl = _token_shift(pl_ref, halol_ref, cachel_ref, mul_ref, first)
    r = x[:, :rw]
    kr = x[:, rw:2 * rw]
    vr = x[:, 2 * rw:3 * rw]
    xw = xl[:, :lw_pad]
    xa = xl[:, lw_pad:lw_pad + la_pad]
    xg = xl[:, lw_pad + la_pad:lw_pad + la_pad + lg_pad]
    w_log = -_softplus(-(w0_ref[...] + _dot(jnp.tanh(xw), w2_ref[...]))) - 0.5
    a = _sigmoid(a0_ref[...] + _dot(xa, a2_ref[...]))
    gate_ref[0] = _dot(_sigmoid(xg), g2_ref[...])
    r_ref[0] = r
    v_ref[0] = vr
    lw_ref[0] = -jnp.exp(w_log)
    k_ref[0] = kr * (1.0 + (a - 1.0) * ka_ref[...])
    kkr = kr * kk_ref[...]
    m0 = _iota2((1, LANE), 1) < LANE // 2
    for jb in range(rw // LANE):
        sl = slice(jb * LANE, (jb + 1) * LANE)
        blk = kkr[:, sl]
        kk = blk * lax.rsqrt(_pair_sums(blk * blk, m0) + L2_EPS)
        av_ref[0, :, sl] = -kk
        bv_ref[0, :, sl] = kk * a[:, sl]


def _rw_pre(p, cache_rkv8, cache_lo8, mu_rkv, mu_lo, w0, w2p, a0, a2p, g2p, k_k, k_a, *, rw, c_rkv, c_lo):
    b, t, _ = p.shape
    tm = _pick(t, (128, 64, 32, 16))
    lw_pad, la_pad, lg_pad = w2p.shape[0], a2p.shape[0], g2p.shape[0]
    lblk = lw_pad + la_pad + lg_pad
    kern = functools.partial(_rw_pre_kernel, rw=rw, lw_pad=lw_pad, la_pad=la_pad, lg_pad=lg_pad)
    full = lambda shape: pl.BlockSpec(shape, lambda bb, i: (0,) * len(shape))
    row = pl.BlockSpec((1, tm, rw), lambda bb, i: (bb, i, 0))

    def shifted(width, cb):
        return [pl.BlockSpec((1, tm, width), lambda bb, i: (bb, i, cb)),
                pl.BlockSpec((1, SUBLANE, width),
                             lambda bb, i: (bb, jnp.maximum(i * (tm // SUBLANE) - 1, 0), cb)),
                pl.BlockSpec((1, SUBLANE, width), lambda bb, i: (bb, 0, 0)),
                full((1, width))]

    return pl.pallas_call(
        kern,
        grid=(b, t // tm),
        in_specs=shifted(3 * rw, c_rkv // (3 * rw)) + shifted(lblk, c_lo // lblk) + [
            full((1, rw)), full(w2p.shape), full((1, rw)), full(a2p.shape), full(g2p.shape),
            full((1, rw)), full((1, rw)),
        ],
        out_specs=[row] * 7,
        out_shape=[jax.ShapeDtypeStruct((b, t, rw), F32)] * 7,
        compiler_params=_cparams(("parallel", "arbitrary")),
        name="rw_pre",
    )(p, p, cache_rkv8, mu_rkv, p, p, cache_lo8, mu_lo, w0, w2p, a0, a2p, g2p, k_k, k_a)


def _rw_kernel(r_ref, k_ref, v_ref, lw_ref, av_ref, bv_ref, gate_ref, rk_ref, lnw_ref, lnb_ref, s0_ref,
               y_ref, sout_ref, s_scr, *, pb, chunk):
    c = pl.program_id(2)
    nc = pl.num_programs(2)
    hn = LANE // 2

    @pl.when(c == 0)
    def _():
        s_scr[...] = s0_ref[0]

    row = _iota2((chunk, chunk), 0)
    col = _iota2((chunk, chunk), 1)
    incl = row >= col
    strict = row > col
    tri = incl.astype(F32)
    m0 = _iota2((1, LANE), 1) < hn
    blockmask = (_iota2((LANE, LANE), 0) < hn) == (_iota2((LANE, LANE), 1) < hn)
    ps = range(pb)
    pes = [(p, e) for p in ps for e in range(2)]
    sl = [slice(p * LANE, (p + 1) * LANE) for p in ps]
    r = [r_ref[0, :, sl[p]] for p in ps]
    k = [k_ref[0, :, sl[p]] for p in ps]
    v = [v_ref[0, :, sl[p]] for p in ps]
    lw = [lw_ref[0, :, sl[p]] for p in ps]
    s = [s_scr[p] for p in ps]
    cw = [_dot_f32(tri, lw[p]) for p in ps]
    tot = [cw[p][chunk - 1:chunk, :] for p in ps]
    a_t = [av_ref[0, :, sl[p]] * jnp.exp(cw[p] - lw[p]) for p in ps]
    r_t = [r[p] * jnp.exp(cw[p]) for p in ps]
    e_neg = [jnp.exp(-cw[p]) for p in ps]
    e_end = [jnp.exp(tot[p] - cw[p]) for p in ps]
    bv = [bv_ref[0, :, sl[p]] for p in ps]
    rhs_g = [jnp.concatenate([bv[p] * e_neg[p], k[p] * e_neg[p]], axis=0) for p in ps]
    rhs_s = [jnp.concatenate([bv[p] * e_end[p], k[p] * e_end[p]], axis=0) for p in ps]
    sel = [m0, jnp.logical_not(m0)]
    g = {(p, e): _dot_nt(jnp.concatenate([jnp.where(sel[e], a_t[p], 0.0), jnp.where(sel[e], r_t[p], 0.0)],
                                         axis=0), rhs_g[p]) for p, e in pes}
    rhs1 = [_dot_nt(a_t[p], s[p]) for p in ps]
    ys = [_dot_nt(r_t[p], s[p]) for p in ps]
    tinv = dict(zip(pes, _inv_unit_lower_many(
        [jnp.where(strict, -g[pe][:chunk, :chunk], 0.0) for pe in pes])))
    avs = {(p, e): _dot(jnp.where(strict, g[p, e][:chunk, chunk:], 0.0), v[p]) for p, e in pes}
    rhs_u = [rhs1[p] + jnp.where(m0, avs[p, 0], avs[p, 1]) for p in ps]
    us = {(p, e): _dot(tinv[p, e], rhs_u[p]) for p, e in pes}
    uv = [jnp.concatenate([jnp.where(m0, us[p, 0], us[p, 1]), v[p]], axis=0) for p in ps]
    col2 = _iota2((chunk, 2 * chunk), 1)
    incl2 = _iota2((chunk, 2 * chunk), 0) >= jnp.where(col2 >= chunk, col2 - chunk, col2)
    yr = {(p, e): _dot(jnp.where(incl2, g[p, e][chunk:, :], 0.0), uv[p]) for p, e in pes}
    s_new = [jnp.where(blockmask, s[p] * jnp.exp(tot[p]) + _dot_tn(uv[p], rhs_s[p]), 0.0) for p in ps]
    for p in ps:
        y = ys[p] + jnp.where(m0, yr[p, 0], yr[p, 1])
        mean = _pair_sums(y, m0) * (1.0 / hn)
        d = y - mean
        var = _pair_sums(d * d, m0) * (1.0 / hn)
        yn = d * lax.rsqrt(var + RW_GN_EPS) * lnw_ref[:, sl[p]] + lnb_ref[:, sl[p]]
        bonus = _pair_sums(r[p] * k[p] * rk_ref[:, sl[p]], m0) * v[p]
        y_ref[0, :, sl[p]] = ((yn + bonus) * gate_ref[0, :, sl[p]]).astype(y_ref.dtype)
        s_scr[p] = s_new[p]

    @pl.when(c == nc - 1)
    def _():
        sout_ref[0] = s_scr[...]


def _rw_recurrence(r, k, v, lw, av, bv, gate, r_k, ln_w, ln_b, s0p, *, pb, chunk):
    b, t, rw = r.shape
    pairs = rw // LANE
    groups = pairs // pb
    wb = pb * LANE
    kern = functools.partial(_rw_kernel, pb=pb, chunk=chunk)
    tile = pl.BlockSpec((1, chunk, wb), lambda bb, g, c: (bb, c, g))
    par = pl.BlockSpec((1, wb), lambda bb, g, c: (0, g))
    st = pl.BlockSpec((1, pb, LANE, LANE), lambda bb, g, c: (bb, g, 0, 0))
    return pl.pallas_call(
        kern,
        grid=(b, groups, t // chunk),
        in_specs=[tile] * 7 + [par] * 3 + [st],
        out_specs=[tile, st],
        out_shape=[jax.ShapeDtypeStruct((b, t, rw), BF16),
                   jax.ShapeDtypeStruct((b, pairs, LANE, LANE), F32)],
        scratch_shapes=[pltpu.VMEM((pb, LANE, LANE), F32)],
        compiler_params=_cparams(("parallel", "parallel", "arbitrary")),
        name="rw_recurrence",
    )(r, k, v, lw, av, bv, gate, r_k, ln_w, ln_b, s0p)


def _pad_cols(a, n):
    return jnp.pad(a, [(0, 0)] * (a.ndim - 1) + [(0, n - a.shape[-1])])


def _pad_rows(a, n):
    return jnp.pad(a, [(0, n - a.shape[0])] + [(0, 0)] * (a.ndim - 1))


def _prepare(w):
    heads = w["dn_a_log"].shape[-1]
    dk = w["dn_norm_w"].shape[-1]
    qkv_w = w["dn_conv_w"].shape[-1]
    v_w = heads * dk
    assert qkv_w == 3 * v_w and dk == LANE
    rw_heads, rw_head = w["rw_r_k"].shape
    assert rw_head == LANE // 2
    rw = rw_heads * rw_head
    lw_n, la_n, lg_n = w["rw_w2"].shape[0], w["rw_a2"].shape[0], w["rw_g2"].shape[0]
    lw_pad, la_pad, lg_pad = (_round_up(n, LANE) for n in (lw_n, la_n, lg_n))
    o1 = qkv_w
    o2 = o1 + v_w
    o3 = o2 + heads
    o4 = o3 + heads
    s3 = o4 + 3 * rw
    s4 = s3 + lw_n
    s5 = s4 + la_n
    w_in = w["w_in"]
    lblk = lw_pad + la_pad + lg_pad
    assert o1 % (3 * rw) == 0
    c_rkv = o1
    c_z = c_rkv + 3 * rw
    c_lo = _round_up(c_z + v_w, lblk)
    c_ba = c_lo + lblk
    n_pad = _round_up(c_ba + LANE, IN_TN)
    d = w_in.shape[0]
    parts = [w_in[:, :o1], w_in[:, o4:s3], w_in[:, o1:o2], jnp.zeros((d, c_lo - c_z - v_w), F32),
             _pad_cols(w_in[:, s3:s4], lw_pad), _pad_cols(w_in[:, s4:s5], la_pad),
             _pad_cols(w_in[:, s5:], lg_pad), _pad_cols(w_in[:, o2:o4], n_pad - c_ba)]
    w_in_p = jnp.concatenate(parts, axis=1).astype(BF16)

    def lora_cols(a):
        return jnp.concatenate([_pad_cols(a[..., :lw_n], lw_pad), _pad_cols(a[..., lw_n:lw_n + la_n], la_pad),
                                _pad_cols(a[..., lw_n + la_n:], lg_pad)], axis=-1)

    lane_pad = lambda a: jnp.pad(a, (heads, LANE - 2 * heads)).reshape(1, LANE)
    return dict(
        heads=heads, dk=dk, rw=rw, rw_heads=rw_heads, c_rkv=c_rkv, c_z=c_z, c_lo=c_lo, c_ba=c_ba,
        dims=(o1, lw_n, la_n, lg_n, lw_pad, la_pad, lblk), lora_cols=lora_cols,
        w_in_p=w_in_p,
        alog_row=lane_pad(w["dn_a_log"]), dtb_row=lane_pad(w["dn_dt_bias"]),
        mu_rkv=w["rw_mu"][:3 * rw].reshape(1, 3 * rw),
        mu_lo=lora_cols(w["rw_mu"][3 * rw:]).reshape(1, lblk),
        w2p=_pad_rows(w["rw_w2"], lw_pad).astype(BF16),
        a2p=_pad_rows(w["rw_a2"], la_pad).astype(BF16),
        g2p=_pad_rows(w["rw_g2"], lg_pad).astype(BF16),
        w_out_a=w["w_out"][:v_w].astype(BF16), w_out_b=w["w_out"][v_w:].astype(BF16),
        w_gate=w["w_gate"].astype(BF16), w_up=w["w_up"].astype(BF16), w_down=w["w_down"].astype(BF16),
    )


def _layer(x, dn_state, dn_conv, rw_state, rw_shift, w, pp):
    b, t, d = x.shape
    m = b * t
    heads, dk, rw, rw_heads = pp["heads"], pp["dk"], pp["rw"], pp["rw_heads"]
    c_rkv, c_z, c_lo, c_ba = pp["c_rkv"], pp["c_z"], pp["c_lo"], pp["c_ba"]
    o1, lw_n, la_n, lg_n, lw_pad, la_pad, lblk = pp["dims"]
    chunk = 64 if t % 64 == 0 else t
    assert chunk & (chunk - 1) == 0 and chunk >= 2 * SUBLANE
    hb = 8 if heads % 8 == 0 else heads
    pairs = rw // LANE
    pb = 8 if pairs % 8 == 0 else pairs
    tm = _pick(m, (1024, 512, 256, 128))

    xf = x.reshape(m, d)
    h = _rms_bf16(xf, w["g_mix_pre"])
    n_pad = pp["w_in_p"].shape[1]
    p = _matmul(h, pp["w_in_p"], tm=tm, tn=IN_TN, out_dtype=F32, name="in_proj").reshape(b, t, n_pad)

    cache8 = jnp.pad(dn_conv.astype(F32), ((0, 0), (SUBLANE - dn_conv.shape[1], 0), (0, 0)))
    qkv, gb = _dn_pre(p, cache8, w["dn_conv_w"], pp["alog_row"], pp["dtb_row"],
                      heads=heads, dk=dk, hb=hb, c_ba=c_ba)
    y_a, new_dn_state = _dn_recurrence(qkv, gb, p, w["dn_norm_w"], dn_state.astype(F32),
                                       heads=heads, hb=hb, dk=dk, chunk=chunk, c_z=c_z)
    new_dn_conv = p[:, t - dn_conv.shape[1]:, :o1]

    shift = rw_shift.astype(F32)
    front = ((0, 0), (SUBLANE - 1, 0), (0, 0))
    r, k, v, lw, av, bv, gate = _rw_pre(p, jnp.pad(shift[..., :3 * rw], front),
                                        jnp.pad(pp["lora_cols"](shift[..., 3 * rw:]), front),
                                        pp["mu_rkv"], pp["mu_lo"], w["rw_w0"].reshape(1, rw), pp["w2p"],
                                        w["rw_a0"].reshape(1, rw), pp["a2p"], pp["g2p"],
                                        w["rw_k_k"].reshape(1, rw), w["rw_k_a"].reshape(1, rw),
                                        rw=rw, c_rkv=c_rkv, c_lo=c_lo)
    hn = LANE // 2
    s4 = rw_state.astype(F32).reshape(b, pairs, 2, hn, hn)
    zeros = jnp.zeros_like(s4[:, :, 0])
    s0p = jnp.concatenate([jnp.concatenate([s4[:, :, 0], zeros], axis=-1),
                           jnp.concatenate([zeros, s4[:, :, 1]], axis=-1)], axis=-2)
    y_b, sp = _rw_recurrence(r, k, v, lw, av, bv, gate, w["rw_r_k"].reshape(1, rw),
                             w["rw_ln_w"].reshape(1, rw), w["rw_ln_b"].reshape(1, rw), s0p,
                             pb=pb, chunk=chunk)
    new_rw_state = jnp.stack([sp[:, :, :hn, :hn], sp[:, :, hn:, hn:]], axis=2).reshape(b, rw_heads, hn, hn)
    last = p[:, t - 1:]
    new_rw_shift = jnp.concatenate(
        [last[..., c_rkv:c_rkv + 3 * rw], last[..., c_lo:c_lo + lw_n],
         last[..., c_lo + lw_pad:c_lo + lw_pad + la_n],
         last[..., c_lo + lw_pad + la_pad:c_lo + lw_pad + la_pad + lg_n]], axis=-1)

    mixo = _matmul2(y_a.reshape(m, heads * dk), y_b.reshape(m, rw), pp["w_out_a"], pp["w_out_b"],
                    tm=_pick(m, (512, 256, 128)), tn=_pick(d, (512, 256, 128)), name="out_proj")
    x1, h2 = _resid_rms(xf, mixo, w["g_mix_post"], w["g_ffn_pre"])
    dff = pp["w_gate"].shape[1]
    f = _swiglu_up(h2, pp["w_gate"], pp["w_up"], tm=tm, tn=_pick(dff, (256, 128)))
    fo = _matmul(f, pp["w_down"], tm=_pick(m, (512, 256, 128)), tn=_pick(d, (256, 128)),
                 out_dtype=F32, name="ffn_down")
    out = _resid_final(x1, fo, w["g_ffn_post"]).reshape(b, t, d)
    return out, (new_dn_state, new_dn_conv, new_rw_state, new_rw_shift)


_WEIGHT_NAMES = ("g_mix_pre", "g_mix_post", "w_in", "dn_conv_w", "dn_a_log", "dn_dt_bias", "dn_norm_w",
                 "rw_mu", "rw_w0", "rw_w2", "rw_a0", "rw_a2", "rw_g2", "rw_k_k", "rw_k_a", "rw_r_k",
                 "rw_ln_w", "rw_ln_b", "w_out", "g_ffn_pre", "g_ffn_post", "w_gate", "w_up", "w_down")


def kernel(x_prompt, x_sample, state_dn, cache_dn_conv, state_rwkv, cache_rwkv_shift,
           g_mix_pre, g_mix_post, w_in, dn_conv_w, dn_a_log, dn_dt_bias, dn_norm_w,
           rw_mu, rw_w0, rw_w2, rw_a0, rw_a2, rw_g2, rw_k_k, rw_k_a, rw_r_k, rw_ln_w, rw_ln_b,
           w_out, g_ffn_pre, g_ffn_post, w_gate, w_up, w_down):
    stacked = (g_mix_pre, g_mix_post, w_in, dn_conv_w, dn_a_log, dn_dt_bias, dn_norm_w,
               rw_mu, rw_w0, rw_w2, rw_a0, rw_a2, rw_g2, rw_k_k, rw_k_a, rw_r_k, rw_ln_w, rw_ln_b,
               w_out, g_ffn_pre, g_ffn_post, w_gate, w_up, w_down)
    depth = w_in.shape[0]
    bp = x_prompt.shape[0]
    dt = x_prompt.dtype
    yp, ys = x_prompt, x_sample
    outs_p, outs_s = [], []
    for l in range(depth):
        w = {n: a[l] for n, a in zip(_WEIGHT_NAMES, stacked)}
        pp = _prepare(w)
        heads, dk, rw_heads = pp["heads"], pp["dk"], pp["rw_heads"]
        hn = LANE // 2
        yp, st_p = _layer(yp,
                          jnp.zeros((bp, heads, dk, dk), dt),
                          jnp.zeros((bp, cache_dn_conv.shape[2], cache_dn_conv.shape[3]), dt),
                          jnp.zeros((bp, rw_heads, hn, hn), dt),
                          jnp.zeros((bp, 1, cache_rwkv_shift.shape[3]), dt), w, pp)
        ys, st_s = _layer(ys, state_dn[l], cache_dn_conv[l], state_rwkv[l], cache_rwkv_shift[l], w, pp)
        outs_p.append(st_p)
        outs_s.append(st_s)
    stack = lambda outs, i: jnp.stack([o[i] for o in outs])
    return (yp, ys,
            stack(outs_p, 0), stack(outs_p, 1), stack(outs_p, 2), stack(outs_p, 3),
            stack(outs_s, 0), stack(outs_s, 1), stack(outs_s, 2), stack(outs_s, 3))
```

```python
import functools
import math

import jax
import jax.numpy as jnp
from jax import lax
from jax.experimental import pallas as pl
from jax.experimental.pallas import tpu as pltpu

F32 = jnp.float32
BF16 = jnp.bfloat16

LANE = 128
SUBLANE = 8
VMEM_LIMIT_BYTES = 56 * 2**20
NORM_EPS = 1e-6
L2_EPS = 1e-6
RW_GN_EPS = 64e-5
IN_TN = 512
REC_CHUNKS = 4
REC_CHUNK_TICKS = 3


def _cparams(sem):
    return pltpu.CompilerParams(dimension_semantics=sem, vmem_limit_bytes=VMEM_LIMIT_BYTES)


def _round_up(x, m):
    return -(-x // m) * m


def _pick(n, cands):
    for c in cands:
        if n % c == 0:
            return c
    return n


def _dot(a, b):
    return jnp.dot(a.astype(BF16), b.astype(BF16), preferred_element_type=F32)


def _dot_nt(a, b):
    return lax.dot_general(a.astype(BF16), b.astype(BF16), (((1,), (1,)), ((), ())),
                           preferred_element_type=F32)


def _dot_tn(a, b):
    return lax.dot_general(a.astype(BF16), b.astype(BF16), (((0,), (0,)), ((), ())),
                           preferred_element_type=F32)


def _cumsum_rows(x):
    c = x.shape[0]
    tri = (_iota2((c, c), 0) >= _iota2((c, c), 1)).astype(BF16)
    hi = x.astype(BF16)
    r1 = x - hi.astype(F32)
    mid = r1.astype(BF16)
    lo = (r1 - mid.astype(F32)).astype(BF16)
    dot = lambda p: jnp.dot(tri, p, preferred_element_type=F32)
    return dot(hi) + (dot(mid) + dot(lo))


def _sigmoid(x):
    return 0.5 * jnp.tanh(0.5 * x) + 0.5


def _softplus(x):
    return jnp.maximum(x, 0.0) + jnp.log(1.0 + jnp.exp(-jnp.abs(x)))


def _iota2(shape, dim):
    return lax.broadcasted_iota(jnp.int32, shape, dim)


def _run_staggered(programs):
    live = list(programs)
    tick = 0
    while live:
        still = []
        for start, prog in live:
            if tick >= start:
                try:
                    next(prog)
                except StopIteration:
                    continue
            still.append((start, prog))
        live = still
        tick += 1


def _rms(x, g):
    return x * lax.rsqrt(jnp.mean(x * x, axis=-1, keepdims=True) + NORM_EPS) * g


def _rms_kernel(x_ref, g_ref, o_ref):
    o_ref[...] = _rms(x_ref[...], g_ref[...]).astype(o_ref.dtype)


def _rms_bf16(x, g):
    m, d = x.shape
    tm = _pick(m, (256, 128))
    return pl.pallas_call(
        _rms_kernel,
        grid=(m // tm,),
        in_specs=[pl.BlockSpec((tm, d), lambda i: (i, 0)), pl.BlockSpec((1, d), lambda i: (0, 0))],
        out_specs=pl.BlockSpec((tm, d), lambda i: (i, 0)),
        out_shape=jax.ShapeDtypeStruct((m, d), BF16),
        compiler_params=_cparams(("parallel",)),
        name="rms_pre",
    )(x, g.reshape(1, d))


def _resid_rms_kernel(x_ref, y_ref, gp_ref, gn_ref, x1_ref, h_ref):
    x1 = x_ref[...] + _rms(y_ref[...], gp_ref[...])
    x1_ref[...] = x1
    h_ref[...] = _rms(x1, gn_ref[...]).astype(h_ref.dtype)


def _resid_rms(x, y, g_post, g_next):
    m, d = x.shape
    tm = _pick(m, (256, 128))
    row = pl.BlockSpec((tm, d), lambda i: (i, 0))
    par = pl.BlockSpec((1, d), lambda i: (0, 0))
    return pl.pallas_call(
        _resid_rms_kernel,
        grid=(m // tm,),
        in_specs=[row, row, par, par],
        out_specs=[row, row],
        out_shape=[jax.ShapeDtypeStruct((m, d), F32), jax.ShapeDtypeStruct((m, d), BF16)],
        compiler_params=_cparams(("parallel",)),
        name="resid_rms",
    )(x, y, g_post.reshape(1, d), g_next.reshape(1, d))


def _resid_final_kernel(x_ref, y_ref, g_ref, o_ref):
    o_ref[...] = x_ref[...] + _rms(y_ref[...], g_ref[...])


def _resid_final(x, y, g):
    m, d = x.shape
    tm = _pick(m, (256, 128))
    row = pl.BlockSpec((tm, d), lambda i: (i, 0))
    return pl.pallas_call(
        _resid_final_kernel,
        grid=(m // tm,),
        in_specs=[row, row, pl.BlockSpec((1, d), lambda i: (0, 0))],
        out_specs=row,
        out_shape=jax.ShapeDtypeStruct((m, d), F32),
        compiler_params=_cparams(("parallel",)),
        name="resid_final",
    )(x, y, g.reshape(1, d))


def _mm_kernel(x_ref, w_ref, o_ref):
    o_ref[...] = jnp.dot(x_ref[...], w_ref[...], preferred_element_type=F32).astype(o_ref.dtype)


def _matmul(x, w, *, tm, tn, out_dtype, name):
    m, k = x.shape
    n = w.shape[1]
    return pl.pallas_call(
        _mm_kernel,
        grid=(m // tm, n // tn),
        in_specs=[pl.BlockSpec((tm, k), lambda i, j: (i, 0)), pl.BlockSpec((k, tn), lambda i, j: (0, j))],
        out_specs=pl.BlockSpec((tm, tn), lambda i, j: (i, j)),
        out_shape=jax.ShapeDtypeStruct((m, n), out_dtype),
        compiler_params=_cparams(("parallel", "arbitrary")),
        name=name,
    )(x, w)


def _mm2_kernel(xa_ref, xb_ref, wa_ref, wb_ref, o_ref):
    acc = jnp.dot(xa_ref[...], wa_ref[...].astype(BF16), preferred_element_type=F32)
    acc = acc + jnp.dot(xb_ref[...], wb_ref[...].astype(BF16), preferred_element_type=F32)
    o_ref[...] = acc.astype(o_ref.dtype)


def _matmul2(xa, xb, w, *, tm, tn, name):
    m, ka = xa.shape
    kb = xb.shape[1]
    n = w.shape[1]
    assert ka == kb and w.shape[0] == ka + kb
    return pl.pallas_call(
        _mm2_kernel,
        grid=(m // tm, n // tn),
        in_specs=[pl.BlockSpec((tm, ka), lambda i, j: (i, 0)), pl.BlockSpec((tm, kb), lambda i, j: (i, 0)),
                  pl.BlockSpec((ka, tn), lambda i, j: (0, j)), pl.BlockSpec((kb, tn), lambda i, j: (1, j))],
        out_specs=pl.BlockSpec((tm, tn), lambda i, j: (i, j)),
        out_shape=jax.ShapeDtypeStruct((m, n), F32),
        compiler_params=_cparams(("parallel", "arbitrary")),
        name=name,
    )(xa, xb, w, w)


def _swiglu_kernel(x_ref, wg_ref, wu_ref, o_ref):
    x = x_ref[...]
    g = jnp.dot(x, wg_ref[...].astype(BF16), preferred_element_type=F32)
    u = jnp.dot(x, wu_ref[...].astype(BF16), preferred_element_type=F32)
    o_ref[...] = (g * _sigmoid(g) * u).astype(o_ref.dtype)


def _swiglu_up(x, wg, wu, *, tm, tn):
    m, k = x.shape
    n = wg.shape[1]
    wspec = pl.BlockSpec((k, tn), lambda i, j: (0, j))
    return pl.pallas_call(
        _swiglu_kernel,
        grid=(m // tm, n // tn),
        in_specs=[pl.BlockSpec((tm, k), lambda i, j: (i, 0)), wspec, wspec],
        out_specs=pl.BlockSpec((tm, tn), lambda i, j: (i, j)),
        out_shape=jax.ShapeDtypeStruct((m, n), BF16),
        compiler_params=_cparams(("parallel", "arbitrary")),
        name="ffn_up",
    )(x, wg, wu)


def _shift_rows(u, halo, s):
    ru = pltpu.roll(u, s, axis=0)
    rh = pltpu.roll(halo, s, axis=0)
    top = jnp.where(_iota2(halo.shape, 0) < s, rh, ru[:SUBLANE])
    if u.shape[0] == SUBLANE:
        return top
    return jnp.concatenate([top, ru[SUBLANE:]], axis=0)


def _dn_pre_kernel(p_ref, halo_ref, cache_ref, cw_ref, ba_ref, alog_ref, dtb_ref,
                   qkv_ref, gb_ref, *, heads, dk, hb):
    i = pl.program_id(1)
    j = pl.program_id(2)
    tm = p_ref.shape[1]
    rb = min(tm, 64)

    def strips(scale):
        for h in range(heads):
            cols = slice(h * dk, (h + 1) * dk)
            cw = cw_ref[:, cols]
            for r0 in range(0, tm, rb):
                u = p_ref[0, r0:r0 + rb, cols]
                if r0 == 0:
                    halo = jnp.where(i == 0, cache_ref[0, :, cols], halo_ref[0, :, cols])
                else:
                    halo = p_ref[0, r0 - SUBLANE:r0, cols]
                prev = [_shift_rows(u, halo, sh) for sh in (3, 2, 1)]
                conv = prev[0] * cw[0:1]
                conv = conv + prev[1] * cw[1:2]
                conv = conv + prev[2] * cw[2:3]
                conv = conv + u * cw[3:4]
                s = conv * _sigmoid(conv)
                if scale is not None:
                    s = s * (lax.rsqrt(jnp.sum(s * s, axis=-1, keepdims=True) + L2_EPS) * scale)
                qkv_ref[0, r0:r0 + rb, cols] = s

    @pl.when(j < 2)
    def _():
        strips(jnp.where(j == 0, dk ** -0.5, 1.0).astype(F32))

    @pl.when(j == 2)
    def _():
        strips(None)

    @pl.when(j == 0)
    def _():
        x = ba_ref[0]
        lane = _iota2(x.shape, 1)
        g = -jnp.exp(alog_ref[...]) * _softplus(x + dtb_ref[...])
        full = jnp.where(lane < heads, _sigmoid(x), g)
        for hg in range(heads // hb):
            gb_ref[0, hg] = full if hg == 0 else pltpu.roll(full, LANE - hg * hb, axis=1)


def _dn_pre(p, cache8, conv_w, alog_row, dtb_row, *, heads, dk, hb, c_ba):
    b, t, _ = p.shape
    w = heads * dk
    tm = _pick(t, (256, 128, 64, 32, 16))
    groups = heads // hb
    kern = functools.partial(_dn_pre_kernel, heads=heads, dk=dk, hb=hb)
    return pl.pallas_call(
        kern,
        grid=(b, t // tm, 3),
        in_specs=[
            pl.BlockSpec((1, tm, w), lambda bb, i, j: (bb, i, j)),
            pl.BlockSpec((1, SUBLANE, w), lambda bb, i, j: (bb, jnp.maximum(i * (tm // SUBLANE) - 1, 0), j)),
            pl.BlockSpec((1, SUBLANE, w), lambda bb, i, j: (bb, 0, j)),
            pl.BlockSpec((4, w), lambda bb, i, j: (0, j)),
            pl.BlockSpec((1, tm, LANE), lambda bb, i, j: (bb, i, c_ba // LANE)),
            pl.BlockSpec((1, LANE), lambda bb, i, j: (0, 0)),
            pl.BlockSpec((1, LANE), lambda bb, i, j: (0, 0)),
        ],
        out_specs=[
            pl.BlockSpec((1, tm, w), lambda bb, i, j: (bb, i, j)),
            pl.BlockSpec((1, groups, tm, LANE), lambda bb, i, j: (bb, 0, i, 0)),
        ],
        out_shape=[jax.ShapeDtypeStruct((b, t, 3 * w), F32),
                   jax.ShapeDtypeStruct((b, groups, t, LANE), F32)],
        compiler_params=_cparams(("parallel", "parallel", "arbitrary")),
        name="dn_pre",
    )(p, p, cache8, conv_w, p, alog_row, dtb_row)


def _dn_kernel(q_ref, k_ref, v_ref, gb_ref, z_ref, nw_ref, s0_ref, y_ref, sout_ref, s_scr,
               *, heads, hb, dk, chunk, nch):
    c = pl.program_id(2)
    nc = pl.num_programs(2)

    @pl.when(c == 0)
    def _():
        s_scr[...] = s0_ref[0]

    row = _iota2((chunk, chunk), 0)
    col = _iota2((chunk, chunk), 1)
    causal = row >= col
    strict = row > col
    nw = nw_ref[...]
    eye = (row == col).astype(F32)
    state = {h: s_scr[h] for h in range(hb)}
    applied = {h: 0 for h in range(hb)}
    per_chunk = {}

    def chunk_gates(j):
        if j not in per_chunk:
            gbt = gb_ref[0, 0, j * chunk:(j + 1) * chunk, :]
            gc_all = _cumsum_rows(gbt)
            per_chunk[j] = (gbt, gc_all, jnp.transpose(gc_all))
        return per_chunk[j]

    def program(hs, j):
        rows = slice(j * chunk, (j + 1) * chunk)
        gbt, gc_all, gc_t = chunk_gates(j)
        sl = {h: slice(h * dk, (h + 1) * dk) for h in hs}
        q = {h: q_ref[0, rows, sl[h]] for h in hs}
        k = {h: k_ref[0, rows, sl[h]] for h in hs}
        v = {h: v_ref[0, rows, sl[h]] for h in hs}
        beta = {h: gbt[:, h:h + 1] for h in hs}
        gcol = {h: gc_all[:, heads + h:heads + h + 1] for h in hs}
        glast = {h: gc_all[chunk - 1:chunk, heads + h:heads + h + 1] for h in hs}
        decay = {h: jnp.where(causal, jnp.exp(gcol[h] - gc_t[heads + h:heads + h + 1, :]), 0.0) for h in hs}
        kb = {h: k[h] * beta[h] for h in hs}
        eg = {h: jnp.exp(gcol[h]) for h in hs}
        yield
        m = {h: jnp.where(strict, -_dot_nt(kb[h], k[h]) * decay[h], 0.0) for h in hs}
        qk = {h: jnp.where(causal, _dot_nt(q[h], k[h]) * decay[h], 0.0) for h in hs}
        yield
        t = {h: eye + m[h] for h in hs}
        m = {h: _dot(m[h], m[h]) for h in hs}
        for _ in range(chunk.bit_length() - 3):
            yield
            res = {h: _dot(jnp.concatenate([m[h], t[h]], axis=0), m[h]) for h in hs}
            m = {h: res[h][:chunk] for h in hs}
            t = {h: t[h] + res[h][chunk:] for h in hs}
        yield
        t = {h: t[h] + _dot(t[h], m[h]) for h in hs}
        yield
        sol = {h: _dot(t[h], jnp.concatenate([v[h] * beta[h], kb[h] * eg[h]], axis=1)) for h in hs}
        yield
        assert all(applied[h] == j for h in hs)
        s = {h: state[h] for h in hs}
        v_new = {h: sol[h][:, :dk] - _dot(sol[h][:, dk:], s[h]) for h in hs}
        os = {h: _dot(q[h] * eg[h], s[h]) for h in hs}
        yield
        o = {h: os[h] + _dot(qk[h], v_new[h]) for h in hs}
        for h in hs:
            state[h] = s[h] * jnp.exp(glast[h]) + _dot_tn(k[h] * jnp.exp(glast[h] - gcol[h]), v_new[h])
            applied[h] = j + 1
        yield
        for h in hs:
            z = z_ref[0, rows, sl[h]]
            y_ref[0, rows, sl[h]] = (_rms(o[h], nw) * (z * _sigmoid(z))).astype(y_ref.dtype)

    gsz = _pick(hb, (4, 2, 1))
    _run_staggered([(j * REC_CHUNK_TICKS, program(range(g0, g0 + gsz), j))
                    for j in range(nch) for g0 in range(0, hb, gsz)])
    for h in range(hb):
        s_scr[h] = state[h]

    @pl.when(c == nc - 1)
    def _():
        sout_ref[0] = s_scr[...]


def _dn_recurrence(qkv, gb, p, norm_w, s0, *, heads, hb, dk, chunk, c_z):
    b, t, _ = qkv.shape
    groups = heads // hb
    wb = hb * dk
    nqk = heads * dk // wb
    nch = _pick(t // chunk, (REC_CHUNKS, 2, 1))
    rows = nch * chunk
    kern = functools.partial(_dn_kernel, heads=heads, hb=hb, dk=dk, chunk=chunk, nch=nch)
    return pl.pallas_call(
        kern,
        grid=(b, groups, t // rows),
        in_specs=[
            pl.BlockSpec((1, rows, wb), lambda bb, g, c: (bb, c, g)),
            pl.BlockSpec((1, rows, wb), lambda bb, g, c: (bb, c, nqk + g)),
            pl.BlockSpec((1, rows, wb), lambda bb, g, c: (bb, c, 2 * nqk + g)),
            pl.BlockSpec((1, 1, rows, LANE), lambda bb, g, c: (bb, g, c, 0)),
            pl.BlockSpec((1, rows, wb), lambda bb, g, c: (bb, c, c_z // wb + g)),
            pl.BlockSpec((1, dk), lambda bb, g, c: (0, 0)),
            pl.BlockSpec((1, hb, dk, dk), lambda bb, g, c: (bb, g, 0, 0)),
        ],
        out_specs=[
            pl.BlockSpec((1, rows, wb), lambda bb, g, c: (bb, c, g)),
            pl.BlockSpec((1, hb, dk, dk), lambda bb, g, c: (bb, g, 0, 0)),
        ],
        out_shape=[jax.ShapeDtypeStruct((b, t, heads * dk), BF16),
                   jax.ShapeDtypeStruct((b, heads, dk, dk), F32)],
        scratch_shapes=[pltpu.VMEM((hb, dk, dk), F32)],
        compiler_params=_cparams(("parallel", "parallel", "arbitrary")),
        name="dn_recurrence",
    )(qkv, qkv, qkv, gb, p, norm_w.reshape(1, dk), s0)


def _pair_sums(x, m0):
    s0 = jnp.sum(jnp.where(m0, x, 0.0), axis=-1, keepdims=True)
    s1 = jnp.sum(jnp.where(m0, 0.0, x), axis=-1, keepdims=True)
    return jnp.where(m0, s0, s1)


def _token_shift(p_ref, halo_ref, cache_ref, mu_ref, first):
    u = p_ref[0]
    halo = jnp.where(first, cache_ref[0], halo_ref[0])
    return u + mu_ref[...] * (_shift_rows(u, halo, 1) - u)


def _rw_pre_kernel(p_ref, halo_ref, cache_ref, mu_ref, pl_ref, halol_ref, cachel_ref, mul_ref,
                   w0_ref, w2_ref, a0_ref, a2_ref, g2_ref, kk_ref, ka_ref,
                   r_ref, k_ref, v_ref, lw_ref, av_ref, bv_ref, gate_ref, *, rw, lw_pad, la_pad, lg_pad):
    first = pl.program_id(1) == 0
    x = _token_shift(p_ref, halo_ref, cache_ref, mu_ref, first)
    xl = _token_shift(pl_ref, halol_ref, cachel_ref, mul_ref, first)
    r = x[:, :rw]
    kr = x[:, rw:2 * rw]
    vr = x[:, 2 * rw:3 * rw]
    xw = xl[:, :lw_pad]
    xa = xl[:, lw_pad:lw_pad + la_pad]
    xg = xl[:, lw_pad + la_pad:lw_pad + la_pad + lg_pad]
    w_log = -_softplus(-(w0_ref[...] + _dot(jnp.tanh(xw), w2_ref[...]))) - 0.5
    a = _sigmoid(a0_ref[...] + _dot(xa, a2_ref[...]))
    gate_ref[0] = _dot(_sigmoid(xg), g2_ref[...])
    r_ref[0] = r
    v_ref[0] = vr
    lw_ref[0] = -jnp.exp(w_log)
    k_ref[0] = kr * (1.0 + (a - 1.0) * ka_ref[...])
    kkr = kr * kk_ref[...]
    m0 = _iota2((1, LANE), 1) < LANE // 2
    for jb in range(rw // LANE):
        sl = slice(jb * LANE, (jb + 1) * LANE)
        blk = kkr[:, sl]
        kk = blk * lax.rsqrt(_pair_sums(blk * blk, m0) + L2_EPS)
        av_ref[0, :, sl] = -kk
        bv_ref[0, :, sl] = kk * a[:, sl]


def _rw_pre(p, cache_rkv8, cache_lo8, mu_rkv, mu_lo, w0, w2p, a0, a2p, g2p, k_k, k_a, *, rw, c_rkv, c_lo):
    b, t, _ = p.shape
    tm = _pick(t, (128, 64, 32, 16))
    lw_pad, la_pad, lg_pad = w2p.shape[0], a2p.shape[0], g2p.shape[0]
    lblk = lw_pad + la_pad + lg_pad
    kern = functools.partial(_rw_pre_kernel, rw=rw, lw_pad=lw_pad, la_pad=la_pad, lg_pad=lg_pad)
    full = lambda shape: pl.BlockSpec(shape, lambda bb, i: (0,) * len(shape))
    row = pl.BlockSpec((1, tm, rw), lambda bb, i: (bb, i, 0))

    def shifted(width, cb):
        return [pl.BlockSpec((1, tm, width), lambda bb, i: (bb, i, cb)),
                pl.BlockSpec((1, SUBLANE, width),
                             lambda bb, i: (bb, jnp.maximum(i * (tm // SUBLANE) - 1, 0), cb)),
                pl.BlockSpec((1, SUBLANE, width), lambda bb, i: (bb, 0, 0)),
                full((1, width))]

    return pl.pallas_call(
        kern,
        grid=(b, t // tm),
        in_specs=shifted(3 * rw, c_rkv // (3 * rw)) + shifted(lblk, c_lo // lblk) + [
            full((1, rw)), full(w2p.shape), full((1, rw)), full(a2p.shape), full(g2p.shape),
            full((1, rw)), full((1, rw)),
        ],
        out_specs=[row] * 7,
        out_shape=[jax.ShapeDtypeStruct((b, t, rw), F32)] * 7,
        compiler_params=_cparams(("parallel", "arbitrary")),
        name="rw_pre",
    )(p, p, cache_rkv8, mu_rkv, p, p, cache_lo8, mu_lo, w0, w2p, a0, a2p, g2p, k_k, k_a)


def _rw_kernel(r_ref, k_ref, v_ref, lw_ref, av_ref, bv_ref, gate_ref, rk_ref, lnw_ref, lnb_ref, s0_ref,
               y_ref, sout_ref, s_scr, *, pb, chunk, nch):
    c = pl.program_id(2)
    nc = pl.num_programs(2)
    hn = LANE // 2

    @pl.when(c == 0)
    def _():
        s_scr[...] = s0_ref[0]

    c2 = 2 * chunk
    m0 = _iota2((1, LANE), 1) < hn
    m1 = jnp.logical_not(m0)
    blockmask = (_iota2((LANE, LANE), 0) < hn) == (_iota2((LANE, LANE), 1) < hn)
    row2 = _iota2((chunk, c2), 0)
    col2 = _iota2((chunk, c2), 1) & (chunk - 1)
    strict2 = row2 > col2
    eye2 = (row2 == col2).astype(F32)
    incl4 = _iota2((chunk, 2 * c2), 0) >= (_iota2((chunk, 2 * c2), 1) & (chunk - 1))
    bd = (_iota2((c2, c2), 0) < chunk) == (_iota2((c2, c2), 1) < chunk)

    def by_head(x):
        return jnp.concatenate([jnp.where(m0, x, 0.0), jnp.where(m1, x, 0.0)], axis=0)

    def blockdiag(p2):
        return jnp.where(bd, jnp.concatenate([p2, p2], axis=0), 0.0)

    state = {p: s_scr[p] for p in range(pb)}
    applied = {p: 0 for p in range(pb)}

    def program(ps, j):
        rows = slice(j * chunk, (j + 1) * chunk)
        lanes = slice(ps[0] * LANE, (ps[-1] + 1) * LANE)
        sl = {p: slice(p * LANE, (p + 1) * LANE) for p in ps}
        loc = {p: slice((p - ps[0]) * LANE, (p - ps[0] + 1) * LANE) for p in ps}
        r = {p: r_ref[0, rows, sl[p]] for p in ps}
        k = {p: k_ref[0, rows, sl[p]] for p in ps}
        v = {p: v_ref[0, rows, sl[p]] for p in ps}
        lw = {p: lw_ref[0, rows, sl[p]] for p in ps}
        bv = {p: bv_ref[0, rows, sl[p]] for p in ps}
        cw_all = _cumsum_rows(lw_ref[0, rows, lanes])
        yield
        cw = {p: cw_all[:, loc[p]] for p in ps}
        tot = {p: cw[p][chunk - 1:chunk, :] for p in ps}
        e_neg = {p: jnp.exp(-cw[p]) for p in ps}
        e_end = {p: jnp.exp(tot[p] - cw[p]) for p in ps}
        lhs = {p: jnp.concatenate([av_ref[0, rows, sl[p]] * jnp.exp(cw[p] - lw[p]), r[p] * jnp.exp(cw[p])],
                                  axis=0) for p in ps}
        rhs_g = {p: jnp.concatenate([by_head(bv[p] * e_neg[p]), by_head(k[p] * e_neg[p])], axis=0) for p in ps}
        rhs_s = {p: jnp.concatenate([bv[p] * e_end[p], k[p] * e_end[p]], axis=0) for p in ps}
        vh = {p: by_head(v[p]) for p in ps}
        g = {p: _dot_nt(lhs[p], rhs_g[p]) for p in ps}
        yield
        avs = {p: _dot(jnp.where(strict2, g[p][:chunk, c2:], 0.0), vh[p]) for p in ps}
        m = {p: jnp.where(strict2, g[p][:chunk, :c2], 0.0) for p in ps}
        t2 = {p: eye2 + m[p] for p in ps}
        m = {p: _dot(m[p], blockdiag(m[p])) for p in ps}
        for _ in range(chunk.bit_length() - 3):
            yield
            res = {p: _dot(jnp.concatenate([m[p], t2[p]], axis=0), blockdiag(m[p])) for p in ps}
            m = {p: res[p][:chunk] for p in ps}
            t2 = {p: t2[p] + res[p][chunk:] for p in ps}
        yield
        t2 = {p: t2[p] + _dot(t2[p], blockdiag(m[p])) for p in ps}
        assert all(applied[p] == j for p in ps)
        s = {p: state[p] for p in ps}
        sr = {p: _dot_nt(lhs[p], s[p]) for p in ps}
        yield
        u = {p: _dot(t2[p], by_head(sr[p][:chunk] + avs[p])) for p in ps}
        yield
        yr = {p: _dot(jnp.where(incl4, g[p][chunk:], 0.0), jnp.concatenate([by_head(u[p]), vh[p]], axis=0))
              for p in ps}
        for p in ps:
            state[p] = jnp.where(blockmask, s[p] * jnp.exp(tot[p])
                                 + _dot_tn(jnp.concatenate([u[p], v[p]], axis=0), rhs_s[p]), 0.0)
            applied[p] = j + 1
        yield
        for p in ps:
            y = sr[p][chunk:] + yr[p]
            mean = _pair_sums(y, m0) * (1.0 / hn)
            d = y - mean
            var = _pair_sums(d * d, m0) * (1.0 / hn)
            yn = d * lax.rsqrt(var + RW_GN_EPS) * lnw_ref[:, sl[p]] + lnb_ref[:, sl[p]]
            bonus = _pair_sums(r[p] * k[p] * rk_ref[:, sl[p]], m0) * v[p]
            y_ref[0, rows, sl[p]] = ((yn + bonus) * gate_ref[0, rows, sl[p]]).astype(y_ref.dtype)

    gsz = _pick(pb, (4, 2, 1))
    _run_staggered([(j * REC_CHUNK_TICKS, program(range(g0, g0 + gsz), j))
                    for j in range(nch) for g0 in range(0, pb, gsz)])
    for p in range(pb):
        s_scr[p] = state[p]

    @pl.when(c == nc - 1)
    def _():
        sout_ref[0] = s_scr[...]


def _rw_recurrence(r, k, v, lw, av, bv, gate, r_k, ln_w, ln_b, s0p, *, pb, chunk):
    b, t, rw = r.shape
    pairs = rw // LANE
    groups = pairs // pb
    wb = pb * LANE
    nch = _pick(t // chunk, (REC_CHUNKS, 2, 1))
    kern = functools.partial(_rw_kernel, pb=pb, chunk=chunk, nch=nch)
    tile = pl.BlockSpec((1, nch * chunk, wb), lambda bb, g, c: (bb, c, g))
    par = pl.BlockSpec((1, wb), lambda bb, g, c: (0, g))
    st = pl.BlockSpec((1, pb, LANE, LANE), lambda bb, g, c: (bb, g, 0, 0))
    return pl.pallas_call(
        kern,
        grid=(b, groups, t // (nch * chunk)),
        in_specs=[tile] * 7 + [par] * 3 + [st],
        out_specs=[tile, st],
        out_shape=[jax.ShapeDtypeStruct((b, t, rw), BF16),
                   jax.ShapeDtypeStruct((b, pairs, LANE, LANE), F32)],
        scratch_shapes=[pltpu.VMEM((pb, LANE, LANE), F32)],
        compiler_params=_cparams(("parallel", "parallel", "arbitrary")),
        name="rw_recurrence",
    )(r, k, v, lw, av, bv, gate, r_k, ln_w, ln_b, s0p)


def _pad_cols(a, n):
    return jnp.pad(a, [(0, 0)] * (a.ndim - 1) + [(0, n - a.shape[-1])])


def _pad_rows(a, n):
    return jnp.pad(a, [(0, n - a.shape[0])] + [(0, 0)] * (a.ndim - 1))


def _prepare(w):
    heads = w["dn_a_log"].shape[-1]
    dk = w["dn_norm_w"].shape[-1]
    qkv_w = w["dn_conv_w"].shape[-1]
    v_w = heads * dk
    assert qkv_w == 3 * v_w and dk == LANE
    rw_heads, rw_head = w["rw_r_k"].shape
    assert rw_head == LANE // 2
    rw = rw_heads * rw_head
    lw_n, la_n, lg_n = w["rw_w2"].shape[0], w["rw_a2"].shape[0], w["rw_g2"].shape[0]
    lw_pad, la_pad, lg_pad = (_round_up(n, LANE) for n in (lw_n, la_n, lg_n))
    o1 = qkv_w
    o2 = o1 + v_w
    o3 = o2 + heads
    o4 = o3 + heads
    s3 = o4 + 3 * rw
    s4 = s3 + lw_n
    s5 = s4 + la_n
    w_in = w["w_in"]
    lblk = lw_pad + la_pad + lg_pad
    assert o1 % (3 * rw) == 0
    c_rkv = o1
    c_z = c_rkv + 3 * rw
    c_lo = _round_up(c_z + v_w, lblk)
    c_ba = c_lo + lblk
    n_pad = _round_up(c_ba + LANE, IN_TN)
    d = w_in.shape[0]
    parts = [w_in[:, :o1], w_in[:, o4:s3], w_in[:, o1:o2], jnp.zeros((d, c_lo - c_z - v_w), F32),
             _pad_cols(w_in[:, s3:s4], lw_pad), _pad_cols(w_in[:, s4:s5], la_pad),
             _pad_cols(w_in[:, s5:], lg_pad), _pad_cols(w_in[:, o2:o4], n_pad - c_ba)]
    w_in_p = jnp.concatenate(parts, axis=1).astype(BF16)

    def lora_cols(a):
        return jnp.concatenate([_pad_cols(a[..., :lw_n], lw_pad), _pad_cols(a[..., lw_n:lw_n + la_n], la_pad),
                                _pad_cols(a[..., lw_n + la_n:], lg_pad)], axis=-1)

    lane_pad = lambda a: jnp.pad(a, (heads, LANE - 2 * heads)).reshape(1, LANE)
    return dict(
        heads=heads, dk=dk, rw=rw, rw_heads=rw_heads, c_rkv=c_rkv, c_z=c_z, c_lo=c_lo, c_ba=c_ba,
        dims=(o1, lw_n, la_n, lg_n, lw_pad, la_pad, lblk), lora_cols=lora_cols,
        w_in_p=w_in_p,
        alog_row=lane_pad(w["dn_a_log"]), dtb_row=lane_pad(w["dn_dt_bias"]),
        mu_rkv=w["rw_mu"][:3 * rw].reshape(1, 3 * rw),
        mu_lo=lora_cols(w["rw_mu"][3 * rw:]).reshape(1, lblk),
        w2p=_pad_rows(w["rw_w2"], lw_pad).astype(BF16),
        a2p=_pad_rows(w["rw_a2"], la_pad).astype(BF16),
        g2p=_pad_rows(w["rw_g2"], lg_pad).astype(BF16),
        w_down=w["w_down"].astype(BF16),
    )


def _layer(x, dn_state, dn_conv, rw_state, rw_shift, w, pp):
    b, t, d = x.shape
    m = b * t
    heads, dk, rw, rw_heads = pp["heads"], pp["dk"], pp["rw"], pp["rw_heads"]
    c_rkv, c_z, c_lo, c_ba = pp["c_rkv"], pp["c_z"], pp["c_lo"], pp["c_ba"]
    o1, lw_n, la_n, lg_n, lw_pad, la_pad, lblk = pp["dims"]
    chunk = 64 if t % 64 == 0 else t
    assert chunk & (chunk - 1) == 0 and chunk >= 2 * SUBLANE
    hb = 16 if heads % 16 == 0 else heads
    pairs = rw // LANE
    pb = 16 if pairs % 16 == 0 else pairs
    tm = _pick(m, (1024, 512, 256, 128))

    xf = x.reshape(m, d)
    h = _rms_bf16(xf, w["g_mix_pre"])
    n_pad = pp["w_in_p"].shape[1]
    p = _matmul(h, pp["w_in_p"], tm=tm, tn=IN_TN, out_dtype=F32, name="in_proj").reshape(b, t, n_pad)

    cache8 = jnp.pad(dn_conv.astype(F32), ((0, 0), (SUBLANE - dn_conv.shape[1], 0), (0, 0)))
    qkv, gb = _dn_pre(p, cache8, w["dn_conv_w"], pp["alog_row"], pp["dtb_row"],
                      heads=heads, dk=dk, hb=hb, c_ba=c_ba)
    y_a, new_dn_state = _dn_recurrence(qkv, gb, p, w["dn_norm_w"], dn_state.astype(F32),
                                       heads=heads, hb=hb, dk=dk, chunk=chunk, c_z=c_z)
    new_dn_conv = p[:, t - dn_conv.shape[1]:, :o1]

    shift = rw_shift.astype(F32)
    front = ((0, 0), (SUBLANE - 1, 0), (0, 0))
    r, k, v, lw, av, bv, gate = _rw_pre(p, jnp.pad(shift[..., :3 * rw], front),
                                        jnp.pad(pp["lora_cols"](shift[..., 3 * rw:]), front),
                                        pp["mu_rkv"], pp["mu_lo"], w["rw_w0"].reshape(1, rw), pp["w2p"],
                                        w["rw_a0"].reshape(1, rw), pp["a2p"], pp["g2p"],
                                        w["rw_k_k"].reshape(1, rw), w["rw_k_a"].reshape(1, rw),
                                        rw=rw, c_rkv=c_rkv, c_lo=c_lo)
    hn = LANE // 2
    s4 = rw_state.astype(F32).reshape(b, pairs, 2, hn, hn)
    zeros = jnp.zeros_like(s4[:, :, 0])
    s0p = jnp.concatenate([jnp.concatenate([s4[:, :, 0], zeros], axis=-1),
                           jnp.concatenate([zeros, s4[:, :, 1]], axis=-1)], axis=-2)
    y_b, sp = _rw_recurrence(r, k, v, lw, av, bv, gate, w["rw_r_k"].reshape(1, rw),
                             w["rw_ln_w"].reshape(1, rw), w["rw_ln_b"].reshape(1, rw), s0p,
                             pb=pb, chunk=chunk)
    new_rw_state = jnp.stack([sp[:, :, :hn, :hn], sp[:, :, hn:, hn:]], axis=2).reshape(b, rw_heads, hn, hn)
    last = p[:, t - 1:]
    new_rw_shift = jnp.concatenate(
        [last[..., c_rkv:c_rkv + 3 * rw], last[..., c_lo:c_lo + lw_n],
         last[..., c_lo + lw_pad:c_lo + lw_pad + la_n],
         last[..., c_lo + lw_pad + la_pad:c_lo + lw_pad + la_pad + lg_n]], axis=-1)

    mixo = _matmul2(y_a.reshape(m, heads * dk), y_b.reshape(m, rw), w["w_out"],
                    tm=tm, tn=_pick(d, (512, 256, 128)), name="out_proj")
    x1, h2 = _resid_rms(xf, mixo, w["g_mix_post"], w["g_ffn_pre"])
    dff = w["w_gate"].shape[1]
    f = _swiglu_up(h2, w["w_gate"], w["w_up"], tm=tm, tn=_pick(dff, (256, 128)))
    fo = _matmul(f, pp["w_down"], tm=_pick(m, (512, 256, 128)), tn=_pick(d, (256, 128)),
                 out_dtype=F32, name="ffn_down")
    out = _resid_final(x1, fo, w["g_ffn_post"]).reshape(b, t, d)
    return out, (new_dn_state, new_dn_conv, new_rw_state, new_rw_shift)


_WEIGHT_NAMES = ("g_mix_pre", "g_mix_post", "w_in", "dn_conv_w", "dn_a_log", "dn_dt_bias", "dn_norm_w",
                 "rw_mu", "rw_w0", "rw_w2", "rw_a0", "rw_a2", "rw_g2", "rw_k_k", "rw_k_a", "rw_r_k",
                 "rw_ln_w", "rw_ln_b", "w_out", "g_ffn_pre", "g_ffn_post", "w_gate", "w_up", "w_down")


def kernel(x_prompt, x_sample, state_dn, cache_dn_conv, state_rwkv, cache_rwkv_shift,
           g_mix_pre, g_mix_post, w_in, dn_conv_w, dn_a_log, dn_dt_bias, dn_norm_w,
           rw_mu, rw_w0, rw_w2, rw_a0, rw_a2, rw_g2, rw_k_k, rw_k_a, rw_r_k, rw_ln_w, rw_ln_b,
           w_out, g_ffn_pre, g_ffn_post, w_gate, w_up, w_down):
    stacked = (g_mix_pre, g_mix_post, w_in, dn_conv_w, dn_a_log, dn_dt_bias, dn_norm_w,
               rw_mu, rw_w0, rw_w2, rw_a0, rw_a2, rw_g2, rw_k_k, rw_k_a, rw_r_k, rw_ln_w, rw_ln_b,
               w_out, g_ffn_pre, g_ffn_post, w_gate, w_up, w_down)
    depth = w_in.shape[0]
    bp = x_prompt.shape[0]
    dt = x_prompt.dtype
    yp, ys = x_prompt, x_sample
    outs_p, outs_s = [], []
    for l in range(depth):
        w = {n: a[l] for n, a in zip(_WEIGHT_NAMES, stacked)}
        pp = _prepare(w)
        heads, dk, rw_heads = pp["heads"], pp["dk"], pp["rw_heads"]
        hn = LANE // 2
        yp, st_p = _layer(yp,
                          jnp.zeros((bp, heads, dk, dk), dt),
                          jnp.zeros((bp, cache_dn_conv.shape[2], cache_dn_conv.shape[3]), dt),
                          jnp.zeros((bp, rw_heads, hn, hn), dt),
                          jnp.zeros((bp, 1, cache_rwkv_shift.shape[3]), dt), w, pp)
        ys, st_s = _layer(ys, state_dn[l], cache_dn_conv[l], state_rwkv[l], cache_rwkv_shift[l], w, pp)
        outs_p.append(st_p)
        outs_s.append(st_s)
    stack = lambda outs, i: jnp.stack([o[i] for o in outs])
    return (yp, ys,
            stack(outs_p, 0), stack(outs_p, 1), stack(outs_p, 2), stack(outs_p, 3),
            stack(outs_s, 0), stack(outs_s, 1), stack(outs_s, 2), stack(outs_s, 3))
```

```python
import functools
import math

import jax
import jax.numpy as jnp
from jax import lax
from jax.experimental import pallas as pl
from jax.experimental.pallas import tpu as pltpu

F32 = jnp.float32
BF16 = jnp.bfloat16

LANE = 128
SUBLANE = 8
VMEM_LIMIT_BYTES = 56 * 2**20
NORM_EPS = 1e-6
L2_EPS = 1e-6
RW_GN_EPS = 64e-5
IN_TN = 512
REC_CHUNKS = 4
REC_CHUNK_TICKS = 3


def _cparams(sem):
    return pltpu.CompilerParams(dimension_semantics=sem, vmem_limit_bytes=VMEM_LIMIT_BYTES)


def _round_up(x, m):
    return -(-x // m) * m


def _pick(n, cands):
    for c in cands:
        if n % c == 0:
            return c
    return n


def _dot(a, b):
    return jnp.dot(a.astype(BF16), b.astype(BF16), preferred_element_type=F32)


def _dot_nt(a, b):
    return lax.dot_general(a.astype(BF16), b.astype(BF16), (((1,), (1,)), ((), ())),
                           preferred_element_type=F32)


def _dot_tn(a, b):
    return lax.dot_general(a.astype(BF16), b.astype(BF16), (((0,), (0,)), ((), ())),
                           preferred_element_type=F32)


def _cumsum_rows(x):
    c = x.shape[0]
    tri = (_iota2((c, c), 0) >= _iota2((c, c), 1)).astype(BF16)
    hi = x.astype(BF16)
    r1 = x - hi.astype(F32)
    mid = r1.astype(BF16)
    lo = (r1 - mid.astype(F32)).astype(BF16)
    dot = lambda p: jnp.dot(tri, p, preferred_element_type=F32)
    return dot(hi) + (dot(mid) + dot(lo))


def _sigmoid(x):
    return 0.5 * jnp.tanh(0.5 * x) + 0.5


def _softplus(x):
    return jnp.maximum(x, 0.0) + jnp.log(1.0 + jnp.exp(-jnp.abs(x)))


def _iota2(shape, dim):
    return lax.broadcasted_iota(jnp.int32, shape, dim)


def _run_staggered(programs):
    live = list(programs)
    tick = 0
    while live:
        still = []
        for start, prog in live:
            if tick >= start:
                try:
                    next(prog)
                except StopIteration:
                    continue
            still.append((start, prog))
        live = still
        tick += 1


def _rms(x, g):
    return x * lax.rsqrt(jnp.mean(x * x, axis=-1, keepdims=True) + NORM_EPS) * g


def _rms_kernel(x_ref, g_ref, o_ref):
    o_ref[...] = _rms(x_ref[...], g_ref[...]).astype(o_ref.dtype)


def _rms_bf16(x, g):
    m, d = x.shape
    tm = _pick(m, (256, 128))
    return pl.pallas_call(
        _rms_kernel,
        grid=(m // tm,),
        in_specs=[pl.BlockSpec((tm, d), lambda i: (i, 0)), pl.BlockSpec((1, d), lambda i: (0, 0))],
        out_specs=pl.BlockSpec((tm, d), lambda i: (i, 0)),
        out_shape=jax.ShapeDtypeStruct((m, d), BF16),
        compiler_params=_cparams(("parallel",)),
        name="rms_pre",
    )(x, g.reshape(1, d))


def _resid_rms_kernel(x_ref, y_ref, gp_ref, gn_ref, x1_ref, h_ref):
    x1 = x_ref[...] + _rms(y_ref[...].astype(F32), gp_ref[...])
    x1_ref[...] = x1
    h_ref[...] = _rms(x1, gn_ref[...]).astype(h_ref.dtype)


def _resid_rms(x, y, g_post, g_next):
    m, d = x.shape
    tm = _pick(m, (256, 128))
    row = pl.BlockSpec((tm, d), lambda i: (i, 0))
    par = pl.BlockSpec((1, d), lambda i: (0, 0))
    return pl.pallas_call(
        _resid_rms_kernel,
        grid=(m // tm,),
        in_specs=[row, row, par, par],
        out_specs=[row, row],
        out_shape=[jax.ShapeDtypeStruct((m, d), F32), jax.ShapeDtypeStruct((m, d), BF16)],
        compiler_params=_cparams(("parallel",)),
        name="resid_rms",
    )(x, y, g_post.reshape(1, d), g_next.reshape(1, d))


def _resid_final_kernel(x_ref, y_ref, g_ref, o_ref):
    o_ref[...] = x_ref[...] + _rms(y_ref[...].astype(F32), g_ref[...])


def _resid_final(x, y, g):
    m, d = x.shape
    tm = _pick(m, (256, 128))
    row = pl.BlockSpec((tm, d), lambda i: (i, 0))
    return pl.pallas_call(
        _resid_final_kernel,
        grid=(m // tm,),
        in_specs=[row, row, pl.BlockSpec((1, d), lambda i: (0, 0))],
        out_specs=row,
        out_shape=jax.ShapeDtypeStruct((m, d), F32),
        compiler_params=_cparams(("parallel",)),
        name="resid_final",
    )(x, y, g.reshape(1, d))


def _mm_kernel(x_ref, w_ref, o_ref):
    o_ref[...] = jnp.dot(x_ref[...], w_ref[...], preferred_element_type=F32).astype(o_ref.dtype)


def _matmul(x, w, *, tm, tn, out_dtype, name):
    m, k = x.shape
    n = w.shape[1]
    return pl.pallas_call(
        _mm_kernel,
        grid=(m // tm, n // tn),
        in_specs=[pl.BlockSpec((tm, k), lambda i, j: (i, 0)), pl.BlockSpec((k, tn), lambda i, j: (0, j))],
        out_specs=pl.BlockSpec((tm, tn), lambda i, j: (i, j)),
        out_shape=jax.ShapeDtypeStruct((m, n), out_dtype),
        compiler_params=_cparams(("parallel", "arbitrary")),
        name=name,
    )(x, w)


def _mm_w32_kernel(x_ref, w_ref, o_ref, *, n_valid):
    w = w_ref[...]
    tn = w.shape[1]
    if n_valid % tn:
        w = jnp.where(pl.program_id(1) * tn + _iota2(w.shape, 1) < n_valid, w, 0.0)
    o_ref[...] = jnp.dot(x_ref[...], w.astype(BF16), preferred_element_type=F32)


def _matmul_w32(x, w, *, tm, tn, name):
    m, k = x.shape
    n = w.shape[1]
    nt = pl.cdiv(n, tn)
    return pl.pallas_call(
        functools.partial(_mm_w32_kernel, n_valid=n),
        grid=(m // tm, nt),
        in_specs=[pl.BlockSpec((tm, k), lambda i, j: (i, 0)), pl.BlockSpec((k, tn), lambda i, j: (0, j))],
        out_specs=pl.BlockSpec((tm, tn), lambda i, j: (i, j)),
        out_shape=jax.ShapeDtypeStruct((m, nt * tn), F32),
        compiler_params=_cparams(("parallel", "arbitrary")),
        name=name,
    )(x, w)


def _mm2_kernel(xa_ref, xb_ref, wa_ref, wb_ref, o_ref):
    acc = jnp.dot(xa_ref[...], wa_ref[...].astype(BF16), preferred_element_type=F32)
    acc = acc + jnp.dot(xb_ref[...], wb_ref[...].astype(BF16), preferred_element_type=F32)
    o_ref[...] = acc.astype(o_ref.dtype)


def _matmul2(xa, xb, w, *, tm, tn, name):
    m, ka = xa.shape
    kb = xb.shape[1]
    n = w.shape[1]
    assert ka == kb and w.shape[0] == ka + kb
    return pl.pallas_call(
        _mm2_kernel,
        grid=(m // tm, n // tn),
        in_specs=[pl.BlockSpec((tm, ka), lambda i, j: (i, 0)), pl.BlockSpec((tm, kb), lambda i, j: (i, 0)),
                  pl.BlockSpec((ka, tn), lambda i, j: (0, j)), pl.BlockSpec((kb, tn), lambda i, j: (1, j))],
        out_specs=pl.BlockSpec((tm, tn), lambda i, j: (i, j)),
        out_shape=jax.ShapeDtypeStruct((m, n), BF16),
        compiler_params=_cparams(("parallel", "arbitrary")),
        name=name,
    )(xa, xb, w, w)


def _swiglu_kernel(x_ref, wg_ref, wu_ref, o_ref):
    x = x_ref[...]
    g = jnp.dot(x, wg_ref[...].astype(BF16), preferred_element_type=F32)
    u = jnp.dot(x, wu_ref[...].astype(BF16), preferred_element_type=F32)
    o_ref[...] = (g * _sigmoid(g) * u).astype(o_ref.dtype)


def _swiglu_up(x, wg, wu, *, tm, tn):
    m, k = x.shape
    n = wg.shape[1]
    wspec = pl.BlockSpec((k, tn), lambda i, j: (0, j))
    return pl.pallas_call(
        _swiglu_kernel,
        grid=(m // tm, n // tn),
        in_specs=[pl.BlockSpec((tm, k), lambda i, j: (i, 0)), wspec, wspec],
        out_specs=pl.BlockSpec((tm, tn), lambda i, j: (i, j)),
        out_shape=jax.ShapeDtypeStruct((m, n), BF16),
        compiler_params=_cparams(("parallel", "arbitrary")),
        name="ffn_up",
    )(x, wg, wu)


def _shift_rows(u, halo, s):
    ru = pltpu.roll(u, s, axis=0)
    rh = pltpu.roll(halo, s, axis=0)
    top = jnp.where(_iota2(halo.shape, 0) < s, rh, ru[:SUBLANE])
    if u.shape[0] == SUBLANE:
        return top
    return jnp.concatenate([top, ru[SUBLANE:]], axis=0)


def _dn_pre_kernel(p_ref, halo_ref, cache_ref, cw_ref, ba_ref, alog_ref, dtb_ref,
                   qkv_ref, gb_ref, *, heads, dk, hb):
    i = pl.program_id(1)
    j = pl.program_id(2)
    tm = p_ref.shape[1]
    rb = min(tm, 64)

    def strips(scale):
        for h in range(heads):
            cols = slice(h * dk, (h + 1) * dk)
            cw = cw_ref[:, cols]
            for r0 in range(0, tm, rb):
                u = p_ref[0, r0:r0 + rb, cols]
                if r0 == 0:
                    halo = jnp.where(i == 0, cache_ref[0, :, cols], halo_ref[0, :, cols])
                else:
                    halo = p_ref[0, r0 - SUBLANE:r0, cols]
                prev = [_shift_rows(u, halo, sh) for sh in (3, 2, 1)]
                conv = prev[0] * cw[0:1]
                conv = conv + prev[1] * cw[1:2]
                conv = conv + prev[2] * cw[2:3]
                conv = conv + u * cw[3:4]
                s = conv * _sigmoid(conv)
                if scale is not None:
                    s = s * (lax.rsqrt(jnp.sum(s * s, axis=-1, keepdims=True) + L2_EPS) * scale)
                qkv_ref[0, r0:r0 + rb, cols] = s

    @pl.when(j < 2)
    def _():
        strips(jnp.where(j == 0, dk ** -0.5, 1.0).astype(F32))

    @pl.when(j == 2)
    def _():
        strips(None)

    @pl.when(j == 0)
    def _():
        x = ba_ref[0]
        lane = _iota2(x.shape, 1)
        g = -jnp.exp(alog_ref[...]) * _softplus(x + dtb_ref[...])
        full = jnp.where(lane < heads, _sigmoid(x), g)
        for hg in range(heads // hb):
            gb_ref[0, hg] = full if hg == 0 else pltpu.roll(full, LANE - hg * hb, axis=1)


def _dn_pre(p, cache8, conv_w, alog_row, dtb_row, *, heads, dk, hb, c_ba):
    b, t, _ = p.shape
    w = heads * dk
    tm = _pick(t, (256, 128, 64, 32, 16))
    groups = heads // hb
    kern = functools.partial(_dn_pre_kernel, heads=heads, dk=dk, hb=hb)
    return pl.pallas_call(
        kern,
        grid=(b, t // tm, 3),
        in_specs=[
            pl.BlockSpec((1, tm, w), lambda bb, i, j: (bb, i, j)),
            pl.BlockSpec((1, SUBLANE, w), lambda bb, i, j: (bb, jnp.maximum(i * (tm // SUBLANE) - 1, 0), j)),
            pl.BlockSpec((1, SUBLANE, w), lambda bb, i, j: (bb, 0, j)),
            pl.BlockSpec((4, w), lambda bb, i, j: (0, j)),
            pl.BlockSpec((1, tm, LANE), lambda bb, i, j: (bb, i, c_ba // LANE)),
            pl.BlockSpec((1, LANE), lambda bb, i, j: (0, 0)),
            pl.BlockSpec((1, LANE), lambda bb, i, j: (0, 0)),
        ],
        out_specs=[
            pl.BlockSpec((1, tm, w), lambda bb, i, j: (bb, i, j)),
            pl.BlockSpec((1, groups, tm, LANE), lambda bb, i, j: (bb, 0, i, 0)),
        ],
        out_shape=[jax.ShapeDtypeStruct((b, t, 3 * w), F32),
                   jax.ShapeDtypeStruct((b, groups, t, LANE), F32)],
        compiler_params=_cparams(("parallel", "parallel", "arbitrary")),
        name="dn_pre",
    )(p, p, cache8, conv_w, p, alog_row, dtb_row)


def _dn_kernel(q_ref, k_ref, v_ref, gb_ref, z_ref, nw_ref, s0_ref, y_ref, sout_ref, s_scr,
               *, heads, hb, dk, chunk, nch):
    c = pl.program_id(2)
    nc = pl.num_programs(2)

    @pl.when(c == 0)
    def _():
        s_scr[...] = s0_ref[0]

    row = _iota2((chunk, chunk), 0)
    col = _iota2((chunk, chunk), 1)
    causal = row >= col
    strict = row > col
    nw = nw_ref[...]
    eye = (row == col).astype(F32)
    state = {h: s_scr[h] for h in range(hb)}
    applied = {h: 0 for h in range(hb)}
    per_chunk = {}

    def chunk_gates(j):
        if j not in per_chunk:
            gbt = gb_ref[0, 0, j * chunk:(j + 1) * chunk, :]
            gc_all = _cumsum_rows(gbt)
            per_chunk[j] = (gbt, gc_all, jnp.transpose(gc_all))
        return per_chunk[j]

    def program(hs, j):
        rows = slice(j * chunk, (j + 1) * chunk)
        gbt, gc_all, gc_t = chunk_gates(j)
        sl = {h: slice(h * dk, (h + 1) * dk) for h in hs}
        q = {h: q_ref[0, rows, sl[h]] for h in hs}
        k = {h: k_ref[0, rows, sl[h]] for h in hs}
        v = {h: v_ref[0, rows, sl[h]] for h in hs}
        beta = {h: gbt[:, h:h + 1] for h in hs}
        gcol = {h: gc_all[:, heads + h:heads + h + 1] for h in hs}
        glast = {h: gc_all[chunk - 1:chunk, heads + h:heads + h + 1] for h in hs}
        decay = {h: jnp.where(causal, jnp.exp(gcol[h] - gc_t[heads + h:heads + h + 1, :]), 0.0) for h in hs}
        kb = {h: k[h] * beta[h] for h in hs}
        eg = {h: jnp.exp(gcol[h]) for h in hs}
        yield
        m = {h: jnp.where(strict, -_dot_nt(kb[h], k[h]) * decay[h], 0.0) for h in hs}
        qk = {h: jnp.where(causal, _dot_nt(q[h], k[h]) * decay[h], 0.0) for h in hs}
        yield
        t = {h: eye + m[h] for h in hs}
        m = {h: _dot(m[h], m[h]) for h in hs}
        for _ in range(chunk.bit_length() - 3):
            yield
            res = {h: _dot(jnp.concatenate([m[h], t[h]], axis=0), m[h]) for h in hs}
            m = {h: res[h][:chunk] for h in hs}
            t = {h: t[h] + res[h][chunk:] for h in hs}
        yield
        t = {h: t[h] + _dot(t[h], m[h]) for h in hs}
        yield
        sol = {h: _dot(t[h], jnp.concatenate([v[h] * beta[h], kb[h] * eg[h]], axis=1)) for h in hs}
        yield
        assert all(applied[h] == j for h in hs)
        s = {h: state[h] for h in hs}
        v_new = {h: sol[h][:, :dk] - _dot(sol[h][:, dk:], s[h]) for h in hs}
        os = {h: _dot(q[h] * eg[h], s[h]) for h in hs}
        yield
        o = {h: os[h] + _dot(qk[h], v_new[h]) for h in hs}
        for h in hs:
            state[h] = s[h] * jnp.exp(glast[h]) + _dot_tn(k[h] * jnp.exp(glast[h] - gcol[h]), v_new[h])
            applied[h] = j + 1
        yield
        for h in hs:
            z = z_ref[0, rows, sl[h]]
            y_ref[0, rows, sl[h]] = (_rms(o[h], nw) * (z * _sigmoid(z))).astype(y_ref.dtype)

    gsz = _pick(hb, (4, 2, 1))
    _run_staggered([(j * REC_CHUNK_TICKS, program(range(g0, g0 + gsz), j))
                    for j in range(nch) for g0 in range(0, hb, gsz)])
    for h in range(hb):
        s_scr[h] = state[h]

    @pl.when(c == nc - 1)
    def _():
        sout_ref[0] = s_scr[...]


def _dn_recurrence(qkv, gb, p, norm_w, s0, *, heads, hb, dk, chunk, c_z):
    b, t, _ = qkv.shape
    groups = heads // hb
    wb = hb * dk
    nqk = heads * dk // wb
    nch = _pick(t // chunk, (REC_CHUNKS, 2, 1))
    rows = nch * chunk
    kern = functools.partial(_dn_kernel, heads=heads, hb=hb, dk=dk, chunk=chunk, nch=nch)
    return pl.pallas_call(
        kern,
        grid=(b, groups, t // rows),
        in_specs=[
            pl.BlockSpec((1, rows, wb), lambda bb, g, c: (bb, c, g)),
            pl.BlockSpec((1, rows, wb), lambda bb, g, c: (bb, c, nqk + g)),
            pl.BlockSpec((1, rows, wb), lambda bb, g, c: (bb, c, 2 * nqk + g)),
            pl.BlockSpec((1, 1, rows, LANE), lambda bb, g, c: (bb, g, c, 0)),
            pl.BlockSpec((1, rows, wb), lambda bb, g, c: (bb, c, c_z // wb + g)),
            pl.BlockSpec((1, dk), lambda bb, g, c: (0, 0)),
            pl.BlockSpec((1, hb, dk, dk), lambda bb, g, c: (bb, g, 0, 0)),
        ],
        out_specs=[
            pl.BlockSpec((1, rows, wb), lambda bb, g, c: (bb, c, g)),
            pl.BlockSpec((1, hb, dk, dk), lambda bb, g, c: (bb, g, 0, 0)),
        ],
        out_shape=[jax.ShapeDtypeStruct((b, t, heads * dk), BF16),
                   jax.ShapeDtypeStruct((b, heads, dk, dk), F32)],
        scratch_shapes=[pltpu.VMEM((hb, dk, dk), F32)],
        compiler_params=_cparams(("parallel", "parallel", "arbitrary")),
        name="dn_recurrence",
    )(qkv, qkv, qkv, gb, p, norm_w.reshape(1, dk), s0)


def _pair_sums(x, m0):
    s0 = jnp.sum(jnp.where(m0, x, 0.0), axis=-1, keepdims=True)
    s1 = jnp.sum(jnp.where(m0, 0.0, x), axis=-1, keepdims=True)
    return jnp.where(m0, s0, s1)


def _rw_pre_kernel(*refs, nblk, off, rw, xw_sl, xa_sl, xg_sl):
    main, halos = refs[:nblk], refs[nblk:2 * nblk]
    (cache_ref, mu_ref, w0_ref, w2_ref, a0_ref, a2_ref, g2_ref, kk_ref, ka_ref,
     r_ref, k_ref, v_ref, lw_ref, av_ref, bv_ref, gate_ref) = refs[2 * nblk:]
    u = jnp.concatenate([ref[0] for ref in main], axis=1)
    halo = jnp.concatenate([ref[0] for ref in halos], axis=1)
    halo = jnp.where(pl.program_id(1) == 0, cache_ref[0], halo)
    x = u + mu_ref[...] * (_shift_rows(u, halo, 1) - u)
    x = pltpu.roll(x, x.shape[1] - off, axis=1)
    r = x[:, :rw]
    kr = x[:, rw:2 * rw]
    vr = x[:, 2 * rw:3 * rw]
    w_log = -_softplus(-(w0_ref[...] + _dot(jnp.tanh(x[:, xw_sl[0]:xw_sl[1]]), w2_ref[...]))) - 0.5
    a = _sigmoid(a0_ref[...] + _dot(x[:, xa_sl[0]:xa_sl[1]], a2_ref[...]))
    gate_ref[0] = _dot(_sigmoid(x[:, xg_sl[0]:xg_sl[1]]), g2_ref[...])
    r_ref[0] = r
    v_ref[0] = vr
    lw_ref[0] = -jnp.exp(w_log)
    k_ref[0] = kr * (1.0 + (a - 1.0) * ka_ref[...])
    kkr = kr * kk_ref[...]
    m0 = _iota2((1, LANE), 1) < LANE // 2
    for jb in range(rw // LANE):
        sl = slice(jb * LANE, (jb + 1) * LANE)
        blk = kkr[:, sl]
        kk = blk * lax.rsqrt(_pair_sums(blk * blk, m0) + L2_EPS)
        av_ref[0, :, sl] = -kk
        bv_ref[0, :, sl] = kk * a[:, sl]


def _rw_pre(p, cache8, pp):
    b, t, _ = p.shape
    rw, win0, width, bw = pp["rw"], pp["win0"], pp["win_w"], pp["win_bw"]
    nblk = width // bw
    tm = _pick(t, (128, 64, 32, 16))
    kern = functools.partial(_rw_pre_kernel, nblk=nblk, off=pp["win_off"], rw=rw,
                             xw_sl=pp["xw_sl"], xa_sl=pp["xa_sl"], xg_sl=pp["xg_sl"])
    full = lambda shape: pl.BlockSpec(shape, lambda bb, i: (0,) * len(shape))
    row = pl.BlockSpec((1, tm, rw), lambda bb, i: (bb, i, 0))
    cb = [win0 // bw + n for n in range(nblk)]
    main = [pl.BlockSpec((1, tm, bw), lambda bb, i, c=c: (bb, i, c)) for c in cb]
    halos = [pl.BlockSpec((1, SUBLANE, bw),
                          lambda bb, i, c=c: (bb, jnp.maximum(i * (tm // SUBLANE) - 1, 0), c)) for c in cb]
    small = [pp["mu_win"], pp["w0"], pp["w2p"], pp["a0"], pp["a2p"], pp["g2p"], pp["k_k"], pp["k_a"]]
    return pl.pallas_call(
        kern,
        grid=(b, t // tm),
        in_specs=main + halos + [pl.BlockSpec((1, SUBLANE, width), lambda bb, i: (bb, 0, 0))]
        + [full(a.shape) for a in small],
        out_specs=[row] * 7,
        out_shape=[jax.ShapeDtypeStruct((b, t, rw), F32)] * 7,
        compiler_params=_cparams(("parallel", "arbitrary")),
        name="rw_pre",
    )(*([p] * (2 * nblk)), cache8, *small)


def _rw_kernel(r_ref, k_ref, v_ref, lw_ref, av_ref, bv_ref, gate_ref, rk_ref, lnw_ref, lnb_ref, s0_ref,
               y_ref, sout_ref, s_scr, *, pb, chunk, nch):
    c = pl.program_id(2)
    nc = pl.num_programs(2)
    hn = LANE // 2

    @pl.when(c == 0)
    def _():
        s_scr[...] = s0_ref[0]

    c2 = 2 * chunk
    m0 = _iota2((1, LANE), 1) < hn
    m1 = jnp.logical_not(m0)
    blockmask = (_iota2((LANE, LANE), 0) < hn) == (_iota2((LANE, LANE), 1) < hn)
    row2 = _iota2((chunk, c2), 0)
    col2 = _iota2((chunk, c2), 1) & (chunk - 1)
    strict2 = row2 > col2
    eye2 = (row2 == col2).astype(F32)
    incl4 = _iota2((chunk, 2 * c2), 0) >= (_iota2((chunk, 2 * c2), 1) & (chunk - 1))
    bd = (_iota2((c2, c2), 0) < chunk) == (_iota2((c2, c2), 1) < chunk)

    def by_head(x):
        return jnp.concatenate([jnp.where(m0, x, 0.0), jnp.where(m1, x, 0.0)], axis=0)

    def blockdiag(p2):
        return jnp.where(bd, jnp.concatenate([p2, p2], axis=0), 0.0)

    state = {p: s_scr[p] for p in range(pb)}
    applied = {p: 0 for p in range(pb)}

    def program(ps, j):
        rows = slice(j * chunk, (j + 1) * chunk)
        lanes = slice(ps[0] * LANE, (ps[-1] + 1) * LANE)
        sl = {p: slice(p * LANE, (p + 1) * LANE) for p in ps}
        loc = {p: slice((p - ps[0]) * LANE, (p - ps[0] + 1) * LANE) for p in ps}
        r = {p: r_ref[0, rows, sl[p]] for p in ps}
        k = {p: k_ref[0, rows, sl[p]] for p in ps}
        v = {p: v_ref[0, rows, sl[p]] for p in ps}
        lw = {p: lw_ref[0, rows, sl[p]] for p in ps}
        bv = {p: bv_ref[0, rows, sl[p]] for p in ps}
        cw_all = _cumsum_rows(lw_ref[0, rows, lanes])
        yield
        cw = {p: cw_all[:, loc[p]] for p in ps}
        tot = {p: cw[p][chunk - 1:chunk, :] for p in ps}
        e_neg = {p: jnp.exp(-cw[p]) for p in ps}
        e_end = {p: jnp.exp(tot[p] - cw[p]) for p in ps}
        lhs = {p: jnp.concatenate([av_ref[0, rows, sl[p]] * jnp.exp(cw[p] - lw[p]), r[p] * jnp.exp(cw[p])],
                                  axis=0) for p in ps}
        rhs_g = {p: jnp.concatenate([by_head(bv[p] * e_neg[p]), by_head(k[p] * e_neg[p])], axis=0) for p in ps}
        rhs_s = {p: jnp.concatenate([bv[p] * e_end[p], k[p] * e_end[p]], axis=0) for p in ps}
        vh = {p: by_head(v[p]) for p in ps}
        g = {p: _dot_nt(lhs[p], rhs_g[p]) for p in ps}
        yield
        avs = {p: _dot(jnp.where(strict2, g[p][:chunk, c2:], 0.0), vh[p]) for p in ps}
        m = {p: jnp.where(strict2, g[p][:chunk, :c2], 0.0) for p in ps}
        t2 = {p: eye2 + m[p] for p in ps}
        m = {p: _dot(m[p], blockdiag(m[p])) for p in ps}
        for _ in range(chunk.bit_length() - 3):
            yield
            res = {p: _dot(jnp.concatenate([m[p], t2[p]], axis=0), blockdiag(m[p])) for p in ps}
            m = {p: res[p][:chunk] for p in ps}
            t2 = {p: t2[p] + res[p][chunk:] for p in ps}
        yield
        t2 = {p: t2[p] + _dot(t2[p], blockdiag(m[p])) for p in ps}
        assert all(applied[p] == j for p in ps)
        s = {p: state[p] for p in ps}
        sr = {p: _dot_nt(lhs[p], s[p]) for p in ps}
        yield
        u = {p: _dot(t2[p], by_head(sr[p][:chunk] + avs[p])) for p in ps}
        yield
        yr = {p: _dot(jnp.where(incl4, g[p][chunk:], 0.0), jnp.concatenate([by_head(u[p]), vh[p]], axis=0))
              for p in ps}
        for p in ps:
            state[p] = jnp.where(blockmask, s[p] * jnp.exp(tot[p])
                                 + _dot_tn(jnp.concatenate([u[p], v[p]], axis=0), rhs_s[p]), 0.0)
            applied[p] = j + 1
        yield
        for p in ps:
            y = sr[p][chunk:] + yr[p]
            mean = _pair_sums(y, m0) * (1.0 / hn)
            d = y - mean
            var = _pair_sums(d * d, m0) * (1.0 / hn)
            yn = d * lax.rsqrt(var + RW_GN_EPS) * lnw_ref[:, sl[p]] + lnb_ref[:, sl[p]]
            bonus = _pair_sums(r[p] * k[p] * rk_ref[:, sl[p]], m0) * v[p]
            y_ref[0, rows, sl[p]] = ((yn + bonus) * gate_ref[0, rows, sl[p]]).astype(y_ref.dtype)

    gsz = _pick(pb, (4, 2, 1))
    _run_staggered([(j * REC_CHUNK_TICKS, program(range(g0, g0 + gsz), j))
                    for j in range(nch) for g0 in range(0, pb, gsz)])
    for p in range(pb):
        s_scr[p] = state[p]

    @pl.when(c == nc - 1)
    def _():
        sout_ref[0] = s_scr[...]


def _rw_recurrence(r, k, v, lw, av, bv, gate, r_k, ln_w, ln_b, s0p, *, pb, chunk):
    b, t, rw = r.shape
    pairs = rw // LANE
    groups = pairs // pb
    wb = pb * LANE
    nch = _pick(t // chunk, (REC_CHUNKS, 2, 1))
    kern = functools.partial(_rw_kernel, pb=pb, chunk=chunk, nch=nch)
    tile = pl.BlockSpec((1, nch * chunk, wb), lambda bb, g, c: (bb, c, g))
    par = pl.BlockSpec((1, wb), lambda bb, g, c: (0, g))
    st = pl.BlockSpec((1, pb, LANE, LANE), lambda bb, g, c: (bb, g, 0, 0))
    return pl.pallas_call(
        kern,
        grid=(b, groups, t // (nch * chunk)),
        in_specs=[tile] * 7 + [par] * 3 + [st],
        out_specs=[tile, st],
        out_shape=[jax.ShapeDtypeStruct((b, t, rw), BF16),
                   jax.ShapeDtypeStruct((b, pairs, LANE, LANE), F32)],
        scratch_shapes=[pltpu.VMEM((pb, LANE, LANE), F32)],
        compiler_params=_cparams(("parallel", "parallel", "arbitrary")),
        name="rw_recurrence",
    )(r, k, v, lw, av, bv, gate, r_k, ln_w, ln_b, s0p)


def _prepare(w):
    heads = w["dn_a_log"].shape[-1]
    dk = w["dn_norm_w"].shape[-1]
    qkv_w = w["dn_conv_w"].shape[-1]
    v_w = heads * dk
    assert qkv_w == 3 * v_w and dk == LANE
    rw_heads, rw_head = w["rw_r_k"].shape
    assert rw_head == LANE // 2
    rw = rw_heads * rw_head
    lw_n, la_n, lg_n = w["rw_w2"].shape[0], w["rw_a2"].shape[0], w["rw_g2"].shape[0]
    o1 = qkv_w
    o2 = o1 + v_w
    o4 = o2 + 2 * heads
    shift_w = 3 * rw + lw_n + la_n + lg_n
    assert o2 % LANE == 0 and 2 * heads <= LANE and w["w_in"].shape[1] == o4 + shift_w
    win0 = o4 // LANE * LANE
    win_off = o4 - win0
    win_w = _round_up(win_off + shift_w, LANE)
    win_bw = math.gcd(math.gcd(win0, win_w), 8 * LANE)
    assert win0 + win_w <= _round_up(o4 + shift_w, IN_TN)

    def lora_block(start, n, weight):
        lo, hi = start // LANE * LANE, _round_up(start + n, LANE)
        padded = jnp.pad(weight, ((start - lo, hi - start - n), (0, 0))).astype(BF16)
        return (lo, hi), padded

    xw_sl, w2p = lora_block(3 * rw, lw_n, w["rw_w2"])
    xa_sl, a2p = lora_block(3 * rw + lw_n, la_n, w["rw_a2"])
    xg_sl, g2p = lora_block(3 * rw + lw_n + la_n, lg_n, w["rw_g2"])
    lane_pad = lambda a: jnp.pad(a, (heads, LANE - 2 * heads)).reshape(1, LANE)
    in_window = lambda a: jnp.pad(a, [(0, 0)] * (a.ndim - 1) + [(win_off, win_w - win_off - shift_w)])
    return dict(
        heads=heads, dk=dk, rw=rw, rw_heads=rw_heads, o1=o1, o2=o2, o4=o4, shift_w=shift_w,
        win0=win0, win_off=win_off, win_w=win_w, win_bw=win_bw, in_window=in_window,
        xw_sl=xw_sl, xa_sl=xa_sl, xg_sl=xg_sl, w2p=w2p, a2p=a2p, g2p=g2p,
        alog_row=lane_pad(w["dn_a_log"]), dtb_row=lane_pad(w["dn_dt_bias"]),
        mu_win=in_window(w["rw_mu"].reshape(1, shift_w)),
        w0=w["rw_w0"].reshape(1, rw), a0=w["rw_a0"].reshape(1, rw),
        k_k=w["rw_k_k"].reshape(1, rw), k_a=w["rw_k_a"].reshape(1, rw),
        w_down=w["w_down"].astype(BF16),
    )


def _layer(x, dn_state, dn_conv, rw_state, rw_shift, w, pp):
    b, t, d = x.shape
    m = b * t
    heads, dk, rw, rw_heads = pp["heads"], pp["dk"], pp["rw"], pp["rw_heads"]
    o1, o2, o4, shift_w = pp["o1"], pp["o2"], pp["o4"], pp["shift_w"]
    chunk = 64 if t % 64 == 0 else t
    assert chunk & (chunk - 1) == 0 and chunk >= 2 * SUBLANE
    hb = 16 if heads % 16 == 0 else heads
    pairs = rw // LANE
    pb = 16 if pairs % 16 == 0 else pairs
    tm = _pick(m, (1024, 512, 256, 128))

    xf = x.reshape(m, d)
    h = _rms_bf16(xf, w["g_mix_pre"])
    p = _matmul_w32(h, w["w_in"], tm=tm, tn=IN_TN, name="in_proj")
    p = p.reshape(b, t, p.shape[1])

    cache8 = jnp.pad(dn_conv.astype(F32), ((0, 0), (SUBLANE - dn_conv.shape[1], 0), (0, 0)))
    qkv, gb = _dn_pre(p, cache8, w["dn_conv_w"], pp["alog_row"], pp["dtb_row"],
                      heads=heads, dk=dk, hb=hb, c_ba=o2)
    y_a, new_dn_state = _dn_recurrence(qkv, gb, p, w["dn_norm_w"], dn_state.astype(F32),
                                       heads=heads, hb=hb, dk=dk, chunk=chunk, c_z=o1)
    new_dn_conv = p[:, t - dn_conv.shape[1]:, :o1]

    shift8 = jnp.pad(pp["in_window"](rw_shift.astype(F32)), ((0, 0), (SUBLANE - 1, 0), (0, 0)))
    r, k, v, lw, av, bv, gate = _rw_pre(p, shift8, pp)
    hn = LANE // 2
    s4 = rw_state.astype(F32).reshape(b, pairs, 2, hn, hn)
    zeros = jnp.zeros_like(s4[:, :, 0])
    s0p = jnp.concatenate([jnp.concatenate([s4[:, :, 0], zeros], axis=-1),
                           jnp.concatenate([zeros, s4[:, :, 1]], axis=-1)], axis=-2)
    y_b, sp = _rw_recurrence(r, k, v, lw, av, bv, gate, w["rw_r_k"].reshape(1, rw),
                             w["rw_ln_w"].reshape(1, rw), w["rw_ln_b"].reshape(1, rw), s0p,
                             pb=pb, chunk=chunk)
    new_rw_state = jnp.stack([sp[:, :, :hn, :hn], sp[:, :, hn:, hn:]], axis=2).reshape(b, rw_heads, hn, hn)
    new_rw_shift = p[:, t - 1:, o4:o4 + shift_w]

    mixo = _matmul2(y_a.reshape(m, heads * dk), y_b.reshape(m, rw), w["w_out"],
                    tm=tm, tn=_pick(d, (512, 256, 128)), name="out_proj")
    x1, h2 = _resid_rms(xf, mixo, w["g_mix_post"], w["g_ffn_pre"])
    dff = w["w_gate"].shape[1]
    f = _swiglu_up(h2, w["w_gate"], w["w_up"], tm=tm, tn=_pick(dff, (256, 128)))
    fo = _matmul(f, pp["w_down"], tm=_pick(m, (512, 256, 128)), tn=_pick(d, (256, 128)),
                 out_dtype=BF16, name="ffn_down")
    out = _resid_final(x1, fo, w["g_ffn_post"]).reshape(b, t, d)
    return out, (new_dn_state, new_dn_conv, new_rw_state, new_rw_shift)


_WEIGHT_NAMES = ("g_mix_pre", "g_mix_post", "w_in", "dn_conv_w", "dn_a_log", "dn_dt_bias", "dn_norm_w",
                 "rw_mu", "rw_w0", "rw_w2", "rw_a0", "rw_a2", "rw_g2", "rw_k_k", "rw_k_a", "rw_r_k",
                 "rw_ln_w", "rw_ln_b", "w_out", "g_ffn_pre", "g_ffn_post", "w_gate", "w_up", "w_down")


def kernel(x_prompt, x_sample, state_dn, cache_dn_conv, state_rwkv, cache_rwkv_shift,
           g_mix_pre, g_mix_post, w_in, dn_conv_w, dn_a_log, dn_dt_bias, dn_norm_w,
           rw_mu, rw_w0, rw_w2, rw_a0, rw_a2, rw_g2, rw_k_k, rw_k_a, rw_r_k, rw_ln_w, rw_ln_b,
           w_out, g_ffn_pre, g_ffn_post, w_gate, w_up, w_down):
    stacked = (g_mix_pre, g_mix_post, w_in, dn_conv_w, dn_a_log, dn_dt_bias, dn_norm_w,
               rw_mu, rw_w0, rw_w2, rw_a0, rw_a2, rw_g2, rw_k_k, rw_k_a, rw_r_k, rw_ln_w, rw_ln_b,
               w_out, g_ffn_pre, g_ffn_post, w_gate, w_up, w_down)
    depth = w_in.shape[0]
    bp = x_prompt.shape[0]
    dt = x_prompt.dtype
    yp, ys = x_prompt, x_sample
    outs_p, outs_s = [], []
    for l in range(depth):
        w = {n: a[l] for n, a in zip(_WEIGHT_NAMES, stacked)}
        pp = _prepare(w)
        heads, dk, rw_heads = pp["heads"], pp["dk"], pp["rw_heads"]
        hn = LANE // 2
        yp, st_p = _layer(yp,
                          jnp.zeros((bp, heads, dk, dk), dt),
                          jnp.zeros((bp, cache_dn_conv.shape[2], cache_dn_conv.shape[3]), dt),
                          jnp.zeros((bp, rw_heads, hn, hn), dt),
                          jnp.zeros((bp, 1, cache_rwkv_shift.shape[3]), dt), w, pp)
        ys, st_s = _layer(ys, state_dn[l], cache_dn_conv[l], state_rwkv[l], cache_rwkv_shift[l], w, pp)
        outs_p.append(st_p)
        outs_s.append(st_s)
    stack = lambda outs, i: jnp.stack([o[i] for o in outs])
    return (yp, ys,
            stack(outs_p, 0), stack(outs_p, 1), stack(outs_p, 2), stack(outs_p, 3),
            stack(outs_s, 0), stack(outs_s, 1), stack(outs_s, 2), stack(outs_s, 3))
```

```python
import functools
import math

import jax
import jax.numpy as jnp
from jax import lax
from jax.experimental import pallas as pl
from jax.experimental.pallas import tpu as pltpu

F32 = jnp.float32
BF16 = jnp.bfloat16

LANE = 128
SUBLANE = 8
VMEM_LIMIT_BYTES = 56 * 2**20
NORM_EPS = 1e-6
L2_EPS = 1e-6
RW_GN_EPS = 64e-5
IN_TN = 512
REC_CHUNKS = 4
REC_CHUNK_TICKS = 3


def _cparams(sem):
    return pltpu.CompilerParams(dimension_semantics=sem, vmem_limit_bytes=VMEM_LIMIT_BYTES)


def _round_up(x, m):
    return -(-x // m) * m


def _pick(n, cands):
    for c in cands:
        if n % c == 0:
            return c
    return n


def _dot(a, b):
    return jnp.dot(a.astype(BF16), b.astype(BF16), preferred_element_type=F32)


def _dot_nt(a, b):
    return lax.dot_general(a.astype(BF16), b.astype(BF16), (((1,), (1,)), ((), ())),
                           preferred_element_type=F32)


def _dot_tn(a, b):
    return lax.dot_general(a.astype(BF16), b.astype(BF16), (((0,), (0,)), ((), ())),
                           preferred_element_type=F32)


def _cumsum_rows(x):
    c = x.shape[0]
    tri = (_iota2((c, c), 0) >= _iota2((c, c), 1)).astype(BF16)
    hi = x.astype(BF16)
    r1 = x - hi.astype(F32)
    mid = r1.astype(BF16)
    lo = (r1 - mid.astype(F32)).astype(BF16)
    dot = lambda p: jnp.dot(tri, p, preferred_element_type=F32)
    return dot(hi) + (dot(mid) + dot(lo))


def _sigmoid(x):
    return 0.5 * jnp.tanh(0.5 * x) + 0.5


def _softplus(x):
    return jnp.maximum(x, 0.0) + jnp.log(1.0 + jnp.exp(-jnp.abs(x)))


def _iota2(shape, dim):
    return lax.broadcasted_iota(jnp.int32, shape, dim)


def _run_staggered(programs):
    live = list(programs)
    tick = 0
    while live:
        still = []
        for start, prog in live:
            if tick >= start:
                try:
                    next(prog)
                except StopIteration:
                    continue
            still.append((start, prog))
        live = still
        tick += 1


def _rms(x, g):
    return x * lax.rsqrt(jnp.mean(x * x, axis=-1, keepdims=True) + NORM_EPS) * g


def _rms_kernel(x_ref, g_ref, o_ref):
    o_ref[...] = _rms(x_ref[...], g_ref[...]).astype(o_ref.dtype)


def _rms_bf16(x, g):
    m, d = x.shape
    tm = _pick(m, (256, 128))
    return pl.pallas_call(
        _rms_kernel,
        grid=(m // tm,),
        in_specs=[pl.BlockSpec((tm, d), lambda i: (i, 0)), pl.BlockSpec((1, d), lambda i: (0, 0))],
        out_specs=pl.BlockSpec((tm, d), lambda i: (i, 0)),
        out_shape=jax.ShapeDtypeStruct((m, d), BF16),
        compiler_params=_cparams(("parallel",)),
        name="rms_pre",
    )(x, g.reshape(1, d))


def _resid_rms_kernel(x_ref, y_ref, gp_ref, gn_ref, x1_ref, h_ref):
    x1 = x_ref[...] + _rms(y_ref[...].astype(F32), gp_ref[...])
    x1_ref[...] = x1
    h_ref[...] = _rms(x1, gn_ref[...]).astype(h_ref.dtype)


def _resid_rms(x, y, g_post, g_next):
    m, d = x.shape
    tm = _pick(m, (256, 128))
    row = pl.BlockSpec((tm, d), lambda i: (i, 0))
    par = pl.BlockSpec((1, d), lambda i: (0, 0))
    return pl.pallas_call(
        _resid_rms_kernel,
        grid=(m // tm,),
        in_specs=[row, row, par, par],
        out_specs=[row, row],
        out_shape=[jax.ShapeDtypeStruct((m, d), F32), jax.ShapeDtypeStruct((m, d), BF16)],
        compiler_params=_cparams(("parallel",)),
        name="resid_rms",
    )(x, y, g_post.reshape(1, d), g_next.reshape(1, d))


def _resid_final_kernel(x_ref, y_ref, g_ref, o_ref):
    o_ref[...] = x_ref[...] + _rms(y_ref[...].astype(F32), g_ref[...])


def _resid_final(x, y, g):
    m, d = x.shape
    tm = _pick(m, (256, 128))
    row = pl.BlockSpec((tm, d), lambda i: (i, 0))
    return pl.pallas_call(
        _resid_final_kernel,
        grid=(m // tm,),
        in_specs=[row, row, pl.BlockSpec((1, d), lambda i: (0, 0))],
        out_specs=row,
        out_shape=jax.ShapeDtypeStruct((m, d), F32),
        compiler_params=_cparams(("parallel",)),
        name="resid_final",
    )(x, y, g.reshape(1, d))


def _mm_kernel(x_ref, w_ref, o_ref):
    o_ref[...] = jnp.dot(x_ref[...], w_ref[...], preferred_element_type=F32).astype(o_ref.dtype)


def _matmul(x, w, *, tm, tn, out_dtype, name):
    m, k = x.shape
    n = w.shape[1]
    return pl.pallas_call(
        _mm_kernel,
        grid=(m // tm, n // tn),
        in_specs=[pl.BlockSpec((tm, k), lambda i, j: (i, 0)), pl.BlockSpec((k, tn), lambda i, j: (0, j))],
        out_specs=pl.BlockSpec((tm, tn), lambda i, j: (i, j)),
        out_shape=jax.ShapeDtypeStruct((m, n), out_dtype),
        compiler_params=_cparams(("parallel", "arbitrary")),
        name=name,
    )(x, w)


def _mm_w32t_kernel(x_ref, wt_ref, o_ref, *, n_valid):
    wt = wt_ref[...]
    tn = wt.shape[0]
    if n_valid % tn:
        wt = jnp.where(pl.program_id(1) * tn + _iota2(wt.shape, 0) < n_valid, wt, 0.0)
    o_ref[...] = lax.dot_general(x_ref[...], wt.astype(BF16), (((1,), (1,)), ((), ())),
                                 preferred_element_type=F32)


def _matmul_w32t(x, wt, *, tm, tn, name):
    m, k = x.shape
    n = wt.shape[0]
    nt = pl.cdiv(n, tn)
    return pl.pallas_call(
        functools.partial(_mm_w32t_kernel, n_valid=n),
        grid=(m // tm, nt),
        in_specs=[pl.BlockSpec((tm, k), lambda i, j: (i, 0)), pl.BlockSpec((tn, k), lambda i, j: (j, 0))],
        out_specs=pl.BlockSpec((tm, tn), lambda i, j: (i, j)),
        out_shape=jax.ShapeDtypeStruct((m, nt * tn), F32),
        compiler_params=_cparams(("parallel", "arbitrary")),
        name=name,
    )(x, wt)


def _mm2_kernel(xa_ref, xb_ref, wa_ref, wb_ref, o_ref):
    acc = jnp.dot(xa_ref[...], wa_ref[...].astype(BF16), preferred_element_type=F32)
    acc = acc + jnp.dot(xb_ref[...], wb_ref[...].astype(BF16), preferred_element_type=F32)
    o_ref[...] = acc.astype(o_ref.dtype)


def _matmul2(xa, xb, w, *, tm, tn, name):
    m, ka = xa.shape
    kb = xb.shape[1]
    n = w.shape[1]
    assert ka == kb and w.shape[0] == ka + kb
    return pl.pallas_call(
        _mm2_kernel,
        grid=(m // tm, n // tn),
        in_specs=[pl.BlockSpec((tm, ka), lambda i, j: (i, 0)), pl.BlockSpec((tm, kb), lambda i, j: (i, 0)),
                  pl.BlockSpec((ka, tn), lambda i, j: (0, j)), pl.BlockSpec((kb, tn), lambda i, j: (1, j))],
        out_specs=pl.BlockSpec((tm, tn), lambda i, j: (i, j)),
        out_shape=jax.ShapeDtypeStruct((m, n), BF16),
        compiler_params=_cparams(("parallel", "arbitrary")),
        name=name,
    )(xa, xb, w, w)


def _swiglu_kernel(x_ref, wg_ref, wu_ref, o_ref):
    x = x_ref[...]
    g = jnp.dot(x, wg_ref[...].astype(BF16), preferred_element_type=F32)
    u = jnp.dot(x, wu_ref[...].astype(BF16), preferred_element_type=F32)
    o_ref[...] = (g * _sigmoid(g) * u).astype(o_ref.dtype)


def _swiglu_up(x, wg, wu, *, tm, tn):
    m, k = x.shape
    n = wg.shape[1]
    wspec = pl.BlockSpec((k, tn), lambda i, j: (0, j))
    return pl.pallas_call(
        _swiglu_kernel,
        grid=(m // tm, n // tn),
        in_specs=[pl.BlockSpec((tm, k), lambda i, j: (i, 0)), wspec, wspec],
        out_specs=pl.BlockSpec((tm, tn), lambda i, j: (i, j)),
        out_shape=jax.ShapeDtypeStruct((m, n), BF16),
        compiler_params=_cparams(("parallel", "arbitrary")),
        name="ffn_up",
    )(x, wg, wu)


def _shift_rows(u, halo, s):
    ru = pltpu.roll(u, s, axis=0)
    rh = pltpu.roll(halo, s, axis=0)
    top = jnp.where(_iota2(halo.shape, 0) < s, rh, ru[:SUBLANE])
    if u.shape[0] == SUBLANE:
        return top
    return jnp.concatenate([top, ru[SUBLANE:]], axis=0)


def _dn_pre_kernel(p_ref, halo_ref, cache_ref, cw_ref, ba_ref, alog_ref, dtb_ref,
                   qkv_ref, gb_ref, *, heads, dk, hb):
    i = pl.program_id(1)
    j = pl.program_id(2)
    tm = p_ref.shape[1]
    rb = min(tm, 64)

    def strips(scale):
        for h in range(heads):
            cols = slice(h * dk, (h + 1) * dk)
            cw = cw_ref[:, cols]
            for r0 in range(0, tm, rb):
                u = p_ref[0, r0:r0 + rb, cols]
                if r0 == 0:
                    halo = jnp.where(i == 0, cache_ref[0, :, cols], halo_ref[0, :, cols])
                else:
                    halo = p_ref[0, r0 - SUBLANE:r0, cols]
                prev = [_shift_rows(u, halo, sh) for sh in (3, 2, 1)]
                conv = prev[0] * cw[0:1]
                conv = conv + prev[1] * cw[1:2]
                conv = conv + prev[2] * cw[2:3]
                conv = conv + u * cw[3:4]
                s = conv * _sigmoid(conv)
                if scale is not None:
                    s = s * (lax.rsqrt(jnp.sum(s * s, axis=-1, keepdims=True) + L2_EPS) * scale)
                qkv_ref[0, r0:r0 + rb, cols] = s

    @pl.when(j < 2)
    def _():
        strips(jnp.where(j == 0, dk ** -0.5, 1.0).astype(F32))

    @pl.when(j == 2)
    def _():
        strips(None)

    @pl.when(j == 0)
    def _():
        x = ba_ref[0]
        lane = _iota2(x.shape, 1)
        g = -jnp.exp(alog_ref[...]) * _softplus(x + dtb_ref[...])
        full = jnp.where(lane < heads, _sigmoid(x), g)
        for hg in range(heads // hb):
            gb_ref[0, hg] = full if hg == 0 else pltpu.roll(full, LANE - hg * hb, axis=1)


def _dn_pre(p, cache8, conv_w, alog_row, dtb_row, *, heads, dk, hb, c_ba):
    b, t, _ = p.shape
    w = heads * dk
    tm = _pick(t, (256, 128, 64, 32, 16))
    groups = heads // hb
    kern = functools.partial(_dn_pre_kernel, heads=heads, dk=dk, hb=hb)
    return pl.pallas_call(
        kern,
        grid=(b, t // tm, 3),
        in_specs=[
            pl.BlockSpec((1, tm, w), lambda bb, i, j: (bb, i, j)),
            pl.BlockSpec((1, SUBLANE, w), lambda bb, i, j: (bb, jnp.maximum(i * (tm // SUBLANE) - 1, 0), j)),
            pl.BlockSpec((1, SUBLANE, w), lambda bb, i, j: (bb, 0, j)),
            pl.BlockSpec((4, w), lambda bb, i, j: (0, j)),
            pl.BlockSpec((1, tm, LANE), lambda bb, i, j: (bb, i, c_ba // LANE)),
            pl.BlockSpec((1, LANE), lambda bb, i, j: (0, 0)),
            pl.BlockSpec((1, LANE), lambda bb, i, j: (0, 0)),
        ],
        out_specs=[
            pl.BlockSpec((1, tm, w), lambda bb, i, j: (bb, i, j)),
            pl.BlockSpec((1, groups, tm, LANE), lambda bb, i, j: (bb, 0, i, 0)),
        ],
        out_shape=[jax.ShapeDtypeStruct((b, t, 3 * w), F32),
                   jax.ShapeDtypeStruct((b, groups, t, LANE), F32)],
        compiler_params=_cparams(("parallel", "parallel", "arbitrary")),
        name="dn_pre",
    )(p, p, cache8, conv_w, p, alog_row, dtb_row)


def _dn_kernel(q_ref, k_ref, v_ref, gb_ref, z_ref, nw_ref, s0_ref, y_ref, sout_ref, s_scr,
               *, heads, hb, dk, chunk, nch):
    c = pl.program_id(2)
    nc = pl.num_programs(2)

    @pl.when(c == 0)
    def _():
        s_scr[...] = s0_ref[0]

    row = _iota2((chunk, chunk), 0)
    col = _iota2((chunk, chunk), 1)
    causal = row >= col
    strict = row > col
    nw = nw_ref[...]
    eye = (row == col).astype(F32)
    state = {h: s_scr[h] for h in range(hb)}
    applied = {h: 0 for h in range(hb)}
    per_chunk = {}

    def chunk_gates(j):
        if j not in per_chunk:
            gbt = gb_ref[0, 0, j * chunk:(j + 1) * chunk, :]
            gc_all = _cumsum_rows(gbt)
            per_chunk[j] = (gbt, gc_all, jnp.transpose(gc_all))
        return per_chunk[j]

    def program(hs, j):
        rows = slice(j * chunk, (j + 1) * chunk)
        gbt, gc_all, gc_t = chunk_gates(j)
        sl = {h: slice(h * dk, (h + 1) * dk) for h in hs}
        q = {h: q_ref[0, rows, sl[h]] for h in hs}
        k = {h: k_ref[0, rows, sl[h]] for h in hs}
        v = {h: v_ref[0, rows, sl[h]] for h in hs}
        beta = {h: gbt[:, h:h + 1] for h in hs}
        gcol = {h: gc_all[:, heads + h:heads + h + 1] for h in hs}
        glast = {h: gc_all[chunk - 1:chunk, heads + h:heads + h + 1] for h in hs}
        decay = {h: jnp.where(causal, jnp.exp(gcol[h] - gc_t[heads + h:heads + h + 1, :]), 0.0) for h in hs}
        kb = {h: k[h] * beta[h] for h in hs}
        eg = {h: jnp.exp(gcol[h]) for h in hs}
        yield
        m = {h: jnp.where(strict, -_dot_nt(kb[h], k[h]) * decay[h], 0.0) for h in hs}
        qk = {h: jnp.where(causal, _dot_nt(q[h], k[h]) * decay[h], 0.0) for h in hs}
        yield
        t = {h: eye + m[h] for h in hs}
        m = {h: _dot(m[h], m[h]) for h in hs}
        for _ in range(chunk.bit_length() - 3):
            yield
            res = {h: _dot(jnp.concatenate([m[h], t[h]], axis=0), m[h]) for h in hs}
            m = {h: res[h][:chunk] for h in hs}
            t = {h: t[h] + res[h][chunk:] for h in hs}
        yield
        t = {h: t[h] + _dot(t[h], m[h]) for h in hs}
        yield
        sol = {h: _dot(t[h], jnp.concatenate([v[h] * beta[h], kb[h] * eg[h]], axis=1)) for h in hs}
        yield
        assert all(applied[h] == j for h in hs)
        s = {h: state[h] for h in hs}
        v_new = {h: sol[h][:, :dk] - _dot(sol[h][:, dk:], s[h]) for h in hs}
        os = {h: _dot(q[h] * eg[h], s[h]) for h in hs}
        yield
        o = {h: os[h] + _dot(qk[h], v_new[h]) for h in hs}
        for h in hs:
            state[h] = s[h] * jnp.exp(glast[h]) + _dot_tn(k[h] * jnp.exp(glast[h] - gcol[h]), v_new[h])
            applied[h] = j + 1
        yield
        for h in hs:
            z = z_ref[0, rows, sl[h]]
            y_ref[0, rows, sl[h]] = (_rms(o[h], nw) * (z * _sigmoid(z))).astype(y_ref.dtype)

    gsz = _pick(hb, (4, 2, 1))
    _run_staggered([(j * REC_CHUNK_TICKS, program(range(g0, g0 + gsz), j))
                    for j in range(nch) for g0 in range(0, hb, gsz)])
    for h in range(hb):
        s_scr[h] = state[h]

    @pl.when(c == nc - 1)
    def _():
        sout_ref[0] = s_scr[...]


def _dn_recurrence(qkv, gb, p, norm_w, s0, *, heads, hb, dk, chunk, c_z):
    b, t, _ = qkv.shape
    groups = heads // hb
    wb = hb * dk
    nqk = heads * dk // wb
    nch = _pick(t // chunk, (REC_CHUNKS, 2, 1))
    rows = nch * chunk
    kern = functools.partial(_dn_kernel, heads=heads, hb=hb, dk=dk, chunk=chunk, nch=nch)
    return pl.pallas_call(
        kern,
        grid=(b, groups, t // rows),
        in_specs=[
            pl.BlockSpec((1, rows, wb), lambda bb, g, c: (bb, c, g)),
            pl.BlockSpec((1, rows, wb), lambda bb, g, c: (bb, c, nqk + g)),
            pl.BlockSpec((1, rows, wb), lambda bb, g, c: (bb, c, 2 * nqk + g)),
            pl.BlockSpec((1, 1, rows, LANE), lambda bb, g, c: (bb, g, c, 0)),
            pl.BlockSpec((1, rows, wb), lambda bb, g, c: (bb, c, c_z // wb + g)),
            pl.BlockSpec((1, dk), lambda bb, g, c: (0, 0)),
            pl.BlockSpec((1, hb, dk, dk), lambda bb, g, c: (bb, g, 0, 0)),
        ],
        out_specs=[
            pl.BlockSpec((1, rows, wb), lambda bb, g, c: (bb, c, g)),
            pl.BlockSpec((1, hb, dk, dk), lambda bb, g, c: (bb, g, 0, 0)),
        ],
        out_shape=[jax.ShapeDtypeStruct((b, t, heads * dk), BF16),
                   jax.ShapeDtypeStruct((b, heads, dk, dk), F32)],
        scratch_shapes=[pltpu.VMEM((hb, dk, dk), F32)],
        compiler_params=_cparams(("parallel", "parallel", "arbitrary")),
        name="dn_recurrence",
    )(qkv, qkv, qkv, gb, p, norm_w.reshape(1, dk), s0)


def _pair_sums(x, m0):
    s0 = jnp.sum(jnp.where(m0, x, 0.0), axis=-1, keepdims=True)
    s1 = jnp.sum(jnp.where(m0, 0.0, x), axis=-1, keepdims=True)
    return jnp.where(m0, s0, s1)


def _rw_pre_kernel(*refs, nblk, off, rw, xw_sl, xa_sl, xg_sl):
    main, halos = refs[:nblk], refs[nblk:2 * nblk]
    (cache_ref, mu_ref, w0_ref, w2_ref, a0_ref, a2_ref, g2_ref, kk_ref, ka_ref,
     r_ref, k_ref, v_ref, lw_ref, av_ref, bv_ref, gate_ref) = refs[2 * nblk:]
    u = jnp.concatenate([ref[0] for ref in main], axis=1)
    halo = jnp.concatenate([ref[0] for ref in halos], axis=1)
    halo = jnp.where(pl.program_id(1) == 0, cache_ref[0], halo)
    x = u + mu_ref[...] * (_shift_rows(u, halo, 1) - u)
    x = pltpu.roll(x, x.shape[1] - off, axis=1)
    r = x[:, :rw]
    kr = x[:, rw:2 * rw]
    vr = x[:, 2 * rw:3 * rw]
    w_log = -_softplus(-(w0_ref[...] + _dot(jnp.tanh(x[:, xw_sl[0]:xw_sl[1]]), w2_ref[...]))) - 0.5
    a = _sigmoid(a0_ref[...] + _dot(x[:, xa_sl[0]:xa_sl[1]], a2_ref[...]))
    gate_ref[0] = _dot(_sigmoid(x[:, xg_sl[0]:xg_sl[1]]), g2_ref[...])
    r_ref[0] = r
    v_ref[0] = vr
    lw_ref[0] = -jnp.exp(w_log)
    k_ref[0] = kr * (1.0 + (a - 1.0) * ka_ref[...])
    kkr = kr * kk_ref[...]
    m0 = _iota2((1, LANE), 1) < LANE // 2
    for jb in range(rw // LANE):
        sl = slice(jb * LANE, (jb + 1) * LANE)
        blk = kkr[:, sl]
        kk = blk * lax.rsqrt(_pair_sums(blk * blk, m0) + L2_EPS)
        av_ref[0, :, sl] = -kk
        bv_ref[0, :, sl] = kk * a[:, sl]


def _rw_pre(p, cache8, pp):
    b, t, _ = p.shape
    rw, win0, width, bw = pp["rw"], pp["win0"], pp["win_w"], pp["win_bw"]
    nblk = width // bw
    tm = _pick(t, (128, 64, 32, 16))
    kern = functools.partial(_rw_pre_kernel, nblk=nblk, off=pp["win_off"], rw=rw,
                             xw_sl=pp["xw_sl"], xa_sl=pp["xa_sl"], xg_sl=pp["xg_sl"])
    full = lambda shape: pl.BlockSpec(shape, lambda bb, i: (0,) * len(shape))
    row = pl.BlockSpec((1, tm, rw), lambda bb, i: (bb, i, 0))
    cb = [win0 // bw + n for n in range(nblk)]
    main = [pl.BlockSpec((1, tm, bw), lambda bb, i, c=c: (bb, i, c)) for c in cb]
    halos = [pl.BlockSpec((1, SUBLANE, bw),
                          lambda bb, i, c=c: (bb, jnp.maximum(i * (tm // SUBLANE) - 1, 0), c)) for c in cb]
    small = [pp["mu_win"], pp["w0"], pp["w2p"], pp["a0"], pp["a2p"], pp["g2p"], pp["k_k"], pp["k_a"]]
    return pl.pallas_call(
        kern,
        grid=(b, t // tm),
        in_specs=main + halos + [pl.BlockSpec((1, SUBLANE, width), lambda bb, i: (bb, 0, 0))]
        + [full(a.shape) for a in small],
        out_specs=[row] * 7,
        out_shape=[jax.ShapeDtypeStruct((b, t, rw), F32)] * 7,
        compiler_params=_cparams(("parallel", "arbitrary")),
        name="rw_pre",
    )(*([p] * (2 * nblk)), cache8, *small)


def _rw_kernel(r_ref, k_ref, v_ref, lw_ref, av_ref, bv_ref, gate_ref, rk_ref, lnw_ref, lnb_ref, s0_ref,
               y_ref, sout_ref, s_scr, *, pb, chunk, nch):
    c = pl.program_id(2)
    nc = pl.num_programs(2)
    hn = LANE // 2

    @pl.when(c == 0)
    def _():
        s_scr[...] = s0_ref[0]

    c2 = 2 * chunk
    m0 = _iota2((1, LANE), 1) < hn
    m1 = jnp.logical_not(m0)
    blockmask = (_iota2((LANE, LANE), 0) < hn) == (_iota2((LANE, LANE), 1) < hn)
    row2 = _iota2((chunk, c2), 0)
    col2 = _iota2((chunk, c2), 1) & (chunk - 1)
    strict2 = row2 > col2
    eye2 = (row2 == col2).astype(F32)
    incl4 = _iota2((chunk, 2 * c2), 0) >= (_iota2((chunk, 2 * c2), 1) & (chunk - 1))
    bd = (_iota2((c2, c2), 0) < chunk) == (_iota2((c2, c2), 1) < chunk)

    def by_head(x):
        return jnp.concatenate([jnp.where(m0, x, 0.0), jnp.where(m1, x, 0.0)], axis=0)

    def blockdiag(p2):
        return jnp.where(bd, jnp.concatenate([p2, p2], axis=0), 0.0)

    state = {p: s_scr[p] for p in range(pb)}
    applied = {p: 0 for p in range(pb)}

    def program(ps, j):
        rows = slice(j * chunk, (j + 1) * chunk)
        lanes = slice(ps[0] * LANE, (ps[-1] + 1) * LANE)
        sl = {p: slice(p * LANE, (p + 1) * LANE) for p in ps}
        loc = {p: slice((p - ps[0]) * LANE, (p - ps[0] + 1) * LANE) for p in ps}
        r = {p: r_ref[0, rows, sl[p]] for p in ps}
        k = {p: k_ref[0, rows, sl[p]] for p in ps}
        v = {p: v_ref[0, rows, sl[p]] for p in ps}
        lw = {p: lw_ref[0, rows, sl[p]] for p in ps}
        bv = {p: bv_ref[0, rows, sl[p]] for p in ps}
        cw_all = _cumsum_rows(lw_ref[0, rows, lanes])
        yield
        cw = {p: cw_all[:, loc[p]] for p in ps}
        tot = {p: cw[p][chunk - 1:chunk, :] for p in ps}
        e_neg = {p: jnp.exp(-cw[p]) for p in ps}
        e_end = {p: jnp.exp(tot[p] - cw[p]) for p in ps}
        lhs = {p: jnp.concatenate([av_ref[0, rows, sl[p]] * jnp.exp(cw[p] - lw[p]), r[p] * jnp.exp(cw[p])],
                                  axis=0) for p in ps}
        rhs_g = {p: jnp.concatenate([by_head(bv[p] * e_neg[p]), by_head(k[p] * e_neg[p])], axis=0) for p in ps}
        rhs_s = {p: jnp.concatenate([bv[p] * e_end[p], k[p] * e_end[p]], axis=0) for p in ps}
        vh = {p: by_head(v[p]) for p in ps}
        g = {p: _dot_nt(lhs[p], rhs_g[p]) for p in ps}
        yield
        avs = {p: _dot(jnp.where(strict2, g[p][:chunk, c2:], 0.0), vh[p]) for p in ps}
        m = {p: jnp.where(strict2, g[p][:chunk, :c2], 0.0) for p in ps}
        t2 = {p: eye2 + m[p] for p in ps}
        m = {p: _dot(m[p], blockdiag(m[p])) for p in ps}
        for _ in range(chunk.bit_length() - 3):
            yield
            res = {p: _dot(jnp.concatenate([m[p], t2[p]], axis=0), blockdiag(m[p])) for p in ps}
            m = {p: res[p][:chunk] for p in ps}
            t2 = {p: t2[p] + res[p][chunk:] for p in ps}
        yield
        t2 = {p: t2[p] + _dot(t2[p], blockdiag(m[p])) for p in ps}
        assert all(applied[p] == j for p in ps)
        s = {p: state[p] for p in ps}
        sr = {p: _dot_nt(lhs[p], s[p]) for p in ps}
        yield
        u = {p: _dot(t2[p], by_head(sr[p][:chunk] + avs[p])) for p in ps}
        yield
        yr = {p: _dot(jnp.where(incl4, g[p][chunk:], 0.0), jnp.concatenate([by_head(u[p]), vh[p]], axis=0))
              for p in ps}
        for p in ps:
            state[p] = jnp.where(blockmask, s[p] * jnp.exp(tot[p])
                                 + _dot_tn(jnp.concatenate([u[p], v[p]], axis=0), rhs_s[p]), 0.0)
            applied[p] = j + 1
        yield
        for p in ps:
            y = sr[p][chunk:] + yr[p]
            mean = _pair_sums(y, m0) * (1.0 / hn)
            d = y - mean
            var = _pair_sums(d * d, m0) * (1.0 / hn)
            yn = d * lax.rsqrt(var + RW_GN_EPS) * lnw_ref[:, sl[p]] + lnb_ref[:, sl[p]]
            bonus = _pair_sums(r[p] * k[p] * rk_ref[:, sl[p]], m0) * v[p]
            y_ref[0, rows, sl[p]] = ((yn + bonus) * gate_ref[0, rows, sl[p]]).astype(y_ref.dtype)

    gsz = _pick(pb, (4, 2, 1))
    _run_staggered([(j * REC_CHUNK_TICKS, program(range(g0, g0 + gsz), j))
                    for j in range(nch) for g0 in range(0, pb, gsz)])
    for p in range(pb):
        s_scr[p] = state[p]

    @pl.when(c == nc - 1)
    def _():
        sout_ref[0] = s_scr[...]


def _rw_recurrence(r, k, v, lw, av, bv, gate, r_k, ln_w, ln_b, s0p, *, pb, chunk):
    b, t, rw = r.shape
    pairs = rw // LANE
    groups = pairs // pb
    wb = pb * LANE
    nch = _pick(t // chunk, (REC_CHUNKS, 2, 1))
    kern = functools.partial(_rw_kernel, pb=pb, chunk=chunk, nch=nch)
    tile = pl.BlockSpec((1, nch * chunk, wb), lambda bb, g, c: (bb, c, g))
    par = pl.BlockSpec((1, wb), lambda bb, g, c: (0, g))
    st = pl.BlockSpec((1, pb, LANE, LANE), lambda bb, g, c: (bb, g, 0, 0))
    return pl.pallas_call(
        kern,
        grid=(b, groups, t // (nch * chunk)),
        in_specs=[tile] * 7 + [par] * 3 + [st],
        out_specs=[tile, st],
        out_shape=[jax.ShapeDtypeStruct((b, t, rw), BF16),
                   jax.ShapeDtypeStruct((b, pairs, LANE, LANE), F32)],
        scratch_shapes=[pltpu.VMEM((pb, LANE, LANE), F32)],
        compiler_params=_cparams(("parallel", "parallel", "arbitrary")),
        name="rw_recurrence",
    )(r, k, v, lw, av, bv, gate, r_k, ln_w, ln_b, s0p)


def _prepare(w):
    heads = w["dn_a_log"].shape[-1]
    dk = w["dn_norm_w"].shape[-1]
    qkv_w = w["dn_conv_w"].shape[-1]
    v_w = heads * dk
    assert qkv_w == 3 * v_w and dk == LANE
    rw_heads, rw_head = w["rw_r_k"].shape
    assert rw_head == LANE // 2
    rw = rw_heads * rw_head
    lw_n, la_n, lg_n = w["rw_w2"].shape[0], w["rw_a2"].shape[0], w["rw_g2"].shape[0]
    o1 = qkv_w
    o2 = o1 + v_w
    o4 = o2 + 2 * heads
    shift_w = 3 * rw + lw_n + la_n + lg_n
    assert o2 % LANE == 0 and 2 * heads <= LANE and w["w_in"].shape[1] == o4 + shift_w
    win0 = o4 // LANE * LANE
    win_off = o4 - win0
    win_w = _round_up(win_off + shift_w, LANE)
    win_bw = math.gcd(math.gcd(win0, win_w), 8 * LANE)
    assert win0 + win_w <= _round_up(o4 + shift_w, IN_TN)

    def lora_block(start, n, weight):
        lo, hi = start // LANE * LANE, _round_up(start + n, LANE)
        padded = jnp.pad(weight, ((start - lo, hi - start - n), (0, 0))).astype(BF16)
        return (lo, hi), padded

    xw_sl, w2p = lora_block(3 * rw, lw_n, w["rw_w2"])
    xa_sl, a2p = lora_block(3 * rw + lw_n, la_n, w["rw_a2"])
    xg_sl, g2p = lora_block(3 * rw + lw_n + la_n, lg_n, w["rw_g2"])
    lane_pad = lambda a: jnp.pad(a, (heads, LANE - 2 * heads)).reshape(1, LANE)
    in_window = lambda a: jnp.pad(a, [(0, 0)] * (a.ndim - 1) + [(win_off, win_w - win_off - shift_w)])
    return dict(
        heads=heads, dk=dk, rw=rw, rw_heads=rw_heads, o1=o1, o2=o2, o4=o4, shift_w=shift_w,
        win0=win0, win_off=win_off, win_w=win_w, win_bw=win_bw, in_window=in_window,
        xw_sl=xw_sl, xa_sl=xa_sl, xg_sl=xg_sl, w2p=w2p, a2p=a2p, g2p=g2p,
        alog_row=lane_pad(w["dn_a_log"]), dtb_row=lane_pad(w["dn_dt_bias"]),
        mu_win=in_window(w["rw_mu"].reshape(1, shift_w)),
        w0=w["rw_w0"].reshape(1, rw), a0=w["rw_a0"].reshape(1, rw),
        k_k=w["rw_k_k"].reshape(1, rw), k_a=w["rw_k_a"].reshape(1, rw),
        w_in_t=jnp.swapaxes(w["w_in"], 0, 1),
        w_down=w["w_down"].astype(BF16),
    )


def _layer(x, dn_state, dn_conv, rw_state, rw_shift, w, pp):
    b, t, d = x.shape
    m = b * t
    heads, dk, rw, rw_heads = pp["heads"], pp["dk"], pp["rw"], pp["rw_heads"]
    o1, o2, o4, shift_w = pp["o1"], pp["o2"], pp["o4"], pp["shift_w"]
    chunk = 64 if t % 64 == 0 else t
    assert chunk & (chunk - 1) == 0 and chunk >= 2 * SUBLANE
    hb = 16 if heads % 16 == 0 else heads
    pairs = rw // LANE
    pb = 16 if pairs % 16 == 0 else pairs
    tm = _pick(m, (1024, 512, 256, 128))

    xf = x.reshape(m, d)
    h = _rms_bf16(xf, w["g_mix_pre"])
    p = _matmul_w32t(h, pp["w_in_t"], tm=tm, tn=IN_TN, name="in_proj")
    p = p.reshape(b, t, p.shape[1])

    cache8 = jnp.pad(dn_conv.astype(F32), ((0, 0), (SUBLANE - dn_conv.shape[1], 0), (0, 0)))
    qkv, gb = _dn_pre(p, cache8, w["dn_conv_w"], pp["alog_row"], pp["dtb_row"],
                      heads=heads, dk=dk, hb=hb, c_ba=o2)
    y_a, new_dn_state = _dn_recurrence(qkv, gb, p, w["dn_norm_w"], dn_state.astype(F32),
                                       heads=heads, hb=hb, dk=dk, chunk=chunk, c_z=o1)
    new_dn_conv = p[:, t - dn_conv.shape[1]:, :o1]

    shift8 = jnp.pad(pp["in_window"](rw_shift.astype(F32)), ((0, 0), (SUBLANE - 1, 0), (0, 0)))
    r, k, v, lw, av, bv, gate = _rw_pre(p, shift8, pp)
    hn = LANE // 2
    s4 = rw_state.astype(F32).reshape(b, pairs, 2, hn, hn)
    zeros = jnp.zeros_like(s4[:, :, 0])
    s0p = jnp.concatenate([jnp.concatenate([s4[:, :, 0], zeros], axis=-1),
                           jnp.concatenate([zeros, s4[:, :, 1]], axis=-1)], axis=-2)
    y_b, sp = _rw_recurrence(r, k, v, lw, av, bv, gate, w["rw_r_k"].reshape(1, rw),
                             w["rw_ln_w"].reshape(1, rw), w["rw_ln_b"].reshape(1, rw), s0p,
                             pb=pb, chunk=chunk)
    new_rw_state = jnp.stack([sp[:, :, :hn, :hn], sp[:, :, hn:, hn:]], axis=2).reshape(b, rw_heads, hn, hn)
    new_rw_shift = p[:, t - 1:, o4:o4 + shift_w]

    mixo = _matmul2(y_a.reshape(m, heads * dk), y_b.reshape(m, rw), w["w_out"],
                    tm=tm, tn=_pick(d, (512, 256, 128)), name="out_proj")
    x1, h2 = _resid_rms(xf, mixo, w["g_mix_post"], w["g_ffn_pre"])
    dff = w["w_gate"].shape[1]
    f = _swiglu_up(h2, w["w_gate"], w["w_up"], tm=tm, tn=_pick(dff, (256, 128)))
    fo = _matmul(f, pp["w_down"], tm=_pick(m, (512, 256, 128)), tn=_pick(d, (256, 128)),
                 out_dtype=BF16, name="ffn_down")
    out = _resid_final(x1, fo, w["g_ffn_post"]).reshape(b, t, d)
    return out, (new_dn_state, new_dn_conv, new_rw_state, new_rw_shift)


_WEIGHT_NAMES = ("g_mix_pre", "g_mix_post", "w_in", "dn_conv_w", "dn_a_log", "dn_dt_bias", "dn_norm_w",
                 "rw_mu", "rw_w0", "rw_w2", "rw_a0", "rw_a2", "rw_g2", "rw_k_k", "rw_k_a", "rw_r_k",
                 "rw_ln_w", "rw_ln_b", "w_out", "g_ffn_pre", "g_ffn_post", "w_gate", "w_up", "w_down")


def kernel(x_prompt, x_sample, state_dn, cache_dn_conv, state_rwkv, cache_rwkv_shift,
           g_mix_pre, g_mix_post, w_in, dn_conv_w, dn_a_log, dn_dt_bias, dn_norm_w,
           rw_mu, rw_w0, rw_w2, rw_a0, rw_a2, rw_g2, rw_k_k, rw_k_a, rw_r_k, rw_ln_w, rw_ln_b,
           w_out, g_ffn_pre, g_ffn_post, w_gate, w_up, w_down):
    stacked = (g_mix_pre, g_mix_post, w_in, dn_conv_w, dn_a_log, dn_dt_bias, dn_norm_w,
               rw_mu, rw_w0, rw_w2, rw_a0, rw_a2, rw_g2, rw_k_k, rw_k_a, rw_r_k, rw_ln_w, rw_ln_b,
               w_out, g_ffn_pre, g_ffn_post, w_gate, w_up, w_down)
    depth = w_in.shape[0]
    bp = x_prompt.shape[0]
    dt = x_prompt.dtype
    yp, ys = x_prompt, x_sample
    outs_p, outs_s = [], []
    for l in range(depth):
        w = {n: a[l] for n, a in zip(_WEIGHT_NAMES, stacked)}
        pp = _prepare(w)
        heads, dk, rw_heads = pp["heads"], pp["dk"], pp["rw_heads"]
        hn = LANE // 2
        yp, st_p = _layer(yp,
                          jnp.zeros((bp, heads, dk, dk), dt),
                          jnp.zeros((bp, cache_dn_conv.shape[2], cache_dn_conv.shape[3]), dt),
                          jnp.zeros((bp, rw_heads, hn, hn), dt),
                          jnp.zeros((bp, 1, cache_rwkv_shift.shape[3]), dt), w, pp)
        ys, st_s = _layer(ys, state_dn[l], cache_dn_conv[l], state_rwkv[l], cache_rwkv_shift[l], w, pp)
        outs_p.append(st_p)
        outs_s.append(st_s)
    stack = lambda outs, i: jnp.stack([o[i] for o in outs])
    return (yp, ys,
            stack(outs_p, 0), stack(outs_p, 1), stack(outs_p, 2), stack(outs_p, 3),
            stack(outs_s, 0), stack(outs_s, 1), stack(outs_s, 2), stack(outs_s, 3))
```

```python
import functools
import math

import jax
import jax.numpy as jnp
from jax import lax
from jax.experimental import pallas as pl
from jax.experimental.pallas import tpu as pltpu

F32 = jnp.float32
BF16 = jnp.bfloat16

LANE = 128
SUBLANE = 8
VMEM_LIMIT_BYTES = 56 * 2**20
NORM_EPS = 1e-6
L2_EPS = 1e-6
RW_GN_EPS = 64e-5
IN_TN = 512
IN_TM_MAX = 1024
REC_CHUNKS = 4
REC_CHUNK_TICKS = 3


def _cparams(sem):
    return pltpu.CompilerParams(dimension_semantics=sem, vmem_limit_bytes=VMEM_LIMIT_BYTES)


def _round_up(x, m):
    return -(-x // m) * m


def _pick(n, cands):
    for c in cands:
        if n % c == 0:
            return c
    return n


def _dot(a, b):
    return jnp.dot(a.astype(BF16), b.astype(BF16), preferred_element_type=F32)


def _dot_nt(a, b):
    return lax.dot_general(a.astype(BF16), b.astype(BF16), (((1,), (1,)), ((), ())),
                           preferred_element_type=F32)


def _dot_tn(a, b):
    return lax.dot_general(a.astype(BF16), b.astype(BF16), (((0,), (0,)), ((), ())),
                           preferred_element_type=F32)


def _cumsum_rows(x):
    c = x.shape[0]
    tri = (_iota2((c, c), 0) >= _iota2((c, c), 1)).astype(BF16)
    hi = x.astype(BF16)
    r1 = x - hi.astype(F32)
    mid = r1.astype(BF16)
    lo = (r1 - mid.astype(F32)).astype(BF16)
    dot = lambda p: jnp.dot(tri, p, preferred_element_type=F32)
    return dot(hi) + (dot(mid) + dot(lo))


def _sigmoid(x):
    return 0.5 * jnp.tanh(0.5 * x) + 0.5


def _softplus(x):
    return jnp.maximum(x, 0.0) + jnp.log(1.0 + jnp.exp(-jnp.abs(x)))


def _iota2(shape, dim):
    return lax.broadcasted_iota(jnp.int32, shape, dim)


def _run_staggered(programs):
    live = list(programs)
    tick = 0
    while live:
        still = []
        for start, prog in live:
            if tick >= start:
                try:
                    next(prog)
                except StopIteration:
                    continue
            still.append((start, prog))
        live = still
        tick += 1


def _rms(x, g):
    return x * lax.rsqrt(jnp.mean(x * x, axis=-1, keepdims=True) + NORM_EPS) * g


def _rms_kernel(x_ref, g_ref, o_ref):
    o_ref[...] = _rms(x_ref[...], g_ref[...]).astype(o_ref.dtype)


def _rms_bf16(x, g):
    m, d = x.shape
    tm = _pick(m, (256, 128))
    return pl.pallas_call(
        _rms_kernel,
        grid=(m // tm,),
        in_specs=[pl.BlockSpec((tm, d), lambda i: (i, 0)), pl.BlockSpec((1, d), lambda i: (0, 0))],
        out_specs=pl.BlockSpec((tm, d), lambda i: (i, 0)),
        out_shape=jax.ShapeDtypeStruct((m, d), BF16),
        compiler_params=_cparams(("parallel",)),
        name="rms_pre",
    )(x, g.reshape(1, d))


def _resid_rms_kernel(x_ref, y_ref, gp_ref, gn_ref, x1_ref, h_ref):
    x1 = x_ref[...] + _rms(y_ref[...].astype(F32), gp_ref[...])
    x1_ref[...] = x1
    h_ref[...] = _rms(x1, gn_ref[...]).astype(h_ref.dtype)


def _resid_rms(x, y, g_post, g_next):
    m, d = x.shape
    tm = _pick(m, (256, 128))
    row = pl.BlockSpec((tm, d), lambda i: (i, 0))
    par = pl.BlockSpec((1, d), lambda i: (0, 0))
    return pl.pallas_call(
        _resid_rms_kernel,
        grid=(m // tm,),
        in_specs=[row, row, par, par],
        out_specs=[row, row],
        out_shape=[jax.ShapeDtypeStruct((m, d), F32), jax.ShapeDtypeStruct((m, d), BF16)],
        compiler_params=_cparams(("parallel",)),
        name="resid_rms",
    )(x, y, g_post.reshape(1, d), g_next.reshape(1, d))


def _resid_final_kernel(x_ref, y_ref, g_ref, o_ref):
    o_ref[...] = x_ref[...] + _rms(y_ref[...].astype(F32), g_ref[...])


def _resid_final(x, y, g):
    m, d = x.shape
    tm = _pick(m, (256, 128))
    row = pl.BlockSpec((tm, d), lambda i: (i, 0))
    return pl.pallas_call(
        _resid_final_kernel,
        grid=(m // tm,),
        in_specs=[row, row, pl.BlockSpec((1, d), lambda i: (0, 0))],
        out_specs=row,
        out_shape=jax.ShapeDtypeStruct((m, d), F32),
        compiler_params=_cparams(("parallel",)),
        name="resid_final",
    )(x, y, g.reshape(1, d))


def _mm_kernel(x_ref, w_ref, o_ref):
    o_ref[...] = jnp.dot(x_ref[...], w_ref[...], preferred_element_type=F32).astype(o_ref.dtype)


def _matmul(x, w, *, tm, tn, out_dtype, name):
    m, k = x.shape
    n = w.shape[1]
    return pl.pallas_call(
        _mm_kernel,
        grid=(m // tm, n // tn),
        in_specs=[pl.BlockSpec((tm, k), lambda i, j: (i, 0)), pl.BlockSpec((k, tn), lambda i, j: (0, j))],
        out_specs=pl.BlockSpec((tm, tn), lambda i, j: (i, j)),
        out_shape=jax.ShapeDtypeStruct((m, n), out_dtype),
        compiler_params=_cparams(("parallel", "arbitrary")),
        name=name,
    )(x, w)


def _mm_w32t_kernel(x_ref, wt_ref, o_ref, *, n_valid):
    wt = wt_ref[...]
    tn = wt.shape[0]
    if n_valid % tn:
        wt = jnp.where(pl.program_id(1) * tn + _iota2(wt.shape, 0) < n_valid, wt, 0.0)
    o_ref[...] = lax.dot_general(x_ref[...], wt.astype(BF16), (((1,), (1,)), ((), ())),
                                 preferred_element_type=F32)


def _matmul_w32t(x, wt, *, tm, tn, name):
    m, k = x.shape
    n = wt.shape[0]
    nt = pl.cdiv(n, tn)
    return pl.pallas_call(
        functools.partial(_mm_w32t_kernel, n_valid=n),
        grid=(m // tm, nt),
        in_specs=[pl.BlockSpec((tm, k), lambda i, j: (i, 0)), pl.BlockSpec((tn, k), lambda i, j: (j, 0))],
        out_specs=pl.BlockSpec((tm, tn), lambda i, j: (i, j)),
        out_shape=jax.ShapeDtypeStruct((m, nt * tn), F32),
        compiler_params=_cparams(("parallel", "arbitrary")),
        name=name,
    )(x, wt)


def _mm2_kernel(xa_ref, xb_ref, wa_ref, wb_ref, o_ref):
    acc = jnp.dot(xa_ref[...], wa_ref[...].astype(BF16), preferred_element_type=F32)
    acc = acc + jnp.dot(xb_ref[...], wb_ref[...].astype(BF16), preferred_element_type=F32)
    o_ref[...] = acc.astype(o_ref.dtype)


def _matmul2(xa, xb, w, *, tm, tn, name):
    m, ka = xa.shape
    kb = xb.shape[1]
    n = w.shape[1]
    assert ka == kb and w.shape[0] == ka + kb
    return pl.pallas_call(
        _mm2_kernel,
        grid=(m // tm, n // tn),
        in_specs=[pl.BlockSpec((tm, ka), lambda i, j: (i, 0)), pl.BlockSpec((tm, kb), lambda i, j: (i, 0)),
                  pl.BlockSpec((ka, tn), lambda i, j: (0, j)), pl.BlockSpec((kb, tn), lambda i, j: (1, j))],
        out_specs=pl.BlockSpec((tm, tn), lambda i, j: (i, j)),
        out_shape=jax.ShapeDtypeStruct((m, n), BF16),
        compiler_params=_cparams(("parallel", "arbitrary")),
        name=name,
    )(xa, xb, w, w)


def _swiglu_kernel(x_ref, wg_ref, wu_ref, o_ref):
    x = x_ref[...]
    g = jnp.dot(x, wg_ref[...].astype(BF16), preferred_element_type=F32)
    u = jnp.dot(x, wu_ref[...].astype(BF16), preferred_element_type=F32)
    o_ref[...] = (g * _sigmoid(g) * u).astype(o_ref.dtype)


def _swiglu_up(x, wg, wu, *, tm, tn):
    m, k = x.shape
    n = wg.shape[1]
    wspec = pl.BlockSpec((k, tn), lambda i, j: (0, j))
    return pl.pallas_call(
        _swiglu_kernel,
        grid=(m // tm, n // tn),
        in_specs=[pl.BlockSpec((tm, k), lambda i, j: (i, 0)), wspec, wspec],
        out_specs=pl.BlockSpec((tm, tn), lambda i, j: (i, j)),
        out_shape=jax.ShapeDtypeStruct((m, n), BF16),
        compiler_params=_cparams(("parallel", "arbitrary")),
        name="ffn_up",
    )(x, wg, wu)


def _shift_rows(u, halo, s):
    ru = pltpu.roll(u, s, axis=0)
    rh = pltpu.roll(halo, s, axis=0)
    top = jnp.where(_iota2(halo.shape, 0) < s, rh, ru[:SUBLANE])
    if u.shape[0] == SUBLANE:
        return top
    return jnp.concatenate([top, ru[SUBLANE:]], axis=0)


def _conv_silu_strips(load, first_halo, cw_ref, store, *, tm, heads, dk, scale, normalize=None, rows=64):
    rb = min(tm, rows)
    for h in range(heads):
        cols = slice(h * dk, (h + 1) * dk)
        cw = cw_ref[:, cols]
        for r0 in range(0, tm, rb):
            u = load(r0, r0 + rb, cols)
            halo = first_halo(cols) if r0 == 0 else load(r0 - SUBLANE, r0, cols)
            prev = [_shift_rows(u, halo, sh) for sh in (3, 2, 1)]
            conv = prev[0] * cw[0:1]
            conv = conv + prev[1] * cw[1:2]
            conv = conv + prev[2] * cw[2:3]
            conv = conv + u * cw[3:4]
            s = conv * _sigmoid(conv)
            if scale is not None:
                normed = s * (lax.rsqrt(jnp.sum(s * s, axis=-1, keepdims=True) + L2_EPS) * scale)
                s = normed if normalize is None else jnp.where(normalize, normed, s)
            store(r0, r0 + rb, cols, s)


def _dn_pre_kernel(p_ref, halo_ref, cache_ref, cw_ref, qkv_ref, *, heads, dk):
    i = pl.program_id(1)
    j = pl.program_id(2)
    tm = p_ref.shape[1]

    def store(r0, r1, cols, value):
        qkv_ref[0, r0:r1, cols] = value

    def strips(scale):
        _conv_silu_strips(lambda r0, r1, cols: p_ref[0, r0:r1, cols],
                          lambda cols: jnp.where(i == 0, cache_ref[0, :, cols], halo_ref[0, :, cols]),
                          cw_ref, store, tm=tm, heads=heads, dk=dk, scale=scale)

    @pl.when(j < 2)
    def _():
        strips(jnp.where(j == 0, dk ** -0.5, 1.0).astype(F32))

    @pl.when(j == 2)
    def _():
        strips(None)


def _dn_pre(p, cache8, conv_w, *, heads, dk):
    b, t, _ = p.shape
    w = heads * dk
    tm = _pick(t, (256, 128, 64, 32, 16))
    kern = functools.partial(_dn_pre_kernel, heads=heads, dk=dk)
    return pl.pallas_call(
        kern,
        grid=(b, t // tm, 3),
        in_specs=[
            pl.BlockSpec((1, tm, w), lambda bb, i, j: (bb, i, j)),
            pl.BlockSpec((1, SUBLANE, w), lambda bb, i, j: (bb, jnp.maximum(i * (tm // SUBLANE) - 1, 0), j)),
            pl.BlockSpec((1, SUBLANE, w), lambda bb, i, j: (bb, 0, j)),
            pl.BlockSpec((4, w), lambda bb, i, j: (0, j)),
        ],
        out_specs=pl.BlockSpec((1, tm, w), lambda bb, i, j: (bb, i, j)),
        out_shape=jax.ShapeDtypeStruct((b, t, 3 * w), F32),
        compiler_params=_cparams(("parallel", "parallel", "arbitrary")),
        name="dn_pre",
    )(p, p, cache8, conv_w)


def _dn_gates_kernel(ba_ref, alog_ref, dtb_ref, gb_ref, *, heads, hb):
    x = ba_ref[0]
    lane = _iota2(x.shape, 1)
    g = -jnp.exp(alog_ref[...]) * _softplus(x + dtb_ref[...])
    full = jnp.where(lane < heads, _sigmoid(x), g)
    for hg in range(heads // hb):
        gb_ref[0, hg] = full if hg == 0 else pltpu.roll(full, LANE - hg * hb, axis=1)


def _dn_gates(p, alog_row, dtb_row, *, heads, hb, c_ba):
    b, t, _ = p.shape
    tm = _pick(t, (1024, 512, 256, 128, 64, 32, 16))
    groups = heads // hb
    return pl.pallas_call(
        functools.partial(_dn_gates_kernel, heads=heads, hb=hb),
        grid=(b, t // tm),
        in_specs=[pl.BlockSpec((1, tm, LANE), lambda bb, i: (bb, i, c_ba // LANE)),
                  pl.BlockSpec((1, LANE), lambda bb, i: (0, 0)),
                  pl.BlockSpec((1, LANE), lambda bb, i: (0, 0))],
        out_specs=pl.BlockSpec((1, groups, tm, LANE), lambda bb, i: (bb, 0, i, 0)),
        out_shape=jax.ShapeDtypeStruct((b, groups, t, LANE), F32),
        compiler_params=_cparams(("parallel", "parallel")),
        name="dn_gates",
    )(p, alog_row, dtb_row)


def _in_proj_dn_kernel(x_ref, wt_ref, cache_ref, cw_ref, p_ref, qkv_ref, raw_scr, halo_scr,
                       *, n_valid, nq, dk, tiles_per_seq):
    i = pl.program_id(0)
    j = pl.program_id(1)
    tm, tn = p_ref.shape
    jj = jnp.clip(j - 1, 0, nq - 1)
    first = (i % tiles_per_seq) == 0
    active = (j >= 1) & (j <= nq)

    @pl.when((i == 0) & (j == 0))
    def _():
        raw_scr[...] = jnp.zeros_like(raw_scr)
        halo_scr[...] = jnp.zeros_like(halo_scr)

    def store(r0, r1, cols, value):
        qkv_ref[r0:r1, cols] = value

    def matmul():
        wt = wt_ref[...]
        if n_valid % tn:
            wt = jnp.where(j * tn + _iota2(wt.shape, 0) < n_valid, wt, 0.0)
        acc = lax.dot_general(x_ref[...], wt.astype(BF16), (((1,), (1,)), ((), ())),
                              preferred_element_type=F32)
        p_ref[...] = acc
        return acc

    def fused_step(write_slot):
        read_slot = 1 - write_slot
        raw_scr[write_slot] = matmul()
        _conv_silu_strips(lambda r0, r1, cols: raw_scr[read_slot, r0:r1, cols],
                          lambda cols: jnp.where(first, cache_ref[0, :, cols], halo_scr[jj, :, cols]),
                          cw_ref, store, tm=tm, heads=tn // dk, dk=dk,
                          scale=jnp.where(jj < nq // 3, dk ** -0.5, 1.0).astype(F32),
                          normalize=jj < 2 * nq // 3)
        halo_scr[jj] = raw_scr[read_slot, tm - SUBLANE:tm, :]

    for parity in range(2):
        pl.when(active & (j % 2 == parity))(functools.partial(fused_step, parity))

    @pl.when(jnp.logical_not(active))
    def _():
        raw_scr[0] = matmul()


def _in_proj_dn(x, wt, cache8, conv_w, *, t, tm, tn, o1, dk):
    m, k = x.shape
    n = wt.shape[0]
    nt = pl.cdiv(n, tn)
    nq = o1 // tn
    tiles_per_seq = t // tm
    assert o1 % (3 * tn) == 0 and t % tm == 0 and tn % dk == 0 and nt > nq
    qcol = lambda i, j: jnp.clip(j - 1, 0, nq - 1)
    kern = functools.partial(_in_proj_dn_kernel, n_valid=n, nq=nq, dk=dk, tiles_per_seq=tiles_per_seq)
    return pl.pallas_call(
        kern,
        grid=(m // tm, nt),
        in_specs=[pl.BlockSpec((tm, k), lambda i, j: (i, 0)),
                  pl.BlockSpec((tn, k), lambda i, j: (j, 0)),
                  pl.BlockSpec((1, SUBLANE, tn), lambda i, j: (i // tiles_per_seq, 0, qcol(i, j))),
                  pl.BlockSpec((4, tn), lambda i, j: (0, qcol(i, j)))],
        out_specs=[pl.BlockSpec((tm, tn), lambda i, j: (i, j)),
                   pl.BlockSpec((tm, tn), lambda i, j: (i, qcol(i, j)))],
        out_shape=[jax.ShapeDtypeStruct((m, nt * tn), F32), jax.ShapeDtypeStruct((m, o1), F32)],
        scratch_shapes=[pltpu.VMEM((2, tm, tn), F32), pltpu.VMEM((nq, SUBLANE, tn), F32)],
        compiler_params=_cparams(("arbitrary", "arbitrary")),
        name="in_proj",
    )(x, wt, cache8, conv_w)


def _dn_kernel(q_ref, k_ref, v_ref, gb_ref, z_ref, nw_ref, s0_ref, y_ref, sout_ref, s_scr,
               *, heads, hb, dk, chunk, nch):
    c = pl.program_id(2)
    nc = pl.num_programs(2)

    @pl.when(c == 0)
    def _():
        s_scr[...] = s0_ref[0]

    row = _iota2((chunk, chunk), 0)
    col = _iota2((chunk, chunk), 1)
    causal = row >= col
    strict = row > col
    nw = nw_ref[...]
    eye = (row == col).astype(F32)
    state = {h: s_scr[h] for h in range(hb)}
    applied = {h: 0 for h in range(hb)}
    per_chunk = {}

    def chunk_gates(j):
        if j not in per_chunk:
            gbt = gb_ref[0, 0, j * chunk:(j + 1) * chunk, :]
            gc_all = _cumsum_rows(gbt)
            per_chunk[j] = (gbt, gc_all, jnp.transpose(gc_all))
        return per_chunk[j]

    def program(hs, j):
        rows = slice(j * chunk, (j + 1) * chunk)
        gbt, gc_all, gc_t = chunk_gates(j)
        sl = {h: slice(h * dk, (h + 1) * dk) for h in hs}
        q = {h: q_ref[0, rows, sl[h]] for h in hs}
        k = {h: k_ref[0, rows, sl[h]] for h in hs}
        v = {h: v_ref[0, rows, sl[h]] for h in hs}
        beta = {h: gbt[:, h:h + 1] for h in hs}
        gcol = {h: gc_all[:, heads + h:heads + h + 1] for h in hs}
        glast = {h: gc_all[chunk - 1:chunk, heads + h:heads + h + 1] for h in hs}
        decay = {h: jnp.where(causal, jnp.exp(gcol[h] - gc_t[heads + h:heads + h + 1, :]), 0.0) for h in hs}
        kb = {h: k[h] * beta[h] for h in hs}
        eg = {h: jnp.exp(gcol[h]) for h in hs}
        yield
        m = {h: jnp.where(strict, -_dot_nt(kb[h], k[h]) * decay[h], 0.0) for h in hs}
        qk = {h: jnp.where(causal, _dot_nt(q[h], k[h]) * decay[h], 0.0) for h in hs}
        yield
        t = {h: eye + m[h] for h in hs}
        m = {h: _dot(m[h], m[h]) for h in hs}
        for _ in range(chunk.bit_length() - 3):
            yield
            res = {h: _dot(jnp.concatenate([m[h], t[h]], axis=0), m[h]) for h in hs}
            m = {h: res[h][:chunk] for h in hs}
            t = {h: t[h] + res[h][chunk:] for h in hs}
        yield
        t = {h: t[h] + _dot(t[h], m[h]) for h in hs}
        yield
        sol = {h: _dot(t[h], jnp.concatenate([v[h] * beta[h], kb[h] * eg[h]], axis=1)) for h in hs}
        yield
        assert all(applied[h] == j for h in hs)
        s = {h: state[h] for h in hs}
        v_new = {h: sol[h][:, :dk] - _dot(sol[h][:, dk:], s[h]) for h in hs}
        os = {h: _dot(q[h] * eg[h], s[h]) for h in hs}
        yield
        o = {h: os[h] + _dot(qk[h], v_new[h]) for h in hs}
        for h in hs:
            state[h] = s[h] * jnp.exp(glast[h]) + _dot_tn(k[h] * jnp.exp(glast[h] - gcol[h]), v_new[h])
            applied[h] = j + 1
        yield
        for h in hs:
            z = z_ref[0, rows, sl[h]]
            y_ref[0, rows, sl[h]] = (_rms(o[h], nw) * (z * _sigmoid(z))).astype(y_ref.dtype)

    gsz = _pick(hb, (4, 2, 1))
    _run_staggered([(j * REC_CHUNK_TICKS, program(range(g0, g0 + gsz), j))
                    for j in range(nch) for g0 in range(0, hb, gsz)])
    for h in range(hb):
        s_scr[h] = state[h]

    @pl.when(c == nc - 1)
    def _():
        sout_ref[0] = s_scr[...]


def _dn_recurrence(qkv, gb, p, norm_w, s0, *, heads, hb, dk, chunk, c_z):
    b, t, _ = qkv.shape
    groups = heads // hb
    wb = hb * dk
    nqk = heads * dk // wb
    nch = _pick(t // chunk, (REC_CHUNKS, 2, 1))
    rows = nch * chunk
    kern = functools.partial(_dn_kernel, heads=heads, hb=hb, dk=dk, chunk=chunk, nch=nch)
    return pl.pallas_call(
        kern,
        grid=(b, groups, t // rows),
        in_specs=[
            pl.BlockSpec((1, rows, wb), lambda bb, g, c: (bb, c, g)),
            pl.BlockSpec((1, rows, wb), lambda bb, g, c: (bb, c, nqk + g)),
            pl.BlockSpec((1, rows, wb), lambda bb, g, c: (bb, c, 2 * nqk + g)),
            pl.BlockSpec((1, 1, rows, LANE), lambda bb, g, c: (bb, g, c, 0)),
            pl.BlockSpec((1, rows, wb), lambda bb, g, c: (bb, c, c_z // wb + g)),
            pl.BlockSpec((1, dk), lambda bb, g, c: (0, 0)),
            pl.BlockSpec((1, hb, dk, dk), lambda bb, g, c: (bb, g, 0, 0)),
        ],
        out_specs=[
            pl.BlockSpec((1, rows, wb), lambda bb, g, c: (bb, c, g)),
            pl.BlockSpec((1, hb, dk, dk), lambda bb, g, c: (bb, g, 0, 0)),
        ],
        out_shape=[jax.ShapeDtypeStruct((b, t, heads * dk), BF16),
                   jax.ShapeDtypeStruct((b, heads, dk, dk), F32)],
        scratch_shapes=[pltpu.VMEM((hb, dk, dk), F32)],
        compiler_params=_cparams(("parallel", "parallel", "arbitrary")),
        name="dn_recurrence",
    )(qkv, qkv, qkv, gb, p, norm_w.reshape(1, dk), s0)


def _pair_sums(x, m0):
    s0 = jnp.sum(jnp.where(m0, x, 0.0), axis=-1, keepdims=True)
    s1 = jnp.sum(jnp.where(m0, 0.0, x), axis=-1, keepdims=True)
    return jnp.where(m0, s0, s1)


def _rw_pre_kernel(*refs, nblk, off, rw, xw_sl, xa_sl, xg_sl):
    main, halos = refs[:nblk], refs[nblk:2 * nblk]
    (cache_ref, mu_ref, w0_ref, w2_ref, a0_ref, a2_ref, g2_ref, kk_ref, ka_ref,
     r_ref, k_ref, v_ref, lw_ref, av_ref, bv_ref, gate_ref) = refs[2 * nblk:]
    u = jnp.concatenate([ref[0] for ref in main], axis=1)
    halo = jnp.concatenate([ref[0] for ref in halos], axis=1)
    halo = jnp.where(pl.program_id(1) == 0, cache_ref[0], halo)
    x = u + mu_ref[...] * (_shift_rows(u, halo, 1) - u)
    x = pltpu.roll(x, x.shape[1] - off, axis=1)
    r = x[:, :rw]
    kr = x[:, rw:2 * rw]
    vr = x[:, 2 * rw:3 * rw]
    w_log = -_softplus(-(w0_ref[...] + _dot(jnp.tanh(x[:, xw_sl[0]:xw_sl[1]]), w2_ref[...]))) - 0.5
    a = _sigmoid(a0_ref[...] + _dot(x[:, xa_sl[0]:xa_sl[1]], a2_ref[...]))
    gate_ref[0] = _dot(_sigmoid(x[:, xg_sl[0]:xg_sl[1]]), g2_ref[...])
    r_ref[0] = r
    v_ref[0] = vr
    lw_ref[0] = -jnp.exp(w_log)
    k_ref[0] = kr * (1.0 + (a - 1.0) * ka_ref[...])
    kkr = kr * kk_ref[...]
    m0 = _iota2((1, LANE), 1) < LANE // 2
    for jb in range(rw // LANE):
        sl = slice(jb * LANE, (jb + 1) * LANE)
        blk = kkr[:, sl]
        kk = blk * lax.rsqrt(_pair_sums(blk * blk, m0) + L2_EPS)
        av_ref[0, :, sl] = -kk
        bv_ref[0, :, sl] = kk * a[:, sl]


def _rw_pre(p, cache8, pp):
    b, t, _ = p.shape
    rw, win0, width, bw = pp["rw"], pp["win0"], pp["win_w"], pp["win_bw"]
    nblk = width // bw
    tm = _pick(t, (128, 64, 32, 16))
    kern = functools.partial(_rw_pre_kernel, nblk=nblk, off=pp["win_off"], rw=rw,
                             xw_sl=pp["xw_sl"], xa_sl=pp["xa_sl"], xg_sl=pp["xg_sl"])
    full = lambda shape: pl.BlockSpec(shape, lambda bb, i: (0,) * len(shape))
    row = pl.BlockSpec((1, tm, rw), lambda bb, i: (bb, i, 0))
    cb = [win0 // bw + n for n in range(nblk)]
    main = [pl.BlockSpec((1, tm, bw), lambda bb, i, c=c: (bb, i, c)) for c in cb]
    halos = [pl.BlockSpec((1, SUBLANE, bw),
                          lambda bb, i, c=c: (bb, jnp.maximum(i * (tm // SUBLANE) - 1, 0), c)) for c in cb]
    small = [pp["mu_win"], pp["w0"], pp["w2p"], pp["a0"], pp["a2p"], pp["g2p"], pp["k_k"], pp["k_a"]]
    return pl.pallas_call(
        kern,
        grid=(b, t // tm),
        in_specs=main + halos + [pl.BlockSpec((1, SUBLANE, width), lambda bb, i: (bb, 0, 0))]
        + [full(a.shape) for a in small],
        out_specs=[row] * 7,
        out_shape=[jax.ShapeDtypeStruct((b, t, rw), F32)] * 7,
        compiler_params=_cparams(("parallel", "arbitrary")),
        name="rw_pre",
    )(*([p] * (2 * nblk)), cache8, *small)


def _rw_kernel(r_ref, k_ref, v_ref, lw_ref, av_ref, bv_ref, gate_ref, rk_ref, lnw_ref, lnb_ref, s0_ref,
               y_ref, sout_ref, s_scr, *, pb, chunk, nch):
    c = pl.program_id(2)
    nc = pl.num_programs(2)
    hn = LANE // 2

    @pl.when(c == 0)
    def _():
        s_scr[...] = s0_ref[0]

    c2 = 2 * chunk
    m0 = _iota2((1, LANE), 1) < hn
    m1 = jnp.logical_not(m0)
    blockmask = (_iota2((LANE, LANE), 0) < hn) == (_iota2((LANE, LANE), 1) < hn)
    row2 = _iota2((chunk, c2), 0)
    col2 = _iota2((chunk, c2), 1) & (chunk - 1)
    strict2 = row2 > col2
    eye2 = (row2 == col2).astype(F32)
    incl4 = _iota2((chunk, 2 * c2), 0) >= (_iota2((chunk, 2 * c2), 1) & (chunk - 1))
    bd = (_iota2((c2, c2), 0) < chunk) == (_iota2((c2, c2), 1) < chunk)

    def by_head(x):
        return jnp.concatenate([jnp.where(m0, x, 0.0), jnp.where(m1, x, 0.0)], axis=0)

    def blockdiag(p2):
        return jnp.where(bd, jnp.concatenate([p2, p2], axis=0), 0.0)

    state = {p: s_scr[p] for p in range(pb)}
    applied = {p: 0 for p in range(pb)}

    def program(ps, j):
        rows = slice(j * chunk, (j + 1) * chunk)
        lanes = slice(ps[0] * LANE, (ps[-1] + 1) * LANE)
        sl = {p: slice(p * LANE, (p + 1) * LANE) for p in ps}
        loc = {p: slice((p - ps[0]) * LANE, (p - ps[0] + 1) * LANE) for p in ps}
        r = {p: r_ref[0, rows, sl[p]] for p in ps}
        k = {p: k_ref[0, rows, sl[p]] for p in ps}
        v = {p: v_ref[0, rows, sl[p]] for p in ps}
        lw = {p: lw_ref[0, rows, sl[p]] for p in ps}
        bv = {p: bv_ref[0, rows, sl[p]] for p in ps}
        cw_all = _cumsum_rows(lw_ref[0, rows, lanes])
        yield
        cw = {p: cw_all[:, loc[p]] for p in ps}
        tot = {p: cw[p][chunk - 1:chunk, :] for p in ps}
        e_neg = {p: jnp.exp(-cw[p]) for p in ps}
        e_end = {p: jnp.exp(tot[p] - cw[p]) for p in ps}
        lhs = {p: jnp.concatenate([av_ref[0, rows, sl[p]] * jnp.exp(cw[p] - lw[p]), r[p] * jnp.exp(cw[p])],
                                  axis=0) for p in ps}
        rhs_g = {p: jnp.concatenate([by_head(bv[p] * e_neg[p]), by_head(k[p] * e_neg[p])], axis=0) for p in ps}
        rhs_s = {p: jnp.concatenate([bv[p] * e_end[p], k[p] * e_end[p]], axis=0) for p in ps}
        vh = {p: by_head(v[p]) for p in ps}
        g = {p: _dot_nt(lhs[p], rhs_g[p]) for p in ps}
        yield
        avs = {p: _dot(jnp.where(strict2, g[p][:chunk, c2:], 0.0), vh[p]) for p in ps}
        m = {p: jnp.where(strict2, g[p][:chunk, :c2], 0.0) for p in ps}
        t2 = {p: eye2 + m[p] for p in ps}
        m = {p: _dot(m[p], blockdiag(m[p])) for p in ps}
        for _ in range(chunk.bit_length() - 3):
            yield
            res = {p: _dot(jnp.concatenate([m[p], t2[p]], axis=0), blockdiag(m[p])) for p in ps}
            m = {p: res[p][:chunk] for p in ps}
            t2 = {p: t2[p] + res[p][chunk:] for p in ps}
        yield
        t2 = {p: t2[p] + _dot(t2[p], blockdiag(m[p])) for p in ps}
        assert all(applied[p] == j for p in ps)
        s = {p: state[p] for p in ps}
        sr = {p: _dot_nt(lhs[p], s[p]) for p in ps}
        yield
        u = {p: _dot(t2[p], by_head(sr[p][:chunk] + avs[p])) for p in ps}
        yield
        yr = {p: _dot(jnp.where(incl4, g[p][chunk:], 0.0), jnp.concatenate([by_head(u[p]), vh[p]], axis=0))
              for p in ps}
        for p in ps:
            state[p] = jnp.where(blockmask, s[p] * jnp.exp(tot[p])
                                 + _dot_tn(jnp.concatenate([u[p], v[p]], axis=0), rhs_s[p]), 0.0)
            applied[p] = j + 1
        yield
        for p in ps:
            y = sr[p][chunk:] + yr[p]
            mean = _pair_sums(y, m0) * (1.0 / hn)
            d = y - mean
            var = _pair_sums(d * d, m0) * (1.0 / hn)
            yn = d * lax.rsqrt(var + RW_GN_EPS) * lnw_ref[:, sl[p]] + lnb_ref[:, sl[p]]
            bonus = _pair_sums(r[p] * k[p] * rk_ref[:, sl[p]], m0) * v[p]
            y_ref[0, rows, sl[p]] = ((yn + bonus) * gate_ref[0, rows, sl[p]]).astype(y_ref.dtype)

    gsz = _pick(pb, (4, 2, 1))
    _run_staggered([(j * REC_CHUNK_TICKS, program(range(g0, g0 + gsz), j))
                    for j in range(nch) for g0 in range(0, pb, gsz)])
    for p in range(pb):
        s_scr[p] = state[p]

    @pl.when(c == nc - 1)
    def _():
        sout_ref[0] = s_scr[...]


def _rw_recurrence(r, k, v, lw, av, bv, gate, r_k, ln_w, ln_b, s0p, *, pb, chunk):
    b, t, rw = r.shape
    pairs = rw // LANE
    groups = pairs // pb
    wb = pb * LANE
    nch = _pick(t // chunk, (REC_CHUNKS, 2, 1))
    kern = functools.partial(_rw_kernel, pb=pb, chunk=chunk, nch=nch)
    tile = pl.BlockSpec((1, nch * chunk, wb), lambda bb, g, c: (bb, c, g))
    par = pl.BlockSpec((1, wb), lambda bb, g, c: (0, g))
    st = pl.BlockSpec((1, pb, LANE, LANE), lambda bb, g, c: (bb, g, 0, 0))
    return pl.pallas_call(
        kern,
        grid=(b, groups, t // (nch * chunk)),
        in_specs=[tile] * 7 + [par] * 3 + [st],
        out_specs=[tile, st],
        out_shape=[jax.ShapeDtypeStruct((b, t, rw), BF16),
                   jax.ShapeDtypeStruct((b, pairs, LANE, LANE), F32)],
        scratch_shapes=[pltpu.VMEM((pb, LANE, LANE), F32)],
        compiler_params=_cparams(("parallel", "parallel", "arbitrary")),
        name="rw_recurrence",
    )(r, k, v, lw, av, bv, gate, r_k, ln_w, ln_b, s0p)


def _prepare(w):
    heads = w["dn_a_log"].shape[-1]
    dk = w["dn_norm_w"].shape[-1]
    qkv_w = w["dn_conv_w"].shape[-1]
    v_w = heads * dk
    assert qkv_w == 3 * v_w and dk == LANE
    rw_heads, rw_head = w["rw_r_k"].shape
    assert rw_head == LANE // 2
    rw = rw_heads * rw_head
    lw_n, la_n, lg_n = w["rw_w2"].shape[0], w["rw_a2"].shape[0], w["rw_g2"].shape[0]
    o1 = qkv_w
    o2 = o1 + v_w
    o4 = o2 + 2 * heads
    shift_w = 3 * rw + lw_n + la_n + lg_n
    assert o2 % LANE == 0 and 2 * heads <= LANE and w["w_in"].shape[1] == o4 + shift_w
    win0 = o4 // LANE * LANE
    win_off = o4 - win0
    win_w = _round_up(win_off + shift_w, LANE)
    win_bw = math.gcd(math.gcd(win0, win_w), 8 * LANE)
    assert win0 + win_w <= _round_up(o4 + shift_w, IN_TN)

    def lora_block(start, n, weight):
        lo, hi = start // LANE * LANE, _round_up(start + n, LANE)
        padded = jnp.pad(weight, ((start - lo, hi - start - n), (0, 0))).astype(BF16)
        return (lo, hi), padded

    xw_sl, w2p = lora_block(3 * rw, lw_n, w["rw_w2"])
    xa_sl, a2p = lora_block(3 * rw + lw_n, la_n, w["rw_a2"])
    xg_sl, g2p = lora_block(3 * rw + lw_n + la_n, lg_n, w["rw_g2"])
    lane_pad = lambda a: jnp.pad(a, (heads, LANE - 2 * heads)).reshape(1, LANE)
    in_window = lambda a: jnp.pad(a, [(0, 0)] * (a.ndim - 1) + [(win_off, win_w - win_off - shift_w)])
    return dict(
        heads=heads, dk=dk, rw=rw, rw_heads=rw_heads, o1=o1, o2=o2, o4=o4, shift_w=shift_w,
        win0=win0, win_off=win_off, win_w=win_w, win_bw=win_bw, in_window=in_window,
        xw_sl=xw_sl, xa_sl=xa_sl, xg_sl=xg_sl, w2p=w2p, a2p=a2p, g2p=g2p,
        alog_row=lane_pad(w["dn_a_log"]), dtb_row=lane_pad(w["dn_dt_bias"]),
        mu_win=in_window(w["rw_mu"].reshape(1, shift_w)),
        w0=w["rw_w0"].reshape(1, rw), a0=w["rw_a0"].reshape(1, rw),
        k_k=w["rw_k_k"].reshape(1, rw), k_a=w["rw_k_a"].reshape(1, rw),
        w_in_t=jnp.swapaxes(w["w_in"], 0, 1),
        w_down=w["w_down"].astype(BF16),
    )


def _layer(x, dn_state, dn_conv, rw_state, rw_shift, w, pp):
    b, t, d = x.shape
    m = b * t
    heads, dk, rw, rw_heads = pp["heads"], pp["dk"], pp["rw"], pp["rw_heads"]
    o1, o2, o4, shift_w = pp["o1"], pp["o2"], pp["o4"], pp["shift_w"]
    chunk = 64 if t % 64 == 0 else t
    assert chunk & (chunk - 1) == 0 and chunk >= 2 * SUBLANE
    hb = 16 if heads % 16 == 0 else heads
    pairs = rw // LANE
    pb = 16 if pairs % 16 == 0 else pairs
    tm = _pick(m, (1024, 512, 256, 128))

    xf = x.reshape(m, d)
    h = _rms_bf16(xf, w["g_mix_pre"])
    cache8 = jnp.pad(dn_conv.astype(F32), ((0, 0), (SUBLANE - dn_conv.shape[1], 0), (0, 0)))
    tm_in = min(tm, IN_TM_MAX)
    if t % tm_in == 0 and o1 % (3 * IN_TN) == 0:
        p, qkv = _in_proj_dn(h, pp["w_in_t"], cache8, w["dn_conv_w"], t=t, tm=tm_in, tn=IN_TN, o1=o1, dk=dk)
        p = p.reshape(b, t, p.shape[1])
        qkv = qkv.reshape(b, t, o1)
    else:
        p = _matmul_w32t(h, pp["w_in_t"], tm=tm, tn=IN_TN, name="in_proj")
        p = p.reshape(b, t, p.shape[1])
        qkv = _dn_pre(p, cache8, w["dn_conv_w"], heads=heads, dk=dk)

    gb = _dn_gates(p, pp["alog_row"], pp["dtb_row"], heads=heads, hb=hb, c_ba=o2)
    y_a, new_dn_state = _dn_recurrence(qkv, gb, p, w["dn_norm_w"], dn_state.astype(F32),
                                       heads=heads, hb=hb, dk=dk, chunk=chunk, c_z=o1)
    new_dn_conv = p[:, t - dn_conv.shape[1]:, :o1]

    shift8 = jnp.pad(pp["in_window"](rw_shift.astype(F32)), ((0, 0), (SUBLANE - 1, 0), (0, 0)))
    r, k, v, lw, av, bv, gate = _rw_pre(p, shift8, pp)
    hn = LANE // 2
    s4 = rw_state.astype(F32).reshape(b, pairs, 2, hn, hn)
    zeros = jnp.zeros_like(s4[:, :, 0])
    s0p = jnp.concatenate([jnp.concatenate([s4[:, :, 0], zeros], axis=-1),
                           jnp.concatenate([zeros, s4[:, :, 1]], axis=-1)], axis=-2)
    y_b, sp = _rw_recurrence(r, k, v, lw, av, bv, gate, w["rw_r_k"].reshape(1, rw),
                             w["rw_ln_w"].reshape(1, rw), w["rw_ln_b"].reshape(1, rw), s0p,
                             pb=pb, chunk=chunk)
    new_rw_state = jnp.stack([sp[:, :, :hn, :hn], sp[:, :, hn:, hn:]], axis=2).reshape(b, rw_heads, hn, hn)
    new_rw_shift = p[:, t - 1:, o4:o4 + shift_w]

    mixo = _matmul2(y_a.reshape(m, heads * dk), y_b.reshape(m, rw), w["w_out"],
                    tm=tm, tn=_pick(d, (512, 256, 128)), name="out_proj")
    x1, h2 = _resid_rms(xf, mixo, w["g_mix_post"], w["g_ffn_pre"])
    dff = w["w_gate"].shape[1]
    f = _swiglu_up(h2, w["w_gate"], w["w_up"], tm=tm, tn=_pick(dff, (256, 128)))
    fo = _matmul(f, pp["w_down"], tm=_pick(m, (512, 256, 128)), tn=_pick(d, (256, 128)),
                 out_dtype=BF16, name="ffn_down")
    out = _resid_final(x1, fo, w["g_ffn_post"]).reshape(b, t, d)
    return out, (new_dn_state, new_dn_conv, new_rw_state, new_rw_shift)


_WEIGHT_NAMES = ("g_mix_pre", "g_mix_post", "w_in", "dn_conv_w", "dn_a_log", "dn_dt_bias", "dn_norm_w",
                 "rw_mu", "rw_w0", "rw_w2", "rw_a0", "rw_a2", "rw_g2", "rw_k_k", "rw_k_a", "rw_r_k",
                 "rw_ln_w", "rw_ln_b", "w_out", "g_ffn_pre", "g_ffn_post", "w_gate", "w_up", "w_down")


def kernel(x_prompt, x_sample, state_dn, cache_dn_conv, state_rwkv, cache_rwkv_shift,
           g_mix_pre, g_mix_post, w_in, dn_conv_w, dn_a_log, dn_dt_bias, dn_norm_w,
           rw_mu, rw_w0, rw_w2, rw_a0, rw_a2, rw_g2, rw_k_k, rw_k_a, rw_r_k, rw_ln_w, rw_ln_b,
           w_out, g_ffn_pre, g_ffn_post, w_gate, w_up, w_down):
    stacked = (g_mix_pre, g_mix_post, w_in, dn_conv_w, dn_a_log, dn_dt_bias, dn_norm_w,
               rw_mu, rw_w0, rw_w2, rw_a0, rw_a2, rw_g2, rw_k_k, rw_k_a, rw_r_k, rw_ln_w, rw_ln_b,
               w_out, g_ffn_pre, g_ffn_post, w_gate, w_up, w_down)
    depth = w_in.shape[0]
    bp = x_prompt.shape[0]
    dt = x_prompt.dtype
    yp, ys = x_prompt, x_sample
    outs_p, outs_s = [], []
    for l in range(depth):
        w = {n: a[l] for n, a in zip(_WEIGHT_NAMES, stacked)}
        pp = _prepare(w)
        heads, dk, rw_heads = pp["heads"], pp["dk"], pp["rw_heads"]
        hn = LANE // 2
        yp, st_p = _layer(yp,
                          jnp.zeros((bp, heads, dk, dk), dt),
                          jnp.zeros((bp, cache_dn_conv.shape[2], cache_dn_conv.shape[3]), dt),
                          jnp.zeros((bp, rw_heads, hn, hn), dt),
                          jnp.zeros((bp, 1, cache_rwkv_shift.shape[3]), dt), w, pp)
        ys, st_s = _layer(ys, state_dn[l], cache_dn_conv[l], state_rwkv[l], cache_rwkv_shift[l], w, pp)
        outs_p.append(st_p)
        outs_s.append(st_s)
    stack = lambda outs, i: jnp.stack([o[i] for o in outs])
    return (yp, ys,
            stack(outs_p, 0), stack(outs_p, 1), stack(outs_p, 2), stack(outs_p, 3),
            stack(outs_s, 0), stack(outs_s, 1), stack(outs_s, 2), stack(outs_s, 3))
```

```python
import functools
import math

import jax
import jax.numpy as jnp
from jax import lax
from jax.experimental import pallas as pl
from jax.experimental.pallas import tpu as pltpu

F32 = jnp.float32
BF16 = jnp.bfloat16

LANE = 128
SUBLANE = 8
VMEM_LIMIT_BYTES = 56 * 2**20
NORM_EPS = 1e-6
L2_EPS = 1e-6
RW_GN_EPS = 64e-5
IN_TN = 512
IN_TM_MAX = 1024
REC_CHUNKS = 4
REC_CHUNK_TICKS = 3


def _cparams(sem):
    return pltpu.CompilerParams(dimension_semantics=sem, vmem_limit_bytes=VMEM_LIMIT_BYTES)


def _round_up(x, m):
    return -(-x // m) * m


def _pick(n, cands):
    for c in cands:
        if n % c == 0:
            return c
    return n


def _dot(a, b):
    return jnp.dot(a.astype(BF16), b.astype(BF16), preferred_element_type=F32)


def _dot_nt(a, b):
    return lax.dot_general(a.astype(BF16), b.astype(BF16), (((1,), (1,)), ((), ())),
                           preferred_element_type=F32)


def _dot_tn(a, b):
    return lax.dot_general(a.astype(BF16), b.astype(BF16), (((0,), (0,)), ((), ())),
                           preferred_element_type=F32)


def _cumsum_rows(x):
    c = x.shape[0]
    tri = (_iota2((c, c), 0) >= _iota2((c, c), 1)).astype(BF16)
    hi = x.astype(BF16)
    r1 = x - hi.astype(F32)
    mid = r1.astype(BF16)
    lo = (r1 - mid.astype(F32)).astype(BF16)
    dot = lambda p: jnp.dot(tri, p, preferred_element_type=F32)
    return dot(hi) + (dot(mid) + dot(lo))


def _sigmoid(x):
    return 0.5 * jnp.tanh(0.5 * x) + 0.5


def _softplus(x):
    return jnp.maximum(x, 0.0) + jnp.log(1.0 + jnp.exp(-jnp.abs(x)))


def _iota2(shape, dim):
    return lax.broadcasted_iota(jnp.int32, shape, dim)


def _run_staggered(programs):
    live = list(programs)
    tick = 0
    while live:
        still = []
        for start, prog in live:
            if tick >= start:
                try:
                    next(prog)
                except StopIteration:
                    continue
            still.append((start, prog))
        live = still
        tick += 1


def _rms(x, g):
    return x * lax.rsqrt(jnp.mean(x * x, axis=-1, keepdims=True) + NORM_EPS) * g


def _rms_kernel(x_ref, g_ref, o_ref):
    o_ref[...] = _rms(x_ref[...], g_ref[...]).astype(o_ref.dtype)


def _rms_bf16(x, g):
    m, d = x.shape
    tm = _pick(m, (256, 128))
    return pl.pallas_call(
        _rms_kernel,
        grid=(m // tm,),
        in_specs=[pl.BlockSpec((tm, d), lambda i: (i, 0)), pl.BlockSpec((1, d), lambda i: (0, 0))],
        out_specs=pl.BlockSpec((tm, d), lambda i: (i, 0)),
        out_shape=jax.ShapeDtypeStruct((m, d), BF16),
        compiler_params=_cparams(("parallel",)),
        name="rms_pre",
    )(x, g.reshape(1, d))


def _resid_rms_kernel(x_ref, y_ref, gp_ref, gn_ref, x1_ref, h_ref):
    x1 = x_ref[...] + _rms(y_ref[...].astype(F32), gp_ref[...])
    x1_ref[...] = x1
    h_ref[...] = _rms(x1, gn_ref[...]).astype(h_ref.dtype)


def _resid_rms(x, y, g_post, g_next):
    m, d = x.shape
    tm = _pick(m, (256, 128))
    row = pl.BlockSpec((tm, d), lambda i: (i, 0))
    par = pl.BlockSpec((1, d), lambda i: (0, 0))
    return pl.pallas_call(
        _resid_rms_kernel,
        grid=(m // tm,),
        in_specs=[row, row, par, par],
        out_specs=[row, row],
        out_shape=[jax.ShapeDtypeStruct((m, d), F32), jax.ShapeDtypeStruct((m, d), BF16)],
        compiler_params=_cparams(("parallel",)),
        name="resid_rms",
    )(x, y, g_post.reshape(1, d), g_next.reshape(1, d))


def _resid_final_kernel(x_ref, y_ref, g_ref, o_ref):
    o_ref[...] = x_ref[...] + _rms(y_ref[...].astype(F32), g_ref[...])


def _resid_final(x, y, g):
    m, d = x.shape
    tm = _pick(m, (256, 128))
    row = pl.BlockSpec((tm, d), lambda i: (i, 0))
    return pl.pallas_call(
        _resid_final_kernel,
        grid=(m // tm,),
        in_specs=[row, row, pl.BlockSpec((1, d), lambda i: (0, 0))],
        out_specs=row,
        out_shape=jax.ShapeDtypeStruct((m, d), F32),
        compiler_params=_cparams(("parallel",)),
        name="resid_final",
    )(x, y, g.reshape(1, d))


def _mm_kernel(x_ref, w_ref, o_ref):
    o_ref[...] = jnp.dot(x_ref[...], w_ref[...], preferred_element_type=F32).astype(o_ref.dtype)


def _matmul(x, w, *, tm, tn, out_dtype, name):
    m, k = x.shape
    n = w.shape[1]
    return pl.pallas_call(
        _mm_kernel,
        grid=(m // tm, n // tn),
        in_specs=[pl.BlockSpec((tm, k), lambda i, j: (i, 0)), pl.BlockSpec((k, tn), lambda i, j: (0, j))],
        out_specs=pl.BlockSpec((tm, tn), lambda i, j: (i, j)),
        out_shape=jax.ShapeDtypeStruct((m, n), out_dtype),
        compiler_params=_cparams(("parallel", "arbitrary")),
        name=name,
    )(x, w)


def _mm_w32t_kernel(x_ref, wt_ref, o_ref, *, n_valid):
    wt = wt_ref[...]
    tn = wt.shape[0]
    if n_valid % tn:
        wt = jnp.where(pl.program_id(1) * tn + _iota2(wt.shape, 0) < n_valid, wt, 0.0)
    o_ref[...] = lax.dot_general(x_ref[...], wt.astype(BF16), (((1,), (1,)), ((), ())),
                                 preferred_element_type=F32)


def _matmul_w32t(x, wt, *, tm, tn, name):
    m, k = x.shape
    n = wt.shape[0]
    nt = pl.cdiv(n, tn)
    return pl.pallas_call(
        functools.partial(_mm_w32t_kernel, n_valid=n),
        grid=(m // tm, nt),
        in_specs=[pl.BlockSpec((tm, k), lambda i, j: (i, 0)), pl.BlockSpec((tn, k), lambda i, j: (j, 0))],
        out_specs=pl.BlockSpec((tm, tn), lambda i, j: (i, j)),
        out_shape=jax.ShapeDtypeStruct((m, nt * tn), F32),
        compiler_params=_cparams(("parallel", "arbitrary")),
        name=name,
    )(x, wt)


def _mm2_kernel(xa_ref, xb_ref, wa_ref, wb_ref, o_ref):
    acc = jnp.dot(xa_ref[...], wa_ref[...].astype(BF16), preferred_element_type=F32)
    acc = acc + jnp.dot(xb_ref[...], wb_ref[...].astype(BF16), preferred_element_type=F32)
    o_ref[...] = acc.astype(o_ref.dtype)


def _matmul2(xa, xb, w, *, tm, tn, name):
    m, ka = xa.shape
    kb = xb.shape[1]
    n = w.shape[1]
    assert ka == kb and w.shape[0] == ka + kb
    return pl.pallas_call(
        _mm2_kernel,
        grid=(m // tm, n // tn),
        in_specs=[pl.BlockSpec((tm, ka), lambda i, j: (i, 0)), pl.BlockSpec((tm, kb), lambda i, j: (i, 0)),
                  pl.BlockSpec((ka, tn), lambda i, j: (0, j)), pl.BlockSpec((kb, tn), lambda i, j: (1, j))],
        out_specs=pl.BlockSpec((tm, tn), lambda i, j: (i, j)),
        out_shape=jax.ShapeDtypeStruct((m, n), BF16),
        compiler_params=_cparams(("parallel", "arbitrary")),
        name=name,
    )(xa, xb, w, w)


def _swiglu_kernel(x_ref, wg_ref, wu_ref, o_ref):
    x = x_ref[...]
    g = jnp.dot(x, wg_ref[...].astype(BF16), preferred_element_type=F32)
    u = jnp.dot(x, wu_ref[...].astype(BF16), preferred_element_type=F32)
    o_ref[...] = (g * _sigmoid(g) * u).astype(o_ref.dtype)


def _swiglu_up(x, wg, wu, *, tm, tn):
    m, k = x.shape
    n = wg.shape[1]
    wspec = pl.BlockSpec((k, tn), lambda i, j: (0, j))
    return pl.pallas_call(
        _swiglu_kernel,
        grid=(m // tm, n // tn),
        in_specs=[pl.BlockSpec((tm, k), lambda i, j: (i, 0)), wspec, wspec],
        out_specs=pl.BlockSpec((tm, tn), lambda i, j: (i, j)),
        out_shape=jax.ShapeDtypeStruct((m, n), BF16),
        compiler_params=_cparams(("parallel", "arbitrary")),
        name="ffn_up",
    )(x, wg, wu)


def _shift_rows(u, halo, s):
    ru = pltpu.roll(u, s, axis=0)
    rh = pltpu.roll(halo, s, axis=0)
    top = jnp.where(_iota2(halo.shape, 0) < s, rh, ru[:SUBLANE])
    if u.shape[0] == SUBLANE:
        return top
    return jnp.concatenate([top, ru[SUBLANE:]], axis=0)


def _conv_silu_strips(load, first_halo, cw_ref, store, *, tm, heads, dk, scale, normalize=None, rows=64):
    rb = min(tm, rows)
    for h in range(heads):
        cols = slice(h * dk, (h + 1) * dk)
        cw = cw_ref[:, cols]
        for r0 in range(0, tm, rb):
            u = load(r0, r0 + rb, cols)
            halo = first_halo(cols) if r0 == 0 else load(r0 - SUBLANE, r0, cols)
            prev = [_shift_rows(u, halo, sh) for sh in (3, 2, 1)]
            conv = prev[0] * cw[0:1]
            conv = conv + prev[1] * cw[1:2]
            conv = conv + prev[2] * cw[2:3]
            conv = conv + u * cw[3:4]
            s = conv * _sigmoid(conv)
            if scale is not None:
                normed = s * (lax.rsqrt(jnp.sum(s * s, axis=-1, keepdims=True) + L2_EPS) * scale)
                s = normed if normalize is None else jnp.where(normalize, normed, s)
            store(r0, r0 + rb, cols, s)


def _dn_pre_kernel(p_ref, halo_ref, cache_ref, cw_ref, qkv_ref, *, heads, dk):
    i = pl.program_id(1)
    j = pl.program_id(2)
    tm = p_ref.shape[1]

    def store(r0, r1, cols, value):
        qkv_ref[0, r0:r1, cols] = value

    def strips(scale):
        _conv_silu_strips(lambda r0, r1, cols: p_ref[0, r0:r1, cols],
                          lambda cols: jnp.where(i == 0, cache_ref[0, :, cols], halo_ref[0, :, cols]),
                          cw_ref, store, tm=tm, heads=heads, dk=dk, scale=scale)

    @pl.when(j < 2)
    def _():
        strips(jnp.where(j == 0, dk ** -0.5, 1.0).astype(F32))

    @pl.when(j == 2)
    def _():
        strips(None)


def _dn_pre(p, cache8, conv_w, *, heads, dk):
    b, t, _ = p.shape
    w = heads * dk
    tm = _pick(t, (256, 128, 64, 32, 16))
    kern = functools.partial(_dn_pre_kernel, heads=heads, dk=dk)
    return pl.pallas_call(
        kern,
        grid=(b, t // tm, 3),
        in_specs=[
            pl.BlockSpec((1, tm, w), lambda bb, i, j: (bb, i, j)),
            pl.BlockSpec((1, SUBLANE, w), lambda bb, i, j: (bb, jnp.maximum(i * (tm // SUBLANE) - 1, 0), j)),
            pl.BlockSpec((1, SUBLANE, w), lambda bb, i, j: (bb, 0, j)),
            pl.BlockSpec((4, w), lambda bb, i, j: (0, j)),
        ],
        out_specs=pl.BlockSpec((1, tm, w), lambda bb, i, j: (bb, i, j)),
        out_shape=jax.ShapeDtypeStruct((b, t, 3 * w), F32),
        compiler_params=_cparams(("parallel", "parallel", "arbitrary")),
        name="dn_pre",
    )(p, p, cache8, conv_w)


def _dn_gates_kernel(ba_ref, alog_ref, dtb_ref, gb_ref, *, heads, hb):
    x = ba_ref[0]
    lane = _iota2(x.shape, 1)
    g = -jnp.exp(alog_ref[...]) * _softplus(x + dtb_ref[...])
    full = jnp.where(lane < heads, _sigmoid(x), g)
    for hg in range(heads // hb):
        gb_ref[0, hg] = full if hg == 0 else pltpu.roll(full, LANE - hg * hb, axis=1)


def _dn_gates(p, alog_row, dtb_row, *, heads, hb, c_ba):
    b, t, _ = p.shape
    tm = _pick(t, (1024, 512, 256, 128, 64, 32, 16))
    groups = heads // hb
    return pl.pallas_call(
        functools.partial(_dn_gates_kernel, heads=heads, hb=hb),
        grid=(b, t // tm),
        in_specs=[pl.BlockSpec((1, tm, LANE), lambda bb, i: (bb, i, c_ba // LANE)),
                  pl.BlockSpec((1, LANE), lambda bb, i: (0, 0)),
                  pl.BlockSpec((1, LANE), lambda bb, i: (0, 0))],
        out_specs=pl.BlockSpec((1, groups, tm, LANE), lambda bb, i: (bb, 0, i, 0)),
        out_shape=jax.ShapeDtypeStruct((b, groups, t, LANE), F32),
        compiler_params=_cparams(("parallel", "parallel")),
        name="dn_gates",
    )(p, alog_row, dtb_row)


def _in_proj_dn_kernel(x_ref, wt_ref, cache_ref, cw_ref, p_ref, qkv_ref, raw_scr, halo_scr,
                       *, n_valid, nq, dk, tiles_per_seq):
    i = pl.program_id(0)
    j = pl.program_id(1)
    tm, tn = p_ref.shape
    jj = jnp.clip(j - 1, 0, nq - 1)
    first = (i % tiles_per_seq) == 0
    active = (j >= 1) & (j <= nq)

    @pl.when((i == 0) & (j == 0))
    def _():
        raw_scr[...] = jnp.zeros_like(raw_scr)
        halo_scr[...] = jnp.zeros_like(halo_scr)

    def store(r0, r1, cols, value):
        qkv_ref[r0:r1, cols] = value

    def matmul():
        wt = wt_ref[...]
        if n_valid % tn:
            wt = jnp.where(j * tn + _iota2(wt.shape, 0) < n_valid, wt, 0.0)
        acc = lax.dot_general(x_ref[...], wt.astype(BF16), (((1,), (1,)), ((), ())),
                              preferred_element_type=F32)
        p_ref[...] = acc
        return acc

    def fused_step(write_slot):
        read_slot = 1 - write_slot
        raw_scr[write_slot] = matmul()
        _conv_silu_strips(lambda r0, r1, cols: raw_scr[read_slot, r0:r1, cols],
                          lambda cols: jnp.where(first, cache_ref[0, :, cols], halo_scr[jj, :, cols]),
                          cw_ref, store, tm=tm, heads=tn // dk, dk=dk,
                          scale=jnp.where(jj < nq // 3, dk ** -0.5, 1.0).astype(F32),
                          normalize=jj < 2 * nq // 3)
        halo_scr[jj] = raw_scr[read_slot, tm - SUBLANE:tm, :]

    for parity in range(2):
        pl.when(active & (j % 2 == parity))(functools.partial(fused_step, parity))

    @pl.when(jnp.logical_not(active))
    def _():
        raw_scr[0] = matmul()


def _in_proj_dn(x, wt, cache8, conv_w, *, t, tm, tn, o1, dk):
    m, k = x.shape
    n = wt.shape[0]
    nt = pl.cdiv(n, tn)
    nq = o1 // tn
    tiles_per_seq = t // tm
    assert o1 % (3 * tn) == 0 and t % tm == 0 and tn % dk == 0 and nt > nq
    qcol = lambda i, j: jnp.clip(j - 1, 0, nq - 1)
    kern = functools.partial(_in_proj_dn_kernel, n_valid=n, nq=nq, dk=dk, tiles_per_seq=tiles_per_seq)
    return pl.pallas_call(
        kern,
        grid=(m // tm, nt),
        in_specs=[pl.BlockSpec((tm, k), lambda i, j: (i, 0)),
                  pl.BlockSpec((tn, k), lambda i, j: (j, 0)),
                  pl.BlockSpec((1, SUBLANE, tn), lambda i, j: (i // tiles_per_seq, 0, qcol(i, j))),
                  pl.BlockSpec((4, tn), lambda i, j: (0, qcol(i, j)))],
        out_specs=[pl.BlockSpec((tm, tn), lambda i, j: (i, j)),
                   pl.BlockSpec((tm, tn), lambda i, j: (i, qcol(i, j)))],
        out_shape=[jax.ShapeDtypeStruct((m, nt * tn), F32), jax.ShapeDtypeStruct((m, o1), F32)],
        scratch_shapes=[pltpu.VMEM((2, tm, tn), F32), pltpu.VMEM((nq, SUBLANE, tn), F32)],
        compiler_params=_cparams(("arbitrary", "arbitrary")),
        name="in_proj",
    )(x, wt, cache8, conv_w)


def _dn_kernel(q_ref, k_ref, v_ref, gb_ref, z_ref, nw_ref, s0_ref, y_ref, sout_ref, s_scr,
               *, heads, hb, dk, chunk, nch):
    c = pl.program_id(2)
    nc = pl.num_programs(2)

    @pl.when(c == 0)
    def _():
        s_scr[...] = s0_ref[0]

    row = _iota2((chunk, chunk), 0)
    col = _iota2((chunk, chunk), 1)
    causal = row >= col
    strict = row > col
    nw = nw_ref[...]
    eye = (row == col).astype(F32)
    state = {h: s_scr[h] for h in range(hb)}
    applied = {h: 0 for h in range(hb)}
    per_chunk = {}

    def chunk_gates(j):
        if j not in per_chunk:
            gbt = gb_ref[0, 0, j * chunk:(j + 1) * chunk, :]
            gc_all = _cumsum_rows(gbt)
            per_chunk[j] = (gbt, gc_all, jnp.transpose(gc_all))
        return per_chunk[j]

    def program(hs, j):
        rows = slice(j * chunk, (j + 1) * chunk)
        gbt, gc_all, gc_t = chunk_gates(j)
        sl = {h: slice(h * dk, (h + 1) * dk) for h in hs}
        q = {h: q_ref[0, rows, sl[h]] for h in hs}
        k = {h: k_ref[0, rows, sl[h]] for h in hs}
        v = {h: v_ref[0, rows, sl[h]] for h in hs}
        beta = {h: gbt[:, h:h + 1] for h in hs}
        gcol = {h: gc_all[:, heads + h:heads + h + 1] for h in hs}
        glast = {h: gc_all[chunk - 1:chunk, heads + h:heads + h + 1] for h in hs}
        decay = {h: jnp.where(causal, jnp.exp(gcol[h] - gc_t[heads + h:heads + h + 1, :]), 0.0) for h in hs}
        kb = {h: k[h] * beta[h] for h in hs}
        eg = {h: jnp.exp(gcol[h]) for h in hs}
        yield
        m = {h: jnp.where(strict, -_dot_nt(kb[h], k[h]) * decay[h], 0.0) for h in hs}
        qk = {h: jnp.where(causal, _dot_nt(q[h], k[h]) * decay[h], 0.0) for h in hs}
        yield
        t = {h: eye + m[h] for h in hs}
        m = {h: _dot(m[h], m[h]) for h in hs}
        for _ in range(chunk.bit_length() - 3):
            yield
            res = {h: _dot(jnp.concatenate([m[h], t[h]], axis=0), m[h]) for h in hs}
            m = {h: res[h][:chunk] for h in hs}
            t = {h: t[h] + res[h][chunk:] for h in hs}
        yield
        t = {h: t[h] + _dot(t[h], m[h]) for h in hs}
        yield
        sol = {h: _dot(t[h], jnp.concatenate([v[h] * beta[h], kb[h] * eg[h]], axis=1)) for h in hs}
        yield
        assert all(applied[h] == j for h in hs)
        s = {h: state[h] for h in hs}
        v_new = {h: sol[h][:, :dk] - _dot(sol[h][:, dk:], s[h]) for h in hs}
        os = {h: _dot(q[h] * eg[h], s[h]) for h in hs}
        yield
        o = {h: os[h] + _dot(qk[h], v_new[h]) for h in hs}
        for h in hs:
            state[h] = s[h] * jnp.exp(glast[h]) + _dot_tn(k[h] * jnp.exp(glast[h] - gcol[h]), v_new[h])
            applied[h] = j + 1
        yield
        for h in hs:
            z = z_ref[0, rows, sl[h]]
            y_ref[0, rows, sl[h]] = (_rms(o[h], nw) * (z * _sigmoid(z))).astype(y_ref.dtype)

    gsz = _pick(hb, (4, 2, 1))
    _run_staggered([(j * REC_CHUNK_TICKS, program(range(g0, g0 + gsz), j))
                    for j in range(nch) for g0 in range(0, hb, gsz)])
    for h in range(hb):
        s_scr[h] = state[h]

    @pl.when(c == nc - 1)
    def _():
        sout_ref[0] = s_scr[...]


def _dn_recurrence(qkv, gb, p, norm_w, s0, *, heads, hb, dk, chunk, c_z):
    b, t, _ = qkv.shape
    groups = heads // hb
    wb = hb * dk
    nqk = heads * dk // wb
    nch = _pick(t // chunk, (REC_CHUNKS, 2, 1))
    rows = nch * chunk
    kern = functools.partial(_dn_kernel, heads=heads, hb=hb, dk=dk, chunk=chunk, nch=nch)
    return pl.pallas_call(
        kern,
        grid=(b, groups, t // rows),
        in_specs=[
            pl.BlockSpec((1, rows, wb), lambda bb, g, c: (bb, c, g)),
            pl.BlockSpec((1, rows, wb), lambda bb, g, c: (bb, c, nqk + g)),
            pl.BlockSpec((1, rows, wb), lambda bb, g, c: (bb, c, 2 * nqk + g)),
            pl.BlockSpec((1, 1, rows, LANE), lambda bb, g, c: (bb, g, c, 0)),
            pl.BlockSpec((1, rows, wb), lambda bb, g, c: (bb, c, c_z // wb + g)),
            pl.BlockSpec((1, dk), lambda bb, g, c: (0, 0)),
            pl.BlockSpec((1, hb, dk, dk), lambda bb, g, c: (bb, g, 0, 0)),
        ],
        out_specs=[
            pl.BlockSpec((1, rows, wb), lambda bb, g, c: (bb, c, g)),
            pl.BlockSpec((1, hb, dk, dk), lambda bb, g, c: (bb, g, 0, 0)),
        ],
        out_shape=[jax.ShapeDtypeStruct((b, t, heads * dk), BF16),
                   jax.ShapeDtypeStruct((b, heads, dk, dk), F32)],
        scratch_shapes=[pltpu.VMEM((hb, dk, dk), F32)],
        compiler_params=_cparams(("parallel", "parallel", "arbitrary")),
        name="dn_recurrence",
    )(qkv, qkv, qkv, gb, p, norm_w.reshape(1, dk), s0)


def _pair_sums(x, m0):
    s0 = jnp.sum(jnp.where(m0, x, 0.0), axis=-1, keepdims=True)
    s1 = jnp.sum(jnp.where(m0, 0.0, x), axis=-1, keepdims=True)
    return jnp.where(m0, s0, s1)


def _rw_pre_kernel(*refs, nblk, off, rw, xw_sl, xa_sl, xg_sl):
    main, halos = refs[:nblk], refs[nblk:2 * nblk]
    (cache_ref, mu_ref, w0_ref, w2_ref, a0_ref, a2_ref, g2_ref, kk_ref, ka_ref,
     r_ref, k_ref, v_ref, lw_ref, av_ref, bv_ref, gate_ref) = refs[2 * nblk:]
    u = jnp.concatenate([ref[0] for ref in main], axis=1)
    halo = jnp.concatenate([ref[0] for ref in halos], axis=1)
    halo = jnp.where(pl.program_id(1) == 0, cache_ref[0], halo)
    x = u + mu_ref[...] * (_shift_rows(u, halo, 1) - u)
    x = pltpu.roll(x, x.shape[1] - off, axis=1)
    r = x[:, :rw]
    kr = x[:, rw:2 * rw]
    vr = x[:, 2 * rw:3 * rw]
    w_log = -_softplus(-(w0_ref[...] + _dot(jnp.tanh(x[:, xw_sl[0]:xw_sl[1]]), w2_ref[...]))) - 0.5
    a = _sigmoid(a0_ref[...] + _dot(x[:, xa_sl[0]:xa_sl[1]], a2_ref[...]))
    gate_ref[0] = _dot(_sigmoid(x[:, xg_sl[0]:xg_sl[1]]), g2_ref[...])
    r_ref[0] = r
    v_ref[0] = vr
    lw_ref[0] = -jnp.exp(w_log)
    k_ref[0] = kr * (1.0 + (a - 1.0) * ka_ref[...])
    kkr = kr * kk_ref[...]
    m0 = _iota2((1, LANE), 1) < LANE // 2
    for jb in range(rw // LANE):
        sl = slice(jb * LANE, (jb + 1) * LANE)
        blk = kkr[:, sl]
        kk = blk * lax.rsqrt(_pair_sums(blk * blk, m0) + L2_EPS)
        av_ref[0, :, sl] = -kk
        bv_ref[0, :, sl] = kk * a[:, sl]


def _rw_pre(p, cache8, pp):
    b, t, _ = p.shape
    rw, win0, width, bw = pp["rw"], pp["win0"], pp["win_w"], pp["win_bw"]
    nblk = width // bw
    tm = _pick(t, (128, 64, 32, 16))
    kern = functools.partial(_rw_pre_kernel, nblk=nblk, off=pp["win_off"], rw=rw,
                             xw_sl=pp["xw_sl"], xa_sl=pp["xa_sl"], xg_sl=pp["xg_sl"])
    full = lambda shape: pl.BlockSpec(shape, lambda bb, i: (0,) * len(shape))
    row = pl.BlockSpec((1, tm, rw), lambda bb, i: (bb, i, 0))
    cb = [win0 // bw + n for n in range(nblk)]
    main = [pl.BlockSpec((1, tm, bw), lambda bb, i, c=c: (bb, i, c)) for c in cb]
    halos = [pl.BlockSpec((1, SUBLANE, bw),
                          lambda bb, i, c=c: (bb, jnp.maximum(i * (tm // SUBLANE) - 1, 0), c)) for c in cb]
    small = [pp["mu_win"], pp["w0"], pp["w2p"], pp["a0"], pp["a2p"], pp["g2p"], pp["k_k"], pp["k_a"]]
    return pl.pallas_call(
        kern,
        grid=(b, t // tm),
        in_specs=main + halos + [pl.BlockSpec((1, SUBLANE, width), lambda bb, i: (bb, 0, 0))]
        + [full(a.shape) for a in small],
        out_specs=[row] * 7,
        out_shape=[jax.ShapeDtypeStruct((b, t, rw), F32)] * 7,
        compiler_params=_cparams(("parallel", "arbitrary")),
        name="rw_pre",
    )(*([p] * (2 * nblk)), cache8, *small)


def _rw_kernel(r_ref, k_ref, v_ref, lw_ref, av_ref, bv_ref, gate_ref, rk_ref, lnw_ref, lnb_ref, s0_ref,
               y_ref, sout_ref, s_scr, *, pb, chunk, nch):
    c = pl.program_id(2)
    nc = pl.num_programs(2)
    hn = LANE // 2

    @pl.when(c == 0)
    def _():
        s_scr[...] = s0_ref[0]

    c2 = 2 * chunk
    m0 = _iota2((1, LANE), 1) < hn
    m1 = jnp.logical_not(m0)
    blockmask = (_iota2((LANE, LANE), 0) < hn) == (_iota2((LANE, LANE), 1) < hn)
    row2 = _iota2((chunk, c2), 0)
    col2 = _iota2((chunk, c2), 1) & (chunk - 1)
    strict2 = row2 > col2
    eye2 = (row2 == col2).astype(F32)
    incl4 = _iota2((chunk, 2 * c2), 0) >= (_iota2((chunk, 2 * c2), 1) & (chunk - 1))

    def by_head(x):
        return jnp.concatenate([jnp.where(m0, x, 0.0), jnp.where(m1, x, 0.0)], axis=0)

    bd = (_iota2((c2, c2), 0) < chunk) == (_iota2((c2, c2), 1) < chunk)

    def blockdiag(p2):
        return jnp.where(bd, jnp.concatenate([p2, p2], axis=0), 0.0)

    state = {p: s_scr[p] for p in range(pb)}
    applied = {p: 0 for p in range(pb)}

    def program(ps, j):
        rows = slice(j * chunk, (j + 1) * chunk)
        lanes = slice(ps[0] * LANE, (ps[-1] + 1) * LANE)
        sl = {p: slice(p * LANE, (p + 1) * LANE) for p in ps}
        loc = {p: slice((p - ps[0]) * LANE, (p - ps[0] + 1) * LANE) for p in ps}
        r = {p: r_ref[0, rows, sl[p]] for p in ps}
        k = {p: k_ref[0, rows, sl[p]] for p in ps}
        v = {p: v_ref[0, rows, sl[p]] for p in ps}
        lw = {p: lw_ref[0, rows, sl[p]] for p in ps}
        bv = {p: bv_ref[0, rows, sl[p]] for p in ps}
        cw_all = _cumsum_rows(lw_ref[0, rows, lanes])
        yield
        cw = {p: cw_all[:, loc[p]] for p in ps}
        tot = {p: cw[p][chunk - 1:chunk, :] for p in ps}
        e_neg = {p: jnp.exp(-cw[p]) for p in ps}
        e_end = {p: jnp.exp(tot[p] - cw[p]) for p in ps}
        lhs = {p: jnp.concatenate([av_ref[0, rows, sl[p]] * jnp.exp(cw[p] - lw[p]), r[p] * jnp.exp(cw[p])],
                                  axis=0) for p in ps}
        rhs_g = {p: jnp.concatenate([by_head(bv[p] * e_neg[p]), by_head(k[p] * e_neg[p])], axis=0) for p in ps}
        rhs_s = {p: jnp.concatenate([bv[p] * e_end[p], k[p] * e_end[p]], axis=0) for p in ps}
        vh = {p: by_head(v[p]) for p in ps}
        g = {p: _dot_nt(lhs[p], rhs_g[p]) for p in ps}
        yield
        avs = {p: _dot(jnp.where(strict2, g[p][:chunk, c2:], 0.0), vh[p]) for p in ps}
        m = {p: jnp.where(strict2, g[p][:chunk, :c2], 0.0) for p in ps}
        t2 = {p: eye2 + m[p] for p in ps}
        m = {p: _dot(m[p], blockdiag(m[p])) for p in ps}
        for _ in range(chunk.bit_length() - 3):
            yield
            res = {p: _dot(jnp.concatenate([m[p], t2[p]], axis=0), blockdiag(m[p])) for p in ps}
            m = {p: res[p][:chunk] for p in ps}
            t2 = {p: t2[p] + res[p][chunk:] for p in ps}
        yield
        t2 = {p: t2[p] + _dot(t2[p], blockdiag(m[p])) for p in ps}
        assert all(applied[p] == j for p in ps)
        s = {p: state[p] for p in ps}
        sr = {p: _dot_nt(lhs[p], s[p]) for p in ps}
        yield
        u = {p: _dot(t2[p], by_head(sr[p][:chunk] + avs[p])) for p in ps}
        yield
        yr = {p: _dot(jnp.where(incl4, g[p][chunk:], 0.0), jnp.concatenate([by_head(u[p]), vh[p]], axis=0))
              for p in ps}
        for p in ps:
            state[p] = jnp.where(blockmask, s[p] * jnp.exp(tot[p])
                                 + _dot_tn(jnp.concatenate([u[p], v[p]], axis=0), rhs_s[p]), 0.0)
            applied[p] = j + 1
        yield
        for p in ps:
            y = sr[p][chunk:] + yr[p]
            mean = _pair_sums(y, m0) * (1.0 / hn)
            d = y - mean
            var = _pair_sums(d * d, m0) * (1.0 / hn)
            yn = d * lax.rsqrt(var + RW_GN_EPS) * lnw_ref[:, sl[p]] + lnb_ref[:, sl[p]]
            bonus = _pair_sums(r[p] * k[p] * rk_ref[:, sl[p]], m0) * v[p]
            y_ref[0, rows, sl[p]] = ((yn + bonus) * gate_ref[0, rows, sl[p]]).astype(y_ref.dtype)

    gsz = _pick(pb, (4, 2, 1))
    _run_staggered([(j * REC_CHUNK_TICKS, program(range(g0, g0 + gsz), j))
                    for j in range(nch) for g0 in range(0, pb, gsz)])
    for p in range(pb):
        s_scr[p] = state[p]

    @pl.when(c == nc - 1)
    def _():
        sout_ref[0] = s_scr[...]


def _rw_recurrence(r, k, v, lw, av, bv, gate, r_k, ln_w, ln_b, s0p, *, pb, chunk):
    b, t, rw = r.shape
    pairs = rw // LANE
    groups = pairs // pb
    wb = pb * LANE
    nch = _pick(t // chunk, (REC_CHUNKS, 2, 1))
    kern = functools.partial(_rw_kernel, pb=pb, chunk=chunk, nch=nch)
    tile = pl.BlockSpec((1, nch * chunk, wb), lambda bb, g, c: (bb, c, g))
    par = pl.BlockSpec((1, wb), lambda bb, g, c: (0, g))
    st = pl.BlockSpec((1, pb, LANE, LANE), lambda bb, g, c: (bb, g, 0, 0))
    return pl.pallas_call(
        kern,
        grid=(b, groups, t // (nch * chunk)),
        in_specs=[tile] * 7 + [par] * 3 + [st],
        out_specs=[tile, st],
        out_shape=[jax.ShapeDtypeStruct((b, t, rw), BF16),
                   jax.ShapeDtypeStruct((b, pairs, LANE, LANE), F32)],
        scratch_shapes=[pltpu.VMEM((pb, LANE, LANE), F32)],
        compiler_params=_cparams(("parallel", "parallel", "arbitrary")),
        name="rw_recurrence",
    )(r, k, v, lw, av, bv, gate, r_k, ln_w, ln_b, s0p)


def _prepare(w):
    heads = w["dn_a_log"].shape[-1]
    dk = w["dn_norm_w"].shape[-1]
    qkv_w = w["dn_conv_w"].shape[-1]
    v_w = heads * dk
    assert qkv_w == 3 * v_w and dk == LANE
    rw_heads, rw_head = w["rw_r_k"].shape
    assert rw_head == LANE // 2
    rw = rw_heads * rw_head
    lw_n, la_n, lg_n = w["rw_w2"].shape[0], w["rw_a2"].shape[0], w["rw_g2"].shape[0]
    o1 = qkv_w
    o2 = o1 + v_w
    o4 = o2 + 2 * heads
    shift_w = 3 * rw + lw_n + la_n + lg_n
    assert o2 % LANE == 0 and 2 * heads <= LANE and w["w_in"].shape[1] == o4 + shift_w
    win0 = o4 // LANE * LANE
    win_off = o4 - win0
    win_w = _round_up(win_off + shift_w, LANE)
    win_bw = math.gcd(math.gcd(win0, win_w), 8 * LANE)
    assert win0 + win_w <= _round_up(o4 + shift_w, IN_TN)

    def lora_block(start, n, weight):
        lo, hi = start // LANE * LANE, _round_up(start + n, LANE)
        padded = jnp.pad(weight, ((start - lo, hi - start - n), (0, 0))).astype(BF16)
        return (lo, hi), padded

    xw_sl, w2p = lora_block(3 * rw, lw_n, w["rw_w2"])
    xa_sl, a2p = lora_block(3 * rw + lw_n, la_n, w["rw_a2"])
    xg_sl, g2p = lora_block(3 * rw + lw_n + la_n, lg_n, w["rw_g2"])
    lane_pad = lambda a: jnp.pad(a, (heads, LANE - 2 * heads)).reshape(1, LANE)
    in_window = lambda a: jnp.pad(a, [(0, 0)] * (a.ndim - 1) + [(win_off, win_w - win_off - shift_w)])
    return dict(
        heads=heads, dk=dk, rw=rw, rw_heads=rw_heads, o1=o1, o2=o2, o4=o4, shift_w=shift_w,
        win0=win0, win_off=win_off, win_w=win_w, win_bw=win_bw, in_window=in_window,
        xw_sl=xw_sl, xa_sl=xa_sl, xg_sl=xg_sl, w2p=w2p, a2p=a2p, g2p=g2p,
        alog_row=lane_pad(w["dn_a_log"]), dtb_row=lane_pad(w["dn_dt_bias"]),
        mu_win=in_window(w["rw_mu"].reshape(1, shift_w)),
        w0=w["rw_w0"].reshape(1, rw), a0=w["rw_a0"].reshape(1, rw),
        k_k=w["rw_k_k"].reshape(1, rw), k_a=w["rw_k_a"].reshape(1, rw),
        w_in_t=jnp.swapaxes(w["w_in"], 0, 1),
        w_down=w["w_down"].astype(BF16),
    )


def _layer(x, dn_state, dn_conv, rw_state, rw_shift, w, pp):
    b, t, d = x.shape
    m = b * t
    heads, dk, rw, rw_heads = pp["heads"], pp["dk"], pp["rw"], pp["rw_heads"]
    o1, o2, o4, shift_w = pp["o1"], pp["o2"], pp["o4"], pp["shift_w"]
    chunk = 64 if t % 64 == 0 else t
    assert chunk & (chunk - 1) == 0 and chunk >= 2 * SUBLANE
    hb = 16 if heads % 16 == 0 else heads
    pairs = rw // LANE
    pb = 16 if pairs % 16 == 0 else pairs
    tm = _pick(m, (1024, 512, 256, 128))

    xf = x.reshape(m, d)
    h = _rms_bf16(xf, w["g_mix_pre"])
    cache8 = jnp.pad(dn_conv.astype(F32), ((0, 0), (SUBLANE - dn_conv.shape[1], 0), (0, 0)))
    tm_in = min(tm, IN_TM_MAX)
    if t % tm_in == 0 and o1 % (3 * IN_TN) == 0:
        p, qkv = _in_proj_dn(h, pp["w_in_t"], cache8, w["dn_conv_w"], t=t, tm=tm_in, tn=IN_TN, o1=o1, dk=dk)
        p = p.reshape(b, t, p.shape[1])
        qkv = qkv.reshape(b, t, o1)
    else:
        p = _matmul_w32t(h, pp["w_in_t"], tm=tm, tn=IN_TN, name="in_proj")
        p = p.reshape(b, t, p.shape[1])
        qkv = _dn_pre(p, cache8, w["dn_conv_w"], heads=heads, dk=dk)

    gb = _dn_gates(p, pp["alog_row"], pp["dtb_row"], heads=heads, hb=hb, c_ba=o2)
    y_a, new_dn_state = _dn_recurrence(qkv, gb, p, w["dn_norm_w"], dn_state.astype(F32),
                                       heads=heads, hb=hb, dk=dk, chunk=chunk, c_z=o1)
    new_dn_conv = p[:, t - dn_conv.shape[1]:, :o1]

    shift8 = jnp.pad(pp["in_window"](rw_shift.astype(F32)), ((0, 0), (SUBLANE - 1, 0), (0, 0)))
    r, k, v, lw, av, bv, gate = _rw_pre(p, shift8, pp)
    hn = LANE // 2
    s4 = rw_state.astype(F32).reshape(b, pairs, 2, hn, hn)
    zeros = jnp.zeros_like(s4[:, :, 0])
    s0p = jnp.concatenate([jnp.concatenate([s4[:, :, 0], zeros], axis=-1),
                           jnp.concatenate([zeros, s4[:, :, 1]], axis=-1)], axis=-2)
    y_b, sp = _rw_recurrence(r, k, v, lw, av, bv, gate, w["rw_r_k"].reshape(1, rw),
                             w["rw_ln_w"].reshape(1, rw), w["rw_ln_b"].reshape(1, rw), s0p,
                             pb=pb, chunk=chunk)
    new_rw_state = jnp.stack([sp[:, :, :hn, :hn], sp[:, :, hn:, hn:]], axis=2).reshape(b, rw_heads, hn, hn)
    new_rw_shift = p[:, t - 1:, o4:o4 + shift_w]

    mixo = _matmul2(y_a.reshape(m, heads * dk), y_b.reshape(m, rw), w["w_out"],
                    tm=tm, tn=_pick(d, (512, 256, 128)), name="out_proj")
    x1, h2 = _resid_rms(xf, mixo, w["g_mix_post"], w["g_ffn_pre"])
    dff = w["w_gate"].shape[1]
    f = _swiglu_up(h2, w["w_gate"], w["w_up"], tm=_pick(m, (2048, 1024, 512, 256, 128)),
                   tn=_pick(dff, (256, 128)))
    fo = _matmul(f, pp["w_down"], tm=_pick(m, (512, 256, 128)), tn=_pick(d, (512, 256, 128)),
                 out_dtype=BF16, name="ffn_down")
    out = _resid_final(x1, fo, w["g_ffn_post"]).reshape(b, t, d)
    return out, (new_dn_state, new_dn_conv, new_rw_state, new_rw_shift)


_WEIGHT_NAMES = ("g_mix_pre", "g_mix_post", "w_in", "dn_conv_w", "dn_a_log", "dn_dt_bias", "dn_norm_w",
                 "rw_mu", "rw_w0", "rw_w2", "rw_a0", "rw_a2", "rw_g2", "rw_k_k", "rw_k_a", "rw_r_k",
                 "rw_ln_w", "rw_ln_b", "w_out", "g_ffn_pre", "g_ffn_post", "w_gate", "w_up", "w_down")


def kernel(x_prompt, x_sample, state_dn, cache_dn_conv, state_rwkv, cache_rwkv_shift,
           g_mix_pre, g_mix_post, w_in, dn_conv_w, dn_a_log, dn_dt_bias, dn_norm_w,
           rw_mu, rw_w0, rw_w2, rw_a0, rw_a2, rw_g2, rw_k_k, rw_k_a, rw_r_k, rw_ln_w, rw_ln_b,
           w_out, g_ffn_pre, g_ffn_post, w_gate, w_up, w_down):
    stacked = (g_mix_pre, g_mix_post, w_in, dn_conv_w, dn_a_log, dn_dt_bias, dn_norm_w,
               rw_mu, rw_w0, rw_w2, rw_a0, rw_a2, rw_g2, rw_k_k, rw_k_a, rw_r_k, rw_ln_w, rw_ln_b,
               w_out, g_ffn_pre, g_ffn_post, w_gate, w_up, w_down)
    depth = w_in.shape[0]
    bp = x_prompt.shape[0]
    dt = x_prompt.dtype
    yp, ys = x_prompt, x_sample
    outs_p, outs_s = [], []
    for l in range(depth):
        w = {n: a[l] for n, a in zip(_WEIGHT_NAMES, stacked)}
        pp = _prepare(w)
        heads, dk, rw_heads = pp["heads"], pp["dk"], pp["rw_heads"]
        hn = LANE // 2
        yp, st_p = _layer(yp,
                          jnp.zeros((bp, heads, dk, dk), dt),
                          jnp.zeros((bp, cache_dn_conv.shape[2], cache_dn_conv.shape[3]), dt),
                          jnp.zeros((bp, rw_heads, hn, hn), dt),
                          jnp.zeros((bp, 1, cache_rwkv_shift.shape[3]), dt), w, pp)
        ys, st_s = _layer(ys, state_dn[l], cache_dn_conv[l], state_rwkv[l], cache_rwkv_shift[l], w, pp)
        outs_p.append(st_p)
        outs_s.append(st_s)
    stack = lambda outs, i: jnp.stack([o[i] for o in outs])
    return (yp, ys,
            stack(outs_p, 0), stack(outs_p, 1), stack(outs_p, 2), stack(outs_p, 3),
            stack(outs_s, 0), stack(outs_s, 1), stack(outs_s, 2), stack(outs_s, 3))
```

```python
import functools
import math

import jax
import jax.numpy as jnp
from jax import lax
from jax.experimental import pallas as pl
from jax.experimental.pallas import tpu as pltpu

F32 = jnp.float32
BF16 = jnp.bfloat16

LANE = 128
SUBLANE = 8
VMEM_LIMIT_BYTES = 56 * 2**20
NORM_EPS = 1e-6
L2_EPS = 1e-6
RW_GN_EPS = 64e-5
IN_TN = 512
IN_TM_MAX = 1024
REC_CHUNKS = 4
REC_CHUNK_TICKS = 3
DN_CHUNK_TICKS = 2


def _cparams(sem):
    return pltpu.CompilerParams(dimension_semantics=sem, vmem_limit_bytes=VMEM_LIMIT_BYTES)


def _round_up(x, m):
    return -(-x // m) * m


def _pick(n, cands):
    for c in cands:
        if n % c == 0:
            return c
    return n


def _dot(a, b):
    return jnp.dot(a.astype(BF16), b.astype(BF16), preferred_element_type=F32)


def _dot_nt(a, b):
    return lax.dot_general(a.astype(BF16), b.astype(BF16), (((1,), (1,)), ((), ())),
                           preferred_element_type=F32)


def _dot_tn(a, b):
    return lax.dot_general(a.astype(BF16), b.astype(BF16), (((0,), (0,)), ((), ())),
                           preferred_element_type=F32)


def _cumsum_rows(x):
    c = x.shape[0]
    tri = (_iota2((c, c), 0) >= _iota2((c, c), 1)).astype(BF16)
    hi = x.astype(BF16)
    r1 = x - hi.astype(F32)
    mid = r1.astype(BF16)
    lo = (r1 - mid.astype(F32)).astype(BF16)
    dot = lambda p: jnp.dot(tri, p, preferred_element_type=F32)
    return dot(hi) + (dot(mid) + dot(lo))


def _sigmoid(x):
    return 0.5 * jnp.tanh(0.5 * x) + 0.5


def _softplus(x):
    return jnp.maximum(x, 0.0) + jnp.log(1.0 + jnp.exp(-jnp.abs(x)))


def _iota2(shape, dim):
    return lax.broadcasted_iota(jnp.int32, shape, dim)


def _run_staggered(programs):
    live = list(programs)
    tick = 0
    while live:
        still = []
        for start, prog in live:
            if tick >= start:
                try:
                    next(prog)
                except StopIteration:
                    continue
            still.append((start, prog))
        live = still
        tick += 1


def _rms(x, g):
    return x * lax.rsqrt(jnp.mean(x * x, axis=-1, keepdims=True) + NORM_EPS) * g


def _rms_kernel(x_ref, g_ref, o_ref):
    o_ref[...] = _rms(x_ref[...], g_ref[...]).astype(o_ref.dtype)


def _rms_bf16(x, g):
    m, d = x.shape
    tm = _pick(m, (256, 128))
    return pl.pallas_call(
        _rms_kernel,
        grid=(m // tm,),
        in_specs=[pl.BlockSpec((tm, d), lambda i: (i, 0)), pl.BlockSpec((1, d), lambda i: (0, 0))],
        out_specs=pl.BlockSpec((tm, d), lambda i: (i, 0)),
        out_shape=jax.ShapeDtypeStruct((m, d), BF16),
        compiler_params=_cparams(("parallel",)),
        name="rms_pre",
    )(x, g.reshape(1, d))


def _resid_rms_kernel(x_ref, y_ref, gp_ref, gn_ref, x1_ref, h_ref):
    x1 = x_ref[...] + _rms(y_ref[...].astype(F32), gp_ref[...])
    x1_ref[...] = x1
    h_ref[...] = _rms(x1, gn_ref[...]).astype(h_ref.dtype)


def _resid_rms(x, y, g_post, g_next):
    m, d = x.shape
    tm = _pick(m, (256, 128))
    row = pl.BlockSpec((tm, d), lambda i: (i, 0))
    par = pl.BlockSpec((1, d), lambda i: (0, 0))
    return pl.pallas_call(
        _resid_rms_kernel,
        grid=(m // tm,),
        in_specs=[row, row, par, par],
        out_specs=[row, row],
        out_shape=[jax.ShapeDtypeStruct((m, d), F32), jax.ShapeDtypeStruct((m, d), BF16)],
        compiler_params=_cparams(("parallel",)),
        name="resid_rms",
    )(x, y, g_post.reshape(1, d), g_next.reshape(1, d))


def _resid_final_kernel(x_ref, y_ref, g_ref, o_ref):
    o_ref[...] = x_ref[...] + _rms(y_ref[...].astype(F32), g_ref[...])


def _resid_final(x, y, g):
    m, d = x.shape
    tm = _pick(m, (256, 128))
    row = pl.BlockSpec((tm, d), lambda i: (i, 0))
    return pl.pallas_call(
        _resid_final_kernel,
        grid=(m // tm,),
        in_specs=[row, row, pl.BlockSpec((1, d), lambda i: (0, 0))],
        out_specs=row,
        out_shape=jax.ShapeDtypeStruct((m, d), F32),
        compiler_params=_cparams(("parallel",)),
        name="resid_final",
    )(x, y, g.reshape(1, d))


def _mm_kernel(x_ref, w_ref, o_ref):
    o_ref[...] = jnp.dot(x_ref[...], w_ref[...], preferred_element_type=F32).astype(o_ref.dtype)


def _matmul(x, w, *, tm, tn, out_dtype, name):
    m, k = x.shape
    n = w.shape[1]
    return pl.pallas_call(
        _mm_kernel,
        grid=(m // tm, n // tn),
        in_specs=[pl.BlockSpec((tm, k), lambda i, j: (i, 0)), pl.BlockSpec((k, tn), lambda i, j: (0, j))],
        out_specs=pl.BlockSpec((tm, tn), lambda i, j: (i, j)),
        out_shape=jax.ShapeDtypeStruct((m, n), out_dtype),
        compiler_params=_cparams(("parallel", "arbitrary")),
        name=name,
    )(x, w)


def _mm_w32t_kernel(x_ref, wt_ref, o_ref, *, n_valid):
    wt = wt_ref[...]
    tn = wt.shape[0]
    if n_valid % tn:
        wt = jnp.where(pl.program_id(1) * tn + _iota2(wt.shape, 0) < n_valid, wt, 0.0)
    o_ref[...] = lax.dot_general(x_ref[...], wt.astype(BF16), (((1,), (1,)), ((), ())),
                                 preferred_element_type=F32)


def _matmul_w32t(x, wt, *, tm, tn, name):
    m, k = x.shape
    n = wt.shape[0]
    nt = pl.cdiv(n, tn)
    return pl.pallas_call(
        functools.partial(_mm_w32t_kernel, n_valid=n),
        grid=(m // tm, nt),
        in_specs=[pl.BlockSpec((tm, k), lambda i, j: (i, 0)), pl.BlockSpec((tn, k), lambda i, j: (j, 0))],
        out_specs=pl.BlockSpec((tm, tn), lambda i, j: (i, j)),
        out_shape=jax.ShapeDtypeStruct((m, nt * tn), F32),
        compiler_params=_cparams(("parallel", "arbitrary")),
        name=name,
    )(x, wt)


def _mm2_kernel(xa_ref, xb_ref, wa_ref, wb_ref, o_ref):
    acc = jnp.dot(xa_ref[...], wa_ref[...].astype(BF16), preferred_element_type=F32)
    acc = acc + jnp.dot(xb_ref[...], wb_ref[...].astype(BF16), preferred_element_type=F32)
    o_ref[...] = acc.astype(o_ref.dtype)


def _matmul2(xa, xb, w, *, tm, tn, name):
    m, ka = xa.shape
    kb = xb.shape[1]
    n = w.shape[1]
    assert ka == kb and w.shape[0] == ka + kb
    return pl.pallas_call(
        _mm2_kernel,
        grid=(m // tm, n // tn),
        in_specs=[pl.BlockSpec((tm, ka), lambda i, j: (i, 0)), pl.BlockSpec((tm, kb), lambda i, j: (i, 0)),
                  pl.BlockSpec((ka, tn), lambda i, j: (0, j)), pl.BlockSpec((kb, tn), lambda i, j: (1, j))],
        out_specs=pl.BlockSpec((tm, tn), lambda i, j: (i, j)),
        out_shape=jax.ShapeDtypeStruct((m, n), BF16),
        compiler_params=_cparams(("parallel", "arbitrary")),
        name=name,
    )(xa, xb, w, w)


def _swiglu_kernel(x_ref, wg_ref, wu_ref, o_ref):
    x = x_ref[...]
    g = jnp.dot(x, wg_ref[...].astype(BF16), preferred_element_type=F32)
    u = jnp.dot(x, wu_ref[...].astype(BF16), preferred_element_type=F32)
    o_ref[...] = (g * _sigmoid(g) * u).astype(o_ref.dtype)


def _swiglu_up(x, wg, wu, *, tm, tn):
    m, k = x.shape
    n = wg.shape[1]
    wspec = pl.BlockSpec((k, tn), lambda i, j: (0, j))
    return pl.pallas_call(
        _swiglu_kernel,
        grid=(m // tm, n // tn),
        in_specs=[pl.BlockSpec((tm, k), lambda i, j: (i, 0)), wspec, wspec],
        out_specs=pl.BlockSpec((tm, tn), lambda i, j: (i, j)),
        out_shape=jax.ShapeDtypeStruct((m, n), BF16),
        compiler_params=_cparams(("parallel", "arbitrary")),
        name="ffn_up",
    )(x, wg, wu)


def _shift_rows(u, halo, s):
    ru = pltpu.roll(u, s, axis=0)
    rh = pltpu.roll(halo, s, axis=0)
    top = jnp.where(_iota2(halo.shape, 0) < s, rh, ru[:SUBLANE])
    if u.shape[0] == SUBLANE:
        return top
    return jnp.concatenate([top, ru[SUBLANE:]], axis=0)


def _conv_silu_strips(load, first_halo, cw_ref, store, *, tm, heads, dk, scale, normalize=None, rows=64):
    rb = min(tm, rows)
    for h in range(heads):
        cols = slice(h * dk, (h + 1) * dk)
        cw = cw_ref[:, cols]
        for r0 in range(0, tm, rb):
            u = load(r0, r0 + rb, cols)
            halo = first_halo(cols) if r0 == 0 else load(r0 - SUBLANE, r0, cols)
            prev = [_shift_rows(u, halo, sh) for sh in (3, 2, 1)]
            conv = prev[0] * cw[0:1]
            conv = conv + prev[1] * cw[1:2]
            conv = conv + prev[2] * cw[2:3]
            conv = conv + u * cw[3:4]
            s = conv * _sigmoid(conv)
            if scale is not None:
                normed = s * (lax.rsqrt(jnp.sum(s * s, axis=-1, keepdims=True) + L2_EPS) * scale)
                s = normed if normalize is None else jnp.where(normalize, normed, s)
            store(r0, r0 + rb, cols, s)


def _dn_pre_kernel(p_ref, halo_ref, cache_ref, cw_ref, qkv_ref, *, heads, dk):
    i = pl.program_id(1)
    j = pl.program_id(2)
    tm = p_ref.shape[1]

    def store(r0, r1, cols, value):
        qkv_ref[0, r0:r1, cols] = value

    def strips(scale):
        _conv_silu_strips(lambda r0, r1, cols: p_ref[0, r0:r1, cols],
                          lambda cols: jnp.where(i == 0, cache_ref[0, :, cols], halo_ref[0, :, cols]),
                          cw_ref, store, tm=tm, heads=heads, dk=dk, scale=scale)

    @pl.when(j < 2)
    def _():
        strips(jnp.where(j == 0, dk ** -0.5, 1.0).astype(F32))

    @pl.when(j == 2)
    def _():
        strips(None)


def _dn_pre(p, cache8, conv_w, *, heads, dk):
    b, t, _ = p.shape
    w = heads * dk
    tm = _pick(t, (256, 128, 64, 32, 16))
    kern = functools.partial(_dn_pre_kernel, heads=heads, dk=dk)
    return pl.pallas_call(
        kern,
        grid=(b, t // tm, 3),
        in_specs=[
            pl.BlockSpec((1, tm, w), lambda bb, i, j: (bb, i, j)),
            pl.BlockSpec((1, SUBLANE, w), lambda bb, i, j: (bb, jnp.maximum(i * (tm // SUBLANE) - 1, 0), j)),
            pl.BlockSpec((1, SUBLANE, w), lambda bb, i, j: (bb, 0, j)),
            pl.BlockSpec((4, w), lambda bb, i, j: (0, j)),
        ],
        out_specs=pl.BlockSpec((1, tm, w), lambda bb, i, j: (bb, i, j)),
        out_shape=jax.ShapeDtypeStruct((b, t, 3 * w), F32),
        compiler_params=_cparams(("parallel", "parallel", "arbitrary")),
        name="dn_pre",
    )(p, p, cache8, conv_w)


def _dn_gates_kernel(ba_ref, alog_ref, dtb_ref, gb_ref, *, heads, hb):
    x = ba_ref[0]
    lane = _iota2(x.shape, 1)
    g = -jnp.exp(alog_ref[...]) * _softplus(x + dtb_ref[...])
    full = jnp.where(lane < heads, _sigmoid(x), g)
    for hg in range(heads // hb):
        gb_ref[0, hg] = full if hg == 0 else pltpu.roll(full, LANE - hg * hb, axis=1)


def _dn_gates(p, alog_row, dtb_row, *, heads, hb, c_ba):
    b, t, _ = p.shape
    tm = _pick(t, (1024, 512, 256, 128, 64, 32, 16))
    groups = heads // hb
    return pl.pallas_call(
        functools.partial(_dn_gates_kernel, heads=heads, hb=hb),
        grid=(b, t // tm),
        in_specs=[pl.BlockSpec((1, tm, LANE), lambda bb, i: (bb, i, c_ba // LANE)),
                  pl.BlockSpec((1, LANE), lambda bb, i: (0, 0)),
                  pl.BlockSpec((1, LANE), lambda bb, i: (0, 0))],
        out_specs=pl.BlockSpec((1, groups, tm, LANE), lambda bb, i: (bb, 0, i, 0)),
        out_shape=jax.ShapeDtypeStruct((b, groups, t, LANE), F32),
        compiler_params=_cparams(("parallel", "parallel")),
        name="dn_gates",
    )(p, alog_row, dtb_row)


def _in_proj_dn_kernel(x_ref, wt_ref, cache_ref, cw_ref, p_ref, qkv_ref, raw_scr, halo_scr,
                       *, n_valid, nq, dk, tiles_per_seq):
    i = pl.program_id(0)
    j = pl.program_id(1)
    tm, tn = p_ref.shape
    jj = jnp.clip(j - 1, 0, nq - 1)
    first = (i % tiles_per_seq) == 0
    active = (j >= 1) & (j <= nq)

    @pl.when((i == 0) & (j == 0))
    def _():
        raw_scr[...] = jnp.zeros_like(raw_scr)
        halo_scr[...] = jnp.zeros_like(halo_scr)

    def store(r0, r1, cols, value):
        qkv_ref[r0:r1, cols] = value

    def matmul():
        wt = wt_ref[...]
        if n_valid % tn:
            wt = jnp.where(j * tn + _iota2(wt.shape, 0) < n_valid, wt, 0.0)
        acc = lax.dot_general(x_ref[...], wt.astype(BF16), (((1,), (1,)), ((), ())),
                              preferred_element_type=F32)
        p_ref[...] = acc
        return acc

    def fused_step(write_slot):
        read_slot = 1 - write_slot
        raw_scr[write_slot] = matmul()
        _conv_silu_strips(lambda r0, r1, cols: raw_scr[read_slot, r0:r1, cols],
                          lambda cols: jnp.where(first, cache_ref[0, :, cols], halo_scr[jj, :, cols]),
                          cw_ref, store, tm=tm, heads=tn // dk, dk=dk,
                          scale=jnp.where(jj < nq // 3, dk ** -0.5, 1.0).astype(F32),
                          normalize=jj < 2 * nq // 3)
        halo_scr[jj] = raw_scr[read_slot, tm - SUBLANE:tm, :]

    for parity in range(2):
        pl.when(active & (j % 2 == parity))(functools.partial(fused_step, parity))

    @pl.when(jnp.logical_not(active))
    def _():
        raw_scr[0] = matmul()


def _in_proj_dn(x, wt, cache8, conv_w, *, t, tm, tn, o1, dk):
    m, k = x.shape
    n = wt.shape[0]
    nt = pl.cdiv(n, tn)
    nq = o1 // tn
    tiles_per_seq = t // tm
    assert o1 % (3 * tn) == 0 and t % tm == 0 and tn % dk == 0 and nt > nq
    qcol = lambda i, j: jnp.clip(j - 1, 0, nq - 1)
    kern = functools.partial(_in_proj_dn_kernel, n_valid=n, nq=nq, dk=dk, tiles_per_seq=tiles_per_seq)
    return pl.pallas_call(
        kern,
        grid=(m // tm, nt),
        in_specs=[pl.BlockSpec((tm, k), lambda i, j: (i, 0)),
                  pl.BlockSpec((tn, k), lambda i, j: (j, 0)),
                  pl.BlockSpec((1, SUBLANE, tn), lambda i, j: (i // tiles_per_seq, 0, qcol(i, j))),
                  pl.BlockSpec((4, tn), lambda i, j: (0, qcol(i, j)))],
        out_specs=[pl.BlockSpec((tm, tn), lambda i, j: (i, j)),
                   pl.BlockSpec((tm, tn), lambda i, j: (i, qcol(i, j)))],
        out_shape=[jax.ShapeDtypeStruct((m, nt * tn), F32), jax.ShapeDtypeStruct((m, o1), F32)],
        scratch_shapes=[pltpu.VMEM((2, tm, tn), F32), pltpu.VMEM((nq, SUBLANE, tn), F32)],
        compiler_params=_cparams(("arbitrary", "arbitrary")),
        name="in_proj",
    )(x, wt, cache8, conv_w)


def _dn_kernel(q_ref, k_ref, v_ref, gb_ref, z_ref, nw_ref, s0_ref, y_ref, sout_ref, s_scr,
               *, heads, hb, dk, chunk, nch):
    c = pl.program_id(2)
    nc = pl.num_programs(2)

    @pl.when(c == 0)
    def _():
        s_scr[...] = s0_ref[0]

    row = _iota2((chunk, chunk), 0)
    col = _iota2((chunk, chunk), 1)
    causal = row >= col
    strict = row > col
    nw = nw_ref[...]
    eye = (row == col).astype(F32)
    state = {h: s_scr[h] for h in range(hb)}
    applied = {h: 0 for h in range(hb)}
    per_chunk = {}

    def chunk_gates(j):
        if j not in per_chunk:
            gbt = gb_ref[0, 0, j * chunk:(j + 1) * chunk, :]
            gc_all = _cumsum_rows(gbt)
            per_chunk[j] = (gbt, gc_all, jnp.transpose(gc_all))
        return per_chunk[j]

    def program(hs, j):
        rows = slice(j * chunk, (j + 1) * chunk)
        gbt, gc_all, gc_t = chunk_gates(j)
        sl = {h: slice(h * dk, (h + 1) * dk) for h in hs}
        q = {h: q_ref[0, rows, sl[h]] for h in hs}
        k = {h: k_ref[0, rows, sl[h]] for h in hs}
        v = {h: v_ref[0, rows, sl[h]] for h in hs}
        beta = {h: gbt[:, h:h + 1] for h in hs}
        gcol = {h: gc_all[:, heads + h:heads + h + 1] for h in hs}
        glast = {h: gc_all[chunk - 1:chunk, heads + h:heads + h + 1] for h in hs}
        decay = {h: jnp.where(causal, jnp.exp(gcol[h] - gc_t[heads + h:heads + h + 1, :]), 0.0) for h in hs}
        kb = {h: k[h] * beta[h] for h in hs}
        eg = {h: jnp.exp(gcol[h]) for h in hs}
        yield
        m = {h: jnp.where(strict, -_dot_nt(kb[h], k[h]) * decay[h], 0.0) for h in hs}
        qk = {h: jnp.where(causal, _dot_nt(q[h], k[h]) * decay[h], 0.0) for h in hs}
        yield
        t = {h: eye + m[h] for h in hs}
        m = {h: _dot(m[h], m[h]) for h in hs}
        for _ in range(chunk.bit_length() - 3):
            yield
            res = {h: _dot(jnp.concatenate([m[h], t[h]], axis=0), m[h]) for h in hs}
            m = {h: res[h][:chunk] for h in hs}
            t = {h: t[h] + res[h][chunk:] for h in hs}
        yield
        t = {h: t[h] + _dot(t[h], m[h]) for h in hs}
        yield
        sol = {h: _dot(t[h], jnp.concatenate([v[h] * beta[h], kb[h] * eg[h]], axis=1)) for h in hs}
        yield
        assert all(applied[h] == j for h in hs)
        s = {h: state[h] for h in hs}
        v_new = {h: sol[h][:, :dk] - _dot(sol[h][:, dk:], s[h]) for h in hs}
        os = {h: _dot(q[h] * eg[h], s[h]) for h in hs}
        yield
        o = {h: os[h] + _dot(qk[h], v_new[h]) for h in hs}
        for h in hs:
            state[h] = s[h] * jnp.exp(glast[h]) + _dot_tn(k[h] * jnp.exp(glast[h] - gcol[h]), v_new[h])
            applied[h] = j + 1
        yield
        for h in hs:
            z = z_ref[0, rows, sl[h]]
            y_ref[0, rows, sl[h]] = (_rms(o[h], nw) * (z * _sigmoid(z))).astype(y_ref.dtype)

    gsz = _pick(hb, (4, 2, 1))
    _run_staggered([(j * DN_CHUNK_TICKS, program(range(g0, g0 + gsz), j))
                    for j in range(nch) for g0 in range(0, hb, gsz)])
    for h in range(hb):
        s_scr[h] = state[h]

    @pl.when(c == nc - 1)
    def _():
        sout_ref[0] = s_scr[...]


def _dn_recurrence(qkv, gb, p, norm_w, s0, *, heads, hb, dk, chunk, c_z):
    b, t, _ = qkv.shape
    groups = heads // hb
    wb = hb * dk
    nqk = heads * dk // wb
    nch = _pick(t // chunk, (REC_CHUNKS, 2, 1))
    rows = nch * chunk
    kern = functools.partial(_dn_kernel, heads=heads, hb=hb, dk=dk, chunk=chunk, nch=nch)
    return pl.pallas_call(
        kern,
        grid=(b, groups, t // rows),
        in_specs=[
            pl.BlockSpec((1, rows, wb), lambda bb, g, c: (bb, c, g)),
            pl.BlockSpec((1, rows, wb), lambda bb, g, c: (bb, c, nqk + g)),
            pl.BlockSpec((1, rows, wb), lambda bb, g, c: (bb, c, 2 * nqk + g)),
            pl.BlockSpec((1, 1, rows, LANE), lambda bb, g, c: (bb, g, c, 0)),
            pl.BlockSpec((1, rows, wb), lambda bb, g, c: (bb, c, c_z // wb + g)),
            pl.BlockSpec((1, dk), lambda bb, g, c: (0, 0)),
            pl.BlockSpec((1, hb, dk, dk), lambda bb, g, c: (bb, g, 0, 0)),
        ],
        out_specs=[
            pl.BlockSpec((1, rows, wb), lambda bb, g, c: (bb, c, g)),
            pl.BlockSpec((1, hb, dk, dk), lambda bb, g, c: (bb, g, 0, 0)),
        ],
        out_shape=[jax.ShapeDtypeStruct((b, t, heads * dk), BF16),
                   jax.ShapeDtypeStruct((b, heads, dk, dk), F32)],
        scratch_shapes=[pltpu.VMEM((hb, dk, dk), F32)],
        compiler_params=_cparams(("parallel", "parallel", "arbitrary")),
        name="dn_recurrence",
    )(qkv, qkv, qkv, gb, p, norm_w.reshape(1, dk), s0)


def _pair_sums(x, m0):
    s0 = jnp.sum(jnp.where(m0, x, 0.0), axis=-1, keepdims=True)
    s1 = jnp.sum(jnp.where(m0, 0.0, x), axis=-1, keepdims=True)
    return jnp.where(m0, s0, s1)


def _rw_pre_kernel(*refs, nblk, off, rw, xw_sl, xa_sl, xg_sl):
    main, halos = refs[:nblk], refs[nblk:2 * nblk]
    (cache_ref, mu_ref, w0_ref, w2_ref, a0_ref, a2_ref, g2_ref, kk_ref, ka_ref,
     r_ref, k_ref, v_ref, lw_ref, av_ref, bv_ref, gate_ref) = refs[2 * nblk:]
    u = jnp.concatenate([ref[0] for ref in main], axis=1)
    halo = jnp.concatenate([ref[0] for ref in halos], axis=1)
    halo = jnp.where(pl.program_id(1) == 0, cache_ref[0], halo)
    x = u + mu_ref[...] * (_shift_rows(u, halo, 1) - u)
    x = pltpu.roll(x, x.shape[1] - off, axis=1)
    r = x[:, :rw]
    kr = x[:, rw:2 * rw]
    vr = x[:, 2 * rw:3 * rw]
    z = w0_ref[...] + _dot(jnp.tanh(x[:, xw_sl[0]:xw_sl[1]]), w2_ref[...])
    a = _sigmoid(a0_ref[...] + _dot(x[:, xa_sl[0]:xa_sl[1]], a2_ref[...]))
    gate_ref[0] = _dot(_sigmoid(x[:, xg_sl[0]:xg_sl[1]]), g2_ref[...])
    r_ref[0] = r
    v_ref[0] = vr
    lw_ref[0] = -math.exp(-0.5) * _sigmoid(z)
    k_ref[0] = kr * (1.0 + (a - 1.0) * ka_ref[...])
    kkr = kr * kk_ref[...]
    m0 = _iota2((1, LANE), 1) < LANE // 2
    for jb in range(rw // LANE):
        sl = slice(jb * LANE, (jb + 1) * LANE)
        blk = kkr[:, sl]
        kk = blk * lax.rsqrt(_pair_sums(blk * blk, m0) + L2_EPS)
        av_ref[0, :, sl] = -kk
        bv_ref[0, :, sl] = kk * a[:, sl]


def _rw_pre(p, cache8, pp):
    b, t, _ = p.shape
    rw, win0, width, bw = pp["rw"], pp["win0"], pp["win_w"], pp["win_bw"]
    nblk = width // bw
    tm = _pick(t, (128, 64, 32, 16))
    kern = functools.partial(_rw_pre_kernel, nblk=nblk, off=pp["win_off"], rw=rw,
                             xw_sl=pp["xw_sl"], xa_sl=pp["xa_sl"], xg_sl=pp["xg_sl"])
    full = lambda shape: pl.BlockSpec(shape, lambda bb, i: (0,) * len(shape))
    row = pl.BlockSpec((1, tm, rw), lambda bb, i: (bb, i, 0))
    cb = [win0 // bw + n for n in range(nblk)]
    main = [pl.BlockSpec((1, tm, bw), lambda bb, i, c=c: (bb, i, c)) for c in cb]
    halos = [pl.BlockSpec((1, SUBLANE, bw),
                          lambda bb, i, c=c: (bb, jnp.maximum(i * (tm // SUBLANE) - 1, 0), c)) for c in cb]
    small = [pp["mu_win"], pp["w0"], pp["w2p"], pp["a0"], pp["a2p"], pp["g2p"], pp["k_k"], pp["k_a"]]
    return pl.pallas_call(
        kern,
        grid=(b, t // tm),
        in_specs=main + halos + [pl.BlockSpec((1, SUBLANE, width), lambda bb, i: (bb, 0, 0))]
        + [full(a.shape) for a in small],
        out_specs=[row] * 7,
        out_shape=[jax.ShapeDtypeStruct((b, t, rw), F32)] * 7,
        compiler_params=_cparams(("parallel", "arbitrary")),
        name="rw_pre",
    )(*([p] * (2 * nblk)), cache8, *small)


def _rw_kernel(r_ref, k_ref, v_ref, lw_ref, av_ref, bv_ref, gate_ref, rk_ref, lnw_ref, lnb_ref, s0_ref,
               y_ref, sout_ref, s_scr, *, pb, chunk, nch):
    c = pl.program_id(2)
    nc = pl.num_programs(2)
    hn = LANE // 2

    @pl.when(c == 0)
    def _():
        s_scr[...] = s0_ref[0]

    c2 = 2 * chunk
    m0 = _iota2((1, LANE), 1) < hn
    m1 = jnp.logical_not(m0)
    blockmask = (_iota2((LANE, LANE), 0) < hn) == (_iota2((LANE, LANE), 1) < hn)
    row2 = _iota2((chunk, c2), 0)
    col2 = _iota2((chunk, c2), 1) & (chunk - 1)
    strict2 = row2 > col2
    eye2 = (row2 == col2).astype(F32)
    incl4 = _iota2((chunk, 2 * c2), 0) >= (_iota2((chunk, 2 * c2), 1) & (chunk - 1))

    def by_head(x):
        return jnp.concatenate([jnp.where(m0, x, 0.0), jnp.where(m1, x, 0.0)], axis=0)

    bd = (_iota2((c2, c2), 0) < chunk) == (_iota2((c2, c2), 1) < chunk)

    def blockdiag(p2):
        return jnp.where(bd, jnp.concatenate([p2, p2], axis=0), 0.0)

    state = {p: s_scr[p] for p in range(pb)}
    applied = {p: 0 for p in range(pb)}

    def program(ps, j):
        rows = slice(j * chunk, (j + 1) * chunk)
        lanes = slice(ps[0] * LANE, (ps[-1] + 1) * LANE)
        sl = {p: slice(p * LANE, (p + 1) * LANE) for p in ps}
        loc = {p: slice((p - ps[0]) * LANE, (p - ps[0] + 1) * LANE) for p in ps}
        r = {p: r_ref[0, rows, sl[p]] for p in ps}
        k = {p: k_ref[0, rows, sl[p]] for p in ps}
        v = {p: v_ref[0, rows, sl[p]] for p in ps}
        lw = {p: lw_ref[0, rows, sl[p]] for p in ps}
        bv = {p: bv_ref[0, rows, sl[p]] for p in ps}
        cw_all = _cumsum_rows(lw_ref[0, rows, lanes])
        yield
        cw = {p: cw_all[:, loc[p]] for p in ps}
        tot = {p: cw[p][chunk - 1:chunk, :] for p in ps}
        e_neg = {p: jnp.exp(-cw[p]) for p in ps}
        e_end = {p: jnp.exp(tot[p] - cw[p]) for p in ps}
        lhs = {p: jnp.concatenate([av_ref[0, rows, sl[p]] * jnp.exp(cw[p] - lw[p]), r[p] * jnp.exp(cw[p])],
                                  axis=0) for p in ps}
        rhs_g = {p: jnp.concatenate([by_head(bv[p] * e_neg[p]), by_head(k[p] * e_neg[p])], axis=0) for p in ps}
        rhs_s = {p: jnp.concatenate([bv[p] * e_end[p], k[p] * e_end[p]], axis=0) for p in ps}
        vh = {p: by_head(v[p]) for p in ps}
        g = {p: _dot_nt(lhs[p], rhs_g[p]) for p in ps}
        yield
        avs = {p: _dot(jnp.where(strict2, g[p][:chunk, c2:], 0.0), vh[p]) for p in ps}
        m = {p: jnp.where(strict2, g[p][:chunk, :c2], 0.0) for p in ps}
        t2 = {p: eye2 + m[p] for p in ps}
        m = {p: _dot(m[p], blockdiag(m[p])) for p in ps}
        for _ in range(chunk.bit_length() - 3):
            yield
            res = {p: _dot(jnp.concatenate([m[p], t2[p]], axis=0), blockdiag(m[p])) for p in ps}
            m = {p: res[p][:chunk] for p in ps}
            t2 = {p: t2[p] + res[p][chunk:] for p in ps}
        yield
        t2 = {p: t2[p] + _dot(t2[p], blockdiag(m[p])) for p in ps}
        assert all(applied[p] == j for p in ps)
        s = {p: state[p] for p in ps}
        sr = {p: _dot_nt(lhs[p], s[p]) for p in ps}
        yield
        u = {p: _dot(t2[p], by_head(sr[p][:chunk] + avs[p])) for p in ps}
        yield
        yr = {p: _dot(jnp.where(incl4, g[p][chunk:], 0.0), jnp.concatenate([by_head(u[p]), vh[p]], axis=0))
              for p in ps}
        for p in ps:
            state[p] = jnp.where(blockmask, s[p] * jnp.exp(tot[p])
                                 + _dot_tn(jnp.concatenate([u[p], v[p]], axis=0), rhs_s[p]), 0.0)
            applied[p] = j + 1
        yield
        for p in ps:
            y = sr[p][chunk:] + yr[p]
            mean = _pair_sums(y, m0) * (1.0 / hn)
            d = y - mean
            var = _pair_sums(d * d, m0) * (1.0 / hn)
            yn = d * lax.rsqrt(var + RW_GN_EPS) * lnw_ref[:, sl[p]] + lnb_ref[:, sl[p]]
            bonus = _pair_sums(r[p] * k[p] * rk_ref[:, sl[p]], m0) * v[p]
            y_ref[0, rows, sl[p]] = ((yn + bonus) * gate_ref[0, rows, sl[p]]).astype(y_ref.dtype)

    gsz = _pick(pb, (4, 2, 1))
    _run_staggered([(j * REC_CHUNK_TICKS, program(range(g0, g0 + gsz), j))
                    for j in range(nch) for g0 in range(0, pb, gsz)])
    for p in range(pb):
        s_scr[p] = state[p]

    @pl.when(c == nc - 1)
    def _():
        sout_ref[0] = s_scr[...]


def _rw_recurrence(r, k, v, lw, av, bv, gate, r_k, ln_w, ln_b, s0p, *, pb, chunk):
    b, t, rw = r.shape
    pairs = rw // LANE
    groups = pairs // pb
    wb = pb * LANE
    nch = _pick(t // chunk, (REC_CHUNKS, 2, 1))
    kern = functools.partial(_rw_kernel, pb=pb, chunk=chunk, nch=nch)
    tile = pl.BlockSpec((1, nch * chunk, wb), lambda bb, g, c: (bb, c, g))
    par = pl.BlockSpec((1, wb), lambda bb, g, c: (0, g))
    st = pl.BlockSpec((1, pb, LANE, LANE), lambda bb, g, c: (bb, g, 0, 0))
    return pl.pallas_call(
        kern,
        grid=(b, groups, t // (nch * chunk)),
        in_specs=[tile] * 7 + [par] * 3 + [st],
        out_specs=[tile, st],
        out_shape=[jax.ShapeDtypeStruct((b, t, rw), BF16),
                   jax.ShapeDtypeStruct((b, pairs, LANE, LANE), F32)],
        scratch_shapes=[pltpu.VMEM((pb, LANE, LANE), F32)],
        compiler_params=_cparams(("parallel", "parallel", "arbitrary")),
        name="rw_recurrence",
    )(r, k, v, lw, av, bv, gate, r_k, ln_w, ln_b, s0p)


def _prepare(w):
    heads = w["dn_a_log"].shape[-1]
    dk = w["dn_norm_w"].shape[-1]
    qkv_w = w["dn_conv_w"].shape[-1]
    v_w = heads * dk
    assert qkv_w == 3 * v_w and dk == LANE
    rw_heads, rw_head = w["rw_r_k"].shape
    assert rw_head == LANE // 2
    rw = rw_heads * rw_head
    lw_n, la_n, lg_n = w["rw_w2"].shape[0], w["rw_a2"].shape[0], w["rw_g2"].shape[0]
    o1 = qkv_w
    o2 = o1 + v_w
    o4 = o2 + 2 * heads
    shift_w = 3 * rw + lw_n + la_n + lg_n
    assert o2 % LANE == 0 and 2 * heads <= LANE and w["w_in"].shape[1] == o4 + shift_w
    win0 = o4 // LANE * LANE
    win_off = o4 - win0
    win_w = _round_up(win_off + shift_w, LANE)
    win_bw = math.gcd(math.gcd(win0, win_w), 8 * LANE)
    assert win0 + win_w <= _round_up(o4 + shift_w, IN_TN)

    def lora_block(start, n, weight):
        lo, hi = start // LANE * LANE, _round_up(start + n, LANE)
        padded = jnp.pad(weight, ((start - lo, hi - start - n), (0, 0))).astype(BF16)
        return (lo, hi), padded

    xw_sl, w2p = lora_block(3 * rw, lw_n, w["rw_w2"])
    xa_sl, a2p = lora_block(3 * rw + lw_n, la_n, w["rw_a2"])
    xg_sl, g2p = lora_block(3 * rw + lw_n + la_n, lg_n, w["rw_g2"])
    lane_pad = lambda a: jnp.pad(a, (heads, LANE - 2 * heads)).reshape(1, LANE)
    in_window = lambda a: jnp.pad(a, [(0, 0)] * (a.ndim - 1) + [(win_off, win_w - win_off - shift_w)])
    return dict(
        heads=heads, dk=dk, rw=rw, rw_heads=rw_heads, o1=o1, o2=o2, o4=o4, shift_w=shift_w,
        win0=win0, win_off=win_off, win_w=win_w, win_bw=win_bw, in_window=in_window,
        xw_sl=xw_sl, xa_sl=xa_sl, xg_sl=xg_sl, w2p=w2p, a2p=a2p, g2p=g2p,
        alog_row=lane_pad(w["dn_a_log"]), dtb_row=lane_pad(w["dn_dt_bias"]),
        mu_win=in_window(w["rw_mu"].reshape(1, shift_w)),
        w0=w["rw_w0"].reshape(1, rw), a0=w["rw_a0"].reshape(1, rw),
        k_k=w["rw_k_k"].reshape(1, rw), k_a=w["rw_k_a"].reshape(1, rw),
        w_in_t=jnp.swapaxes(w["w_in"], 0, 1),
        w_down=w["w_down"].astype(BF16),
    )


def _layer(x, dn_state, dn_conv, rw_state, rw_shift, w, pp):
    b, t, d = x.shape
    m = b * t
    heads, dk, rw, rw_heads = pp["heads"], pp["dk"], pp["rw"], pp["rw_heads"]
    o1, o2, o4, shift_w = pp["o1"], pp["o2"], pp["o4"], pp["shift_w"]
    chunk = 64 if t % 64 == 0 else t
    assert chunk & (chunk - 1) == 0 and chunk >= 2 * SUBLANE
    hb = 16 if heads % 16 == 0 else heads
    pairs = rw // LANE
    pb = 16 if pairs % 16 == 0 else pairs
    tm = _pick(m, (1024, 512, 256, 128))

    xf = x.reshape(m, d)
    h = _rms_bf16(xf, w["g_mix_pre"])
    cache8 = jnp.pad(dn_conv.astype(F32), ((0, 0), (SUBLANE - dn_conv.shape[1], 0), (0, 0)))
    tm_in = min(tm, IN_TM_MAX)
    if t % tm_in == 0 and o1 % (3 * IN_TN) == 0:
        p, qkv = _in_proj_dn(h, pp["w_in_t"], cache8, w["dn_conv_w"], t=t, tm=tm_in, tn=IN_TN, o1=o1, dk=dk)
        p = p.reshape(b, t, p.shape[1])
        qkv = qkv.reshape(b, t, o1)
    else:
        p = _matmul_w32t(h, pp["w_in_t"], tm=tm, tn=IN_TN, name="in_proj")
        p = p.reshape(b, t, p.shape[1])
        qkv = _dn_pre(p, cache8, w["dn_conv_w"], heads=heads, dk=dk)

    gb = _dn_gates(p, pp["alog_row"], pp["dtb_row"], heads=heads, hb=hb, c_ba=o2)
    y_a, new_dn_state = _dn_recurrence(qkv, gb, p, w["dn_norm_w"], dn_state.astype(F32),
                                       heads=heads, hb=hb, dk=dk, chunk=chunk, c_z=o1)
    new_dn_conv = p[:, t - dn_conv.shape[1]:, :o1]

    shift8 = jnp.pad(pp["in_window"](rw_shift.astype(F32)), ((0, 0), (SUBLANE - 1, 0), (0, 0)))
    r, k, v, lw, av, bv, gate = _rw_pre(p, shift8, pp)
    hn = LANE // 2
    s4 = rw_state.astype(F32).reshape(b, pairs, 2, hn, hn)
    zeros = jnp.zeros_like(s4[:, :, 0])
    s0p = jnp.concatenate([jnp.concatenate([s4[:, :, 0], zeros], axis=-1),
                           jnp.concatenate([zeros, s4[:, :, 1]], axis=-1)], axis=-2)
    y_b, sp = _rw_recurrence(r, k, v, lw, av, bv, gate, w["rw_r_k"].reshape(1, rw),
                             w["rw_ln_w"].reshape(1, rw), w["rw_ln_b"].reshape(1, rw), s0p,
                             pb=pb, chunk=chunk)
    new_rw_state = jnp.stack([sp[:, :, :hn, :hn], sp[:, :, hn:, hn:]], axis=2).reshape(b, rw_heads, hn, hn)
    new_rw_shift = p[:, t - 1:, o4:o4 + shift_w]

    mixo = _matmul2(y_a.reshape(m, heads * dk), y_b.reshape(m, rw), w["w_out"],
                    tm=tm, tn=_pick(d, (512, 256, 128)), name="out_proj")
    x1, h2 = _resid_rms(xf, mixo, w["g_mix_post"], w["g_ffn_pre"])
    dff = w["w_gate"].shape[1]
    f = _swiglu_up(h2, w["w_gate"], w["w_up"], tm=_pick(m, (2048, 1024, 512, 256, 128)),
                   tn=_pick(dff, (256, 128)))
    fo = _matmul(f, pp["w_down"], tm=_pick(m, (512, 256, 128)), tn=_pick(d, (512, 256, 128)),
                 out_dtype=BF16, name="ffn_down")
    out = _resid_final(x1, fo, w["g_ffn_post"]).reshape(b, t, d)
    return out, (new_dn_state, new_dn_conv, new_rw_state, new_rw_shift)


_WEIGHT_NAMES = ("g_mix_pre", "g_mix_post", "w_in", "dn_conv_w", "dn_a_log", "dn_dt_bias", "dn_norm_w",
                 "rw_mu", "rw_w0", "rw_w2", "rw_a0", "rw_a2", "rw_g2", "rw_k_k", "rw_k_a", "rw_r_k",
                 "rw_ln_w", "rw_ln_b", "w_out", "g_ffn_pre", "g_ffn_post", "w_gate", "w_up", "w_down")


def kernel(x_prompt, x_sample, state_dn, cache_dn_conv, state_rwkv, cache_rwkv_shift,
           g_mix_pre, g_mix_post, w_in, dn_conv_w, dn_a_log, dn_dt_bias, dn_norm_w,
           rw_mu, rw_w0, rw_w2, rw_a0, rw_a2, rw_g2, rw_k_k, rw_k_a, rw_r_k, rw_ln_w, rw_ln_b,
           w_out, g_ffn_pre, g_ffn_post, w_gate, w_up, w_down):
    stacked = (g_mix_pre, g_mix_post, w_in, dn_conv_w, dn_a_log, dn_dt_bias, dn_norm_w,
               rw_mu, rw_w0, rw_w2, rw_a0, rw_a2, rw_g2, rw_k_k, rw_k_a, rw_r_k, rw_ln_w, rw_ln_b,
               w_out, g_ffn_pre, g_ffn_post, w_gate, w_up, w_down)
    depth = w_in.shape[0]
    bp = x_prompt.shape[0]
    dt = x_prompt.dtype
    yp, ys = x_prompt, x_sample
    outs_p, outs_s = [], []
    for l in range(depth):
        w = {n: a[l] for n, a in zip(_WEIGHT_NAMES, stacked)}
        pp = _prepare(w)
        heads, dk, rw_heads = pp["heads"], pp["dk"], pp["rw_heads"]
        hn = LANE // 2
        yp, st_p = _layer(yp,
                          jnp.zeros((bp, heads, dk, dk), dt),
                          jnp.zeros((bp, cache_dn_conv.shape[2], cache_dn_conv.shape[3]), dt),
                          jnp.zeros((bp, rw_heads, hn, hn), dt),
                          jnp.zeros((bp, 1, cache_rwkv_shift.shape[3]), dt), w, pp)
        ys, st_s = _layer(ys, state_dn[l], cache_dn_conv[l], state_rwkv[l], cache_rwkv_shift[l], w, pp)
        outs_p.append(st_p)
        outs_s.append(st_s)
    stack = lambda outs, i: jnp.stack([o[i] for o in outs])
    return (yp, ys,
            stack(outs_p, 0), stack(outs_p, 1), stack(outs_p, 2), stack(outs_p, 3),
            stack(outs_s, 0), stack(outs_s, 1), stack(outs_s, 2), stack(outs_s, 3))
```

```python
import functools
import math

import jax
import jax.numpy as jnp
from jax import lax
from jax.experimental import pallas as pl
from jax.experimental.pallas import tpu as pltpu

F32 = jnp.float32
BF16 = jnp.bfloat16

LANE = 128
SUBLANE = 8
VMEM_LIMIT_BYTES = 56 * 2**20
NORM_EPS = 1e-6
L2_EPS = 1e-6
RW_GN_EPS = 64e-5
IN_TN = 512
IN_TM_MAX = 1024
RW_STORE = BF16
REC_CHUNKS = 4
REC_CHUNK_TICKS = 3
DN_CHUNK_TICKS = 2


def _cparams(sem):
    return pltpu.CompilerParams(dimension_semantics=sem, vmem_limit_bytes=VMEM_LIMIT_BYTES)


def _round_up(x, m):
    return -(-x // m) * m


def _pick(n, cands):
    for c in cands:
        if n % c == 0:
            return c
    return n


def _dot(a, b):
    return jnp.dot(a.astype(BF16), b.astype(BF16), preferred_element_type=F32)


def _dot_nt(a, b):
    return lax.dot_general(a.astype(BF16), b.astype(BF16), (((1,), (1,)), ((), ())),
                           preferred_element_type=F32)


def _dot_tn(a, b):
    return lax.dot_general(a.astype(BF16), b.astype(BF16), (((0,), (0,)), ((), ())),
                           preferred_element_type=F32)


def _cumsum_rows(x):
    c = x.shape[0]
    tri = (_iota2((c, c), 0) >= _iota2((c, c), 1)).astype(BF16)
    hi = x.astype(BF16)
    r1 = x - hi.astype(F32)
    mid = r1.astype(BF16)
    lo = (r1 - mid.astype(F32)).astype(BF16)
    dot = lambda p: jnp.dot(tri, p, preferred_element_type=F32)
    return dot(hi) + (dot(mid) + dot(lo))


def _sigmoid(x):
    return 0.5 * jnp.tanh(0.5 * x) + 0.5


def _softplus(x):
    return jnp.maximum(x, 0.0) + jnp.log(1.0 + jnp.exp(-jnp.abs(x)))


def _iota2(shape, dim):
    return lax.broadcasted_iota(jnp.int32, shape, dim)


def _run_staggered(programs):
    live = list(programs)
    tick = 0
    while live:
        still = []
        for start, prog in live:
            if tick >= start:
                try:
                    next(prog)
                except StopIteration:
                    continue
            still.append((start, prog))
        live = still
        tick += 1


def _rms(x, g):
    return x * lax.rsqrt(jnp.mean(x * x, axis=-1, keepdims=True) + NORM_EPS) * g


def _rms_kernel(x_ref, g_ref, o_ref):
    o_ref[...] = _rms(x_ref[...], g_ref[...]).astype(o_ref.dtype)


def _rms_bf16(x, g):
    m, d = x.shape
    tm = _pick(m, (256, 128))
    return pl.pallas_call(
        _rms_kernel,
        grid=(m // tm,),
        in_specs=[pl.BlockSpec((tm, d), lambda i: (i, 0)), pl.BlockSpec((1, d), lambda i: (0, 0))],
        out_specs=pl.BlockSpec((tm, d), lambda i: (i, 0)),
        out_shape=jax.ShapeDtypeStruct((m, d), BF16),
        compiler_params=_cparams(("parallel",)),
        name="rms_pre",
    )(x, g.reshape(1, d))


def _resid_rms_kernel(x_ref, y_ref, gp_ref, gn_ref, x1_ref, h_ref):
    x1 = x_ref[...] + _rms(y_ref[...].astype(F32), gp_ref[...])
    x1_ref[...] = x1
    h_ref[...] = _rms(x1, gn_ref[...]).astype(h_ref.dtype)


def _resid_rms(x, y, g_post, g_next):
    m, d = x.shape
    tm = _pick(m, (256, 128))
    row = pl.BlockSpec((tm, d), lambda i: (i, 0))
    par = pl.BlockSpec((1, d), lambda i: (0, 0))
    return pl.pallas_call(
        _resid_rms_kernel,
        grid=(m // tm,),
        in_specs=[row, row, par, par],
        out_specs=[row, row],
        out_shape=[jax.ShapeDtypeStruct((m, d), F32), jax.ShapeDtypeStruct((m, d), BF16)],
        compiler_params=_cparams(("parallel",)),
        name="resid_rms",
    )(x, y, g_post.reshape(1, d), g_next.reshape(1, d))


def _resid_final_kernel(x_ref, y_ref, g_ref, o_ref):
    o_ref[...] = x_ref[...] + _rms(y_ref[...].astype(F32), g_ref[...])


def _resid_final(x, y, g):
    m, d = x.shape
    tm = _pick(m, (256, 128))
    row = pl.BlockSpec((tm, d), lambda i: (i, 0))
    return pl.pallas_call(
        _resid_final_kernel,
        grid=(m // tm,),
        in_specs=[row, row, pl.BlockSpec((1, d), lambda i: (0, 0))],
        out_specs=row,
        out_shape=jax.ShapeDtypeStruct((m, d), F32),
        compiler_params=_cparams(("parallel",)),
        name="resid_final",
    )(x, y, g.reshape(1, d))


def _mm_kernel(x_ref, w_ref, o_ref):
    o_ref[...] = jnp.dot(x_ref[...], w_ref[...], preferred_element_type=F32).astype(o_ref.dtype)


def _matmul(x, w, *, tm, tn, out_dtype, name):
    m, k = x.shape
    n = w.shape[1]
    return pl.pallas_call(
        _mm_kernel,
        grid=(m // tm, n // tn),
        in_specs=[pl.BlockSpec((tm, k), lambda i, j: (i, 0)), pl.BlockSpec((k, tn), lambda i, j: (0, j))],
        out_specs=pl.BlockSpec((tm, tn), lambda i, j: (i, j)),
        out_shape=jax.ShapeDtypeStruct((m, n), out_dtype),
        compiler_params=_cparams(("parallel", "arbitrary")),
        name=name,
    )(x, w)


def _mm_w32t_kernel(x_ref, wt_ref, o_ref, *, n_valid):
    wt = wt_ref[...]
    tn = wt.shape[0]
    if n_valid % tn:
        wt = jnp.where(pl.program_id(1) * tn + _iota2(wt.shape, 0) < n_valid, wt, 0.0)
    o_ref[...] = lax.dot_general(x_ref[...], wt.astype(BF16), (((1,), (1,)), ((), ())),
                                 preferred_element_type=F32)


def _matmul_w32t(x, wt, *, tm, tn, name):
    m, k = x.shape
    n = wt.shape[0]
    nt = pl.cdiv(n, tn)
    return pl.pallas_call(
        functools.partial(_mm_w32t_kernel, n_valid=n),
        grid=(m // tm, nt),
        in_specs=[pl.BlockSpec((tm, k), lambda i, j: (i, 0)), pl.BlockSpec((tn, k), lambda i, j: (j, 0))],
        out_specs=pl.BlockSpec((tm, tn), lambda i, j: (i, j)),
        out_shape=jax.ShapeDtypeStruct((m, nt * tn), F32),
        compiler_params=_cparams(("parallel", "arbitrary")),
        name=name,
    )(x, wt)


def _mm2_kernel(xa_ref, xb_ref, wa_ref, wb_ref, o_ref):
    acc = jnp.dot(xa_ref[...], wa_ref[...].astype(BF16), preferred_element_type=F32)
    acc = acc + jnp.dot(xb_ref[...], wb_ref[...].astype(BF16), preferred_element_type=F32)
    o_ref[...] = acc.astype(o_ref.dtype)


def _matmul2(xa, xb, w, *, tm, tn, name):
    m, ka = xa.shape
    kb = xb.shape[1]
    n = w.shape[1]
    assert ka == kb and w.shape[0] == ka + kb
    return pl.pallas_call(
        _mm2_kernel,
        grid=(m // tm, n // tn),
        in_specs=[pl.BlockSpec((tm, ka), lambda i, j: (i, 0)), pl.BlockSpec((tm, kb), lambda i, j: (i, 0)),
                  pl.BlockSpec((ka, tn), lambda i, j: (0, j)), pl.BlockSpec((kb, tn), lambda i, j: (1, j))],
        out_specs=pl.BlockSpec((tm, tn), lambda i, j: (i, j)),
        out_shape=jax.ShapeDtypeStruct((m, n), BF16),
        compiler_params=_cparams(("parallel", "arbitrary")),
        name=name,
    )(xa, xb, w, w)


def _swiglu_kernel(x_ref, wg_ref, wu_ref, o_ref):
    x = x_ref[...]
    g = jnp.dot(x, wg_ref[...].astype(BF16), preferred_element_type=F32)
    u = jnp.dot(x, wu_ref[...].astype(BF16), preferred_element_type=F32)
    o_ref[...] = (g * _sigmoid(g) * u).astype(o_ref.dtype)


def _swiglu_up(x, wg, wu, *, tm, tn):
    m, k = x.shape
    n = wg.shape[1]
    wspec = pl.BlockSpec((k, tn), lambda i, j: (0, j))
    return pl.pallas_call(
        _swiglu_kernel,
        grid=(m // tm, n // tn),
        in_specs=[pl.BlockSpec((tm, k), lambda i, j: (i, 0)), wspec, wspec],
        out_specs=pl.BlockSpec((tm, tn), lambda i, j: (i, j)),
        out_shape=jax.ShapeDtypeStruct((m, n), BF16),
        compiler_params=_cparams(("parallel", "arbitrary")),
        name="ffn_up",
    )(x, wg, wu)


def _shift_rows(u, halo, s):
    ru = pltpu.roll(u, s, axis=0)
    rh = pltpu.roll(halo, s, axis=0)
    top = jnp.where(_iota2(halo.shape, 0) < s, rh, ru[:SUBLANE])
    if u.shape[0] == SUBLANE:
        return top
    return jnp.concatenate([top, ru[SUBLANE:]], axis=0)


def _conv_silu_strips(load, first_halo, cw_ref, store, *, tm, heads, dk, scale, normalize=None, rows=64):
    rb = min(tm, rows)
    for h in range(heads):
        cols = slice(h * dk, (h + 1) * dk)
        cw = cw_ref[:, cols]
        for r0 in range(0, tm, rb):
            u = load(r0, r0 + rb, cols)
            halo = first_halo(cols) if r0 == 0 else load(r0 - SUBLANE, r0, cols)
            prev = [_shift_rows(u, halo, sh) for sh in (3, 2, 1)]
            conv = prev[0] * cw[0:1]
            conv = conv + prev[1] * cw[1:2]
            conv = conv + prev[2] * cw[2:3]
            conv = conv + u * cw[3:4]
            s = conv * _sigmoid(conv)
            if scale is not None:
                normed = s * (lax.rsqrt(jnp.sum(s * s, axis=-1, keepdims=True) + L2_EPS) * scale)
                s = normed if normalize is None else jnp.where(normalize, normed, s)
            store(r0, r0 + rb, cols, s)


def _dn_pre_kernel(p_ref, halo_ref, cache_ref, cw_ref, qkv_ref, *, heads, dk):
    i = pl.program_id(1)
    j = pl.program_id(2)
    tm = p_ref.shape[1]

    def store(r0, r1, cols, value):
        qkv_ref[0, r0:r1, cols] = value

    def strips(scale):
        _conv_silu_strips(lambda r0, r1, cols: p_ref[0, r0:r1, cols],
                          lambda cols: jnp.where(i == 0, cache_ref[0, :, cols], halo_ref[0, :, cols]),
                          cw_ref, store, tm=tm, heads=heads, dk=dk, scale=scale)

    @pl.when(j < 2)
    def _():
        strips(jnp.where(j == 0, dk ** -0.5, 1.0).astype(F32))

    @pl.when(j == 2)
    def _():
        strips(None)


def _dn_pre(p, cache8, conv_w, *, heads, dk):
    b, t, _ = p.shape
    w = heads * dk
    tm = _pick(t, (256, 128, 64, 32, 16))
    kern = functools.partial(_dn_pre_kernel, heads=heads, dk=dk)
    return pl.pallas_call(
        kern,
        grid=(b, t // tm, 3),
        in_specs=[
            pl.BlockSpec((1, tm, w), lambda bb, i, j: (bb, i, j)),
            pl.BlockSpec((1, SUBLANE, w), lambda bb, i, j: (bb, jnp.maximum(i * (tm // SUBLANE) - 1, 0), j)),
            pl.BlockSpec((1, SUBLANE, w), lambda bb, i, j: (bb, 0, j)),
            pl.BlockSpec((4, w), lambda bb, i, j: (0, j)),
        ],
        out_specs=pl.BlockSpec((1, tm, w), lambda bb, i, j: (bb, i, j)),
        out_shape=jax.ShapeDtypeStruct((b, t, 3 * w), F32),
        compiler_params=_cparams(("parallel", "parallel", "arbitrary")),
        name="dn_pre",
    )(p, p, cache8, conv_w)


def _dn_gates_kernel(ba_ref, alog_ref, dtb_ref, gb_ref, *, heads, hb):
    x = ba_ref[0]
    lane = _iota2(x.shape, 1)
    g = -jnp.exp(alog_ref[...]) * _softplus(x + dtb_ref[...])
    full = jnp.where(lane < heads, _sigmoid(x), g)
    for hg in range(heads // hb):
        gb_ref[0, hg] = full if hg == 0 else pltpu.roll(full, LANE - hg * hb, axis=1)


def _dn_gates(p, alog_row, dtb_row, *, heads, hb, c_ba):
    b, t, _ = p.shape
    tm = _pick(t, (1024, 512, 256, 128, 64, 32, 16))
    groups = heads // hb
    return pl.pallas_call(
        functools.partial(_dn_gates_kernel, heads=heads, hb=hb),
        grid=(b, t // tm),
        in_specs=[pl.BlockSpec((1, tm, LANE), lambda bb, i: (bb, i, c_ba // LANE)),
                  pl.BlockSpec((1, LANE), lambda bb, i: (0, 0)),
                  pl.BlockSpec((1, LANE), lambda bb, i: (0, 0))],
        out_specs=pl.BlockSpec((1, groups, tm, LANE), lambda bb, i: (bb, 0, i, 0)),
        out_shape=jax.ShapeDtypeStruct((b, groups, t, LANE), F32),
        compiler_params=_cparams(("parallel", "parallel")),
        name="dn_gates",
    )(p, alog_row, dtb_row)


def _in_proj_dn_kernel(x_ref, wt_ref, cache_ref, cw_ref, p_ref, qkv_ref, raw_scr, halo_scr,
                       *, n_valid, nq, dk, tiles_per_seq):
    i = pl.program_id(0)
    j = pl.program_id(1)
    tm, tn = p_ref.shape
    jj = jnp.clip(j - 1, 0, nq - 1)
    first = (i % tiles_per_seq) == 0
    active = (j >= 1) & (j <= nq)

    @pl.when((i == 0) & (j == 0))
    def _():
        raw_scr[...] = jnp.zeros_like(raw_scr)
        halo_scr[...] = jnp.zeros_like(halo_scr)

    def store(r0, r1, cols, value):
        qkv_ref[r0:r1, cols] = value

    def matmul():
        wt = wt_ref[...]
        if n_valid % tn:
            wt = jnp.where(j * tn + _iota2(wt.shape, 0) < n_valid, wt, 0.0)
        acc = lax.dot_general(x_ref[...], wt.astype(BF16), (((1,), (1,)), ((), ())),
                              preferred_element_type=F32)
        p_ref[...] = acc
        return acc

    def fused_step(write_slot):
        read_slot = 1 - write_slot
        raw_scr[write_slot] = matmul()
        _conv_silu_strips(lambda r0, r1, cols: raw_scr[read_slot, r0:r1, cols],
                          lambda cols: jnp.where(first, cache_ref[0, :, cols], halo_scr[jj, :, cols]),
                          cw_ref, store, tm=tm, heads=tn // dk, dk=dk,
                          scale=jnp.where(jj < nq // 3, dk ** -0.5, 1.0).astype(F32),
                          normalize=jj < 2 * nq // 3)
        halo_scr[jj] = raw_scr[read_slot, tm - SUBLANE:tm, :]

    for parity in range(2):
        pl.when(active & (j % 2 == parity))(functools.partial(fused_step, parity))

    @pl.when(jnp.logical_not(active))
    def _():
        raw_scr[0] = matmul()


def _in_proj_dn(x, wt, cache8, conv_w, *, t, tm, tn, o1, dk):
    m, k = x.shape
    n = wt.shape[0]
    nt = pl.cdiv(n, tn)
    nq = o1 // tn
    tiles_per_seq = t // tm
    assert o1 % (3 * tn) == 0 and t % tm == 0 and tn % dk == 0 and nt > nq
    qcol = lambda i, j: jnp.clip(j - 1, 0, nq - 1)
    kern = functools.partial(_in_proj_dn_kernel, n_valid=n, nq=nq, dk=dk, tiles_per_seq=tiles_per_seq)
    return pl.pallas_call(
        kern,
        grid=(m // tm, nt),
        in_specs=[pl.BlockSpec((tm, k), lambda i, j: (i, 0)),
                  pl.BlockSpec((tn, k), lambda i, j: (j, 0)),
                  pl.BlockSpec((1, SUBLANE, tn), lambda i, j: (i // tiles_per_seq, 0, qcol(i, j))),
                  pl.BlockSpec((4, tn), lambda i, j: (0, qcol(i, j)))],
        out_specs=[pl.BlockSpec((tm, tn), lambda i, j: (i, j)),
                   pl.BlockSpec((tm, tn), lambda i, j: (i, qcol(i, j)))],
        out_shape=[jax.ShapeDtypeStruct((m, nt * tn), F32), jax.ShapeDtypeStruct((m, o1), F32)],
        scratch_shapes=[pltpu.VMEM((2, tm, tn), F32), pltpu.VMEM((nq, SUBLANE, tn), F32)],
        compiler_params=_cparams(("arbitrary", "arbitrary")),
        name="in_proj",
    )(x, wt, cache8, conv_w)


def _dn_kernel(q_ref, k_ref, v_ref, gb_ref, z_ref, nw_ref, s0_ref, y_ref, sout_ref, s_scr,
               *, heads, hb, dk, chunk, nch):
    c = pl.program_id(2)
    nc = pl.num_programs(2)

    @pl.when(c == 0)
    def _():
        s_scr[...] = s0_ref[0]

    row = _iota2((chunk, chunk), 0)
    col = _iota2((chunk, chunk), 1)
    causal = row >= col
    strict = row > col
    nw = nw_ref[...]
    eye = (row == col).astype(F32)
    state = {h: s_scr[h] for h in range(hb)}
    applied = {h: 0 for h in range(hb)}
    per_chunk = {}

    def chunk_gates(j):
        if j not in per_chunk:
            gbt = gb_ref[0, 0, j * chunk:(j + 1) * chunk, :]
            gc_all = _cumsum_rows(gbt)
            per_chunk[j] = (gbt, gc_all, jnp.transpose(gc_all))
        return per_chunk[j]

    def program(hs, j):
        rows = slice(j * chunk, (j + 1) * chunk)
        gbt, gc_all, gc_t = chunk_gates(j)
        sl = {h: slice(h * dk, (h + 1) * dk) for h in hs}
        q = {h: q_ref[0, rows, sl[h]] for h in hs}
        k = {h: k_ref[0, rows, sl[h]] for h in hs}
        v = {h: v_ref[0, rows, sl[h]] for h in hs}
        beta = {h: gbt[:, h:h + 1] for h in hs}
        gcol = {h: gc_all[:, heads + h:heads + h + 1] for h in hs}
        glast = {h: gc_all[chunk - 1:chunk, heads + h:heads + h + 1] for h in hs}
        decay = {h: jnp.where(causal, jnp.exp(gcol[h] - gc_t[heads + h:heads + h + 1, :]), 0.0) for h in hs}
        kb = {h: k[h] * beta[h] for h in hs}
        eg = {h: jnp.exp(gcol[h]) for h in hs}
        yield
        m = {h: jnp.where(strict, -_dot_nt(kb[h], k[h]) * decay[h], 0.0) for h in hs}
        qk = {h: jnp.where(causal, _dot_nt(q[h], k[h]) * decay[h], 0.0) for h in hs}
        yield
        t = {h: eye + m[h] for h in hs}
        m = {h: _dot(m[h], m[h]) for h in hs}
        for _ in range(chunk.bit_length() - 3):
            yield
            res = {h: _dot(jnp.concatenate([m[h], t[h]], axis=0), m[h]) for h in hs}
            m = {h: res[h][:chunk] for h in hs}
            t = {h: t[h] + res[h][chunk:] for h in hs}
        yield
        t = {h: t[h] + _dot(t[h], m[h]) for h in hs}
        yield
        sol = {h: _dot(t[h], jnp.concatenate([v[h] * beta[h], kb[h] * eg[h]], axis=1)) for h in hs}
        yield
        assert all(applied[h] == j for h in hs)
        s = {h: state[h] for h in hs}
        v_new = {h: sol[h][:, :dk] - _dot(sol[h][:, dk:], s[h]) for h in hs}
        os = {h: _dot(q[h] * eg[h], s[h]) for h in hs}
        yield
        o = {h: os[h] + _dot(qk[h], v_new[h]) for h in hs}
        for h in hs:
            state[h] = s[h] * jnp.exp(glast[h]) + _dot_tn(k[h] * jnp.exp(glast[h] - gcol[h]), v_new[h])
            applied[h] = j + 1
        yield
        for h in hs:
            z = z_ref[0, rows, sl[h]]
            y_ref[0, rows, sl[h]] = (_rms(o[h], nw) * (z * _sigmoid(z))).astype(y_ref.dtype)

    gsz = _pick(hb, (4, 2, 1))
    _run_staggered([(j * DN_CHUNK_TICKS, program(range(g0, g0 + gsz), j))
                    for j in range(nch) for g0 in range(0, hb, gsz)])
    for h in range(hb):
        s_scr[h] = state[h]

    @pl.when(c == nc - 1)
    def _():
        sout_ref[0] = s_scr[...]


def _dn_recurrence(qkv, gb, p, norm_w, s0, *, heads, hb, dk, chunk, c_z):
    b, t, _ = qkv.shape
    groups = heads // hb
    wb = hb * dk
    nqk = heads * dk // wb
    nch = _pick(t // chunk, (REC_CHUNKS, 2, 1))
    rows = nch * chunk
    kern = functools.partial(_dn_kernel, heads=heads, hb=hb, dk=dk, chunk=chunk, nch=nch)
    return pl.pallas_call(
        kern,
        grid=(b, groups, t // rows),
        in_specs=[
            pl.BlockSpec((1, rows, wb), lambda bb, g, c: (bb, c, g)),
            pl.BlockSpec((1, rows, wb), lambda bb, g, c: (bb, c, nqk + g)),
            pl.BlockSpec((1, rows, wb), lambda bb, g, c: (bb, c, 2 * nqk + g)),
            pl.BlockSpec((1, 1, rows, LANE), lambda bb, g, c: (bb, g, c, 0)),
            pl.BlockSpec((1, rows, wb), lambda bb, g, c: (bb, c, c_z // wb + g)),
            pl.BlockSpec((1, dk), lambda bb, g, c: (0, 0)),
            pl.BlockSpec((1, hb, dk, dk), lambda bb, g, c: (bb, g, 0, 0)),
        ],
        out_specs=[
            pl.BlockSpec((1, rows, wb), lambda bb, g, c: (bb, c, g)),
            pl.BlockSpec((1, hb, dk, dk), lambda bb, g, c: (bb, g, 0, 0)),
        ],
        out_shape=[jax.ShapeDtypeStruct((b, t, heads * dk), BF16),
                   jax.ShapeDtypeStruct((b, heads, dk, dk), F32)],
        scratch_shapes=[pltpu.VMEM((hb, dk, dk), F32)],
        compiler_params=_cparams(("parallel", "parallel", "arbitrary")),
        name="dn_recurrence",
    )(qkv, qkv, qkv, gb, p, norm_w.reshape(1, dk), s0)


def _pair_sums(x, m0):
    s0 = jnp.sum(jnp.where(m0, x, 0.0), axis=-1, keepdims=True)
    s1 = jnp.sum(jnp.where(m0, 0.0, x), axis=-1, keepdims=True)
    return jnp.where(m0, s0, s1)


def _rw_pre_kernel(*refs, nblk, off, rw, xw_sl, xa_sl, xg_sl):
    main, halos = refs[:nblk], refs[nblk:2 * nblk]
    (cache_ref, mu_ref, w0_ref, w2_ref, a0_ref, a2_ref, g2_ref, kk_ref, ka_ref,
     r_ref, k_ref, v_ref, lw_ref, av_ref, bv_ref, gate_ref) = refs[2 * nblk:]
    u = jnp.concatenate([ref[0] for ref in main], axis=1)
    halo = jnp.concatenate([ref[0] for ref in halos], axis=1)
    halo = jnp.where(pl.program_id(1) == 0, cache_ref[0], halo)
    x = u + mu_ref[...] * (_shift_rows(u, halo, 1) - u)
    x = pltpu.roll(x, x.shape[1] - off, axis=1)
    r = x[:, :rw]
    kr = x[:, rw:2 * rw]
    vr = x[:, 2 * rw:3 * rw]
    z = w0_ref[...] + _dot(jnp.tanh(x[:, xw_sl[0]:xw_sl[1]]), w2_ref[...])
    a = _sigmoid(a0_ref[...] + _dot(x[:, xa_sl[0]:xa_sl[1]], a2_ref[...]))
    gate_ref[0] = _dot(_sigmoid(x[:, xg_sl[0]:xg_sl[1]]), g2_ref[...]).astype(gate_ref.dtype)
    r_ref[0] = r.astype(r_ref.dtype)
    v_ref[0] = vr.astype(v_ref.dtype)
    lw_ref[0] = -math.exp(-0.5) * _sigmoid(z)
    k_ref[0] = (kr * (1.0 + (a - 1.0) * ka_ref[...])).astype(k_ref.dtype)
    kkr = kr * kk_ref[...]
    m0 = _iota2((1, LANE), 1) < LANE // 2
    for jb in range(rw // LANE):
        sl = slice(jb * LANE, (jb + 1) * LANE)
        blk = kkr[:, sl]
        kk = blk * lax.rsqrt(_pair_sums(blk * blk, m0) + L2_EPS)
        av_ref[0, :, sl] = (-kk).astype(av_ref.dtype)
        bv_ref[0, :, sl] = (kk * a[:, sl]).astype(bv_ref.dtype)


def _rw_pre(p, cache8, pp):
    b, t, _ = p.shape
    rw, win0, width, bw = pp["rw"], pp["win0"], pp["win_w"], pp["win_bw"]
    nblk = width // bw
    tm = _pick(t, (128, 64, 32, 16))
    kern = functools.partial(_rw_pre_kernel, nblk=nblk, off=pp["win_off"], rw=rw,
                             xw_sl=pp["xw_sl"], xa_sl=pp["xa_sl"], xg_sl=pp["xg_sl"])
    full = lambda shape: pl.BlockSpec(shape, lambda bb, i: (0,) * len(shape))
    row = pl.BlockSpec((1, tm, rw), lambda bb, i: (bb, i, 0))
    cb = [win0 // bw + n for n in range(nblk)]
    main = [pl.BlockSpec((1, tm, bw), lambda bb, i, c=c: (bb, i, c)) for c in cb]
    halos = [pl.BlockSpec((1, SUBLANE, bw),
                          lambda bb, i, c=c: (bb, jnp.maximum(i * (tm // SUBLANE) - 1, 0), c)) for c in cb]
    small = [pp["mu_win"], pp["w0"], pp["w2p"], pp["a0"], pp["a2p"], pp["g2p"], pp["k_k"], pp["k_a"]]
    return pl.pallas_call(
        kern,
        grid=(b, t // tm),
        in_specs=main + halos + [pl.BlockSpec((1, SUBLANE, width), lambda bb, i: (bb, 0, 0))]
        + [full(a.shape) for a in small],
        out_specs=[row] * 7,
        out_shape=[jax.ShapeDtypeStruct((b, t, rw), F32 if n == 3 else RW_STORE) for n in range(7)],
        compiler_params=_cparams(("parallel", "arbitrary")),
        name="rw_pre",
    )(*([p] * (2 * nblk)), cache8, *small)


def _rw_kernel(r_ref, k_ref, v_ref, lw_ref, av_ref, bv_ref, gate_ref, rk_ref, lnw_ref, lnb_ref, s0_ref,
               y_ref, sout_ref, s_scr, *, pb, chunk, nch):
    c = pl.program_id(2)
    nc = pl.num_programs(2)
    hn = LANE // 2

    @pl.when(c == 0)
    def _():
        s_scr[...] = s0_ref[0]

    c2 = 2 * chunk
    m0 = _iota2((1, LANE), 1) < hn
    m1 = jnp.logical_not(m0)
    blockmask = (_iota2((LANE, LANE), 0) < hn) == (_iota2((LANE, LANE), 1) < hn)
    row2 = _iota2((chunk, c2), 0)
    col2 = _iota2((chunk, c2), 1) & (chunk - 1)
    strict2 = row2 > col2
    eye2 = (row2 == col2).astype(F32)
    incl4 = _iota2((chunk, 2 * c2), 0) >= (_iota2((chunk, 2 * c2), 1) & (chunk - 1))

    def by_head(x):
        return jnp.concatenate([jnp.where(m0, x, 0.0), jnp.where(m1, x, 0.0)], axis=0)

    bd = (_iota2((c2, c2), 0) < chunk) == (_iota2((c2, c2), 1) < chunk)

    def blockdiag(p2):
        return jnp.where(bd, jnp.concatenate([p2, p2], axis=0), 0.0)

    state = {p: s_scr[p] for p in range(pb)}
    applied = {p: 0 for p in range(pb)}

    def program(ps, j):
        rows = slice(j * chunk, (j + 1) * chunk)
        lanes = slice(ps[0] * LANE, (ps[-1] + 1) * LANE)
        sl = {p: slice(p * LANE, (p + 1) * LANE) for p in ps}
        loc = {p: slice((p - ps[0]) * LANE, (p - ps[0] + 1) * LANE) for p in ps}
        r = {p: r_ref[0, rows, sl[p]].astype(F32) for p in ps}
        k = {p: k_ref[0, rows, sl[p]].astype(F32) for p in ps}
        v = {p: v_ref[0, rows, sl[p]].astype(F32) for p in ps}
        lw = {p: lw_ref[0, rows, sl[p]] for p in ps}
        bv = {p: bv_ref[0, rows, sl[p]].astype(F32) for p in ps}
        cw_all = _cumsum_rows(lw_ref[0, rows, lanes])
        yield
        cw = {p: cw_all[:, loc[p]] for p in ps}
        tot = {p: cw[p][chunk - 1:chunk, :] for p in ps}
        e_neg = {p: jnp.exp(-cw[p]) for p in ps}
        e_end = {p: jnp.exp(tot[p] - cw[p]) for p in ps}
        lhs = {p: jnp.concatenate([av_ref[0, rows, sl[p]].astype(F32) * jnp.exp(cw[p] - lw[p]), r[p] * jnp.exp(cw[p])],
                                  axis=0) for p in ps}
        rhs_g = {p: jnp.concatenate([by_head(bv[p] * e_neg[p]), by_head(k[p] * e_neg[p])], axis=0) for p in ps}
        rhs_s = {p: jnp.concatenate([bv[p] * e_end[p], k[p] * e_end[p]], axis=0) for p in ps}
        vh = {p: by_head(v[p]) for p in ps}
        g = {p: _dot_nt(lhs[p], rhs_g[p]) for p in ps}
        yield
        avs = {p: _dot(jnp.where(strict2, g[p][:chunk, c2:], 0.0), vh[p]) for p in ps}
        m = {p: jnp.where(strict2, g[p][:chunk, :c2], 0.0) for p in ps}
        t2 = {p: eye2 + m[p] for p in ps}
        m = {p: _dot(m[p], blockdiag(m[p])) for p in ps}
        for _ in range(chunk.bit_length() - 3):
            yield
            res = {p: _dot(jnp.concatenate([m[p], t2[p]], axis=0), blockdiag(m[p])) for p in ps}
            m = {p: res[p][:chunk] for p in ps}
            t2 = {p: t2[p] + res[p][chunk:] for p in ps}
        yield
        t2 = {p: t2[p] + _dot(t2[p], blockdiag(m[p])) for p in ps}
        assert all(applied[p] == j for p in ps)
        s = {p: state[p] for p in ps}
        sr = {p: _dot_nt(lhs[p], s[p]) for p in ps}
        yield
        u = {p: _dot(t2[p], by_head(sr[p][:chunk] + avs[p])) for p in ps}
        yield
        yr = {p: _dot(jnp.where(incl4, g[p][chunk:], 0.0), jnp.concatenate([by_head(u[p]), vh[p]], axis=0))
              for p in ps}
        for p in ps:
            state[p] = jnp.where(blockmask, s[p] * jnp.exp(tot[p])
                                 + _dot_tn(jnp.concatenate([u[p], v[p]], axis=0), rhs_s[p]), 0.0)
            applied[p] = j + 1
        yield
        for p in ps:
            y = sr[p][chunk:] + yr[p]
            mean = _pair_sums(y, m0) * (1.0 / hn)
            d = y - mean
            var = _pair_sums(d * d, m0) * (1.0 / hn)
            yn = d * lax.rsqrt(var + RW_GN_EPS) * lnw_ref[:, sl[p]] + lnb_ref[:, sl[p]]
            bonus = _pair_sums(r[p] * k[p] * rk_ref[:, sl[p]], m0) * v[p]
            y_ref[0, rows, sl[p]] = ((yn + bonus) * gate_ref[0, rows, sl[p]].astype(F32)).astype(y_ref.dtype)

    gsz = _pick(pb, (4, 2, 1))
    _run_staggered([(j * REC_CHUNK_TICKS, program(range(g0, g0 + gsz), j))
                    for j in range(nch) for g0 in range(0, pb, gsz)])
    for p in range(pb):
        s_scr[p] = state[p]

    @pl.when(c == nc - 1)
    def _():
        sout_ref[0] = s_scr[...]


def _rw_recurrence(r, k, v, lw, av, bv, gate, r_k, ln_w, ln_b, s0p, *, pb, chunk):
    b, t, rw = r.shape
    pairs = rw // LANE
    groups = pairs // pb
    wb = pb * LANE
    nch = _pick(t // chunk, (REC_CHUNKS, 2, 1))
    kern = functools.partial(_rw_kernel, pb=pb, chunk=chunk, nch=nch)
    tile = pl.BlockSpec((1, nch * chunk, wb), lambda bb, g, c: (bb, c, g))
    par = pl.BlockSpec((1, wb), lambda bb, g, c: (0, g))
    st = pl.BlockSpec((1, pb, LANE, LANE), lambda bb, g, c: (bb, g, 0, 0))
    return pl.pallas_call(
        kern,
        grid=(b, groups, t // (nch * chunk)),
        in_specs=[tile] * 7 + [par] * 3 + [st],
        out_specs=[tile, st],
        out_shape=[jax.ShapeDtypeStruct((b, t, rw), BF16),
                   jax.ShapeDtypeStruct((b, pairs, LANE, LANE), F32)],
        scratch_shapes=[pltpu.VMEM((pb, LANE, LANE), F32)],
        compiler_params=_cparams(("parallel", "parallel", "arbitrary")),
        name="rw_recurrence",
    )(r, k, v, lw, av, bv, gate, r_k, ln_w, ln_b, s0p)


def _prepare(w):
    heads = w["dn_a_log"].shape[-1]
    dk = w["dn_norm_w"].shape[-1]
    qkv_w = w["dn_conv_w"].shape[-1]
    v_w = heads * dk
    assert qkv_w == 3 * v_w and dk == LANE
    rw_heads, rw_head = w["rw_r_k"].shape
    assert rw_head == LANE // 2
    rw = rw_heads * rw_head
    lw_n, la_n, lg_n = w["rw_w2"].shape[0], w["rw_a2"].shape[0], w["rw_g2"].shape[0]
    o1 = qkv_w
    o2 = o1 + v_w
    o4 = o2 + 2 * heads
    shift_w = 3 * rw + lw_n + la_n + lg_n
    assert o2 % LANE == 0 and 2 * heads <= LANE and w["w_in"].shape[1] == o4 + shift_w
    win0 = o4 // LANE * LANE
    win_off = o4 - win0
    win_w = _round_up(win_off + shift_w, LANE)
    win_bw = math.gcd(math.gcd(win0, win_w), 8 * LANE)
    assert win0 + win_w <= _round_up(o4 + shift_w, IN_TN)

    def lora_block(start, n, weight):
        lo, hi = start // LANE * LANE, _round_up(start + n, LANE)
        padded = jnp.pad(weight, ((start - lo, hi - start - n), (0, 0))).astype(BF16)
        return (lo, hi), padded

    xw_sl, w2p = lora_block(3 * rw, lw_n, w["rw_w2"])
    xa_sl, a2p = lora_block(3 * rw + lw_n, la_n, w["rw_a2"])
    xg_sl, g2p = lora_block(3 * rw + lw_n + la_n, lg_n, w["rw_g2"])
    lane_pad = lambda a: jnp.pad(a, (heads, LANE - 2 * heads)).reshape(1, LANE)
    in_window = lambda a: jnp.pad(a, [(0, 0)] * (a.ndim - 1) + [(win_off, win_w - win_off - shift_w)])
    return dict(
        heads=heads, dk=dk, rw=rw, rw_heads=rw_heads, o1=o1, o2=o2, o4=o4, shift_w=shift_w,
        win0=win0, win_off=win_off, win_w=win_w, win_bw=win_bw, in_window=in_window,
        xw_sl=xw_sl, xa_sl=xa_sl, xg_sl=xg_sl, w2p=w2p, a2p=a2p, g2p=g2p,
        alog_row=lane_pad(w["dn_a_log"]), dtb_row=lane_pad(w["dn_dt_bias"]),
        mu_win=in_window(w["rw_mu"].reshape(1, shift_w)),
        w0=w["rw_w0"].reshape(1, rw), a0=w["rw_a0"].reshape(1, rw),
        k_k=w["rw_k_k"].reshape(1, rw), k_a=w["rw_k_a"].reshape(1, rw),
        w_in_t=jnp.swapaxes(w["w_in"], 0, 1),
        w_down=w["w_down"].astype(BF16),
    )


def _layer(x, dn_state, dn_conv, rw_state, rw_shift, w, pp):
    b, t, d = x.shape
    m = b * t
    heads, dk, rw, rw_heads = pp["heads"], pp["dk"], pp["rw"], pp["rw_heads"]
    o1, o2, o4, shift_w = pp["o1"], pp["o2"], pp["o4"], pp["shift_w"]
    chunk = 64 if t % 64 == 0 else t
    assert chunk & (chunk - 1) == 0 and chunk >= 2 * SUBLANE
    hb = 16 if heads % 16 == 0 else heads
    pairs = rw // LANE
    pb = 16 if pairs % 16 == 0 else pairs
    tm = _pick(m, (1024, 512, 256, 128))

    xf = x.reshape(m, d)
    h = _rms_bf16(xf, w["g_mix_pre"])
    cache8 = jnp.pad(dn_conv.astype(F32), ((0, 0), (SUBLANE - dn_conv.shape[1], 0), (0, 0)))
    tm_in = min(tm, IN_TM_MAX)
    if t % tm_in == 0 and o1 % (3 * IN_TN) == 0:
        p, qkv = _in_proj_dn(h, pp["w_in_t"], cache8, w["dn_conv_w"], t=t, tm=tm_in, tn=IN_TN, o1=o1, dk=dk)
        p = p.reshape(b, t, p.shape[1])
        qkv = qkv.reshape(b, t, o1)
    else:
        p = _matmul_w32t(h, pp["w_in_t"], tm=tm, tn=IN_TN, name="in_proj")
        p = p.reshape(b, t, p.shape[1])
        qkv = _dn_pre(p, cache8, w["dn_conv_w"], heads=heads, dk=dk)

    gb = _dn_gates(p, pp["alog_row"], pp["dtb_row"], heads=heads, hb=hb, c_ba=o2)
    y_a, new_dn_state = _dn_recurrence(qkv, gb, p, w["dn_norm_w"], dn_state.astype(F32),
                                       heads=heads, hb=hb, dk=dk, chunk=chunk, c_z=o1)
    new_dn_conv = p[:, t - dn_conv.shape[1]:, :o1]

    shift8 = jnp.pad(pp["in_window"](rw_shift.astype(F32)), ((0, 0), (SUBLANE - 1, 0), (0, 0)))
    r, k, v, lw, av, bv, gate = _rw_pre(p, shift8, pp)
    hn = LANE // 2
    s4 = rw_state.astype(F32).reshape(b, pairs, 2, hn, hn)
    zeros = jnp.zeros_like(s4[:, :, 0])
    s0p = jnp.concatenate([jnp.concatenate([s4[:, :, 0], zeros], axis=-1),
                           jnp.concatenate([zeros, s4[:, :, 1]], axis=-1)], axis=-2)
    y_b, sp = _rw_recurrence(r, k, v, lw, av, bv, gate, w["rw_r_k"].reshape(1, rw),
                             w["rw_ln_w"].reshape(1, rw), w["rw_ln_b"].reshape(1, rw), s0p,
                             pb=pb, chunk=chunk)
    new_rw_state = jnp.stack([sp[:, :, :hn, :hn], sp[:, :, hn:, hn:]], axis=2).reshape(b, rw_heads, hn, hn)
    new_rw_shift = p[:, t - 1:, o4:o4 + shift_w]

    mixo = _matmul2(y_a.reshape(m, heads * dk), y_b.reshape(m, rw), w["w_out"],
                    tm=tm, tn=_pick(d, (512, 256, 128)), name="out_proj")
    x1, h2 = _resid_rms(xf, mixo, w["g_mix_post"], w["g_ffn_pre"])
    dff = w["w_gate"].shape[1]
    f = _swiglu_up(h2, w["w_gate"], w["w_up"], tm=_pick(m, (2048, 1024, 512, 256, 128)),
                   tn=_pick(dff, (256, 128)))
    fo = _matmul(f, pp["w_down"], tm=_pick(m, (512, 256, 128)), tn=_pick(d, (512, 256, 128)),
                 out_dtype=BF16, name="ffn_down")
    out = _resid_final(x1, fo, w["g_ffn_post"]).reshape(b, t, d)
    return out, (new_dn_state, new_dn_conv, new_rw_state, new_rw_shift)


_WEIGHT_NAMES = ("g_mix_pre", "g_mix_post", "w_in", "dn_conv_w", "dn_a_log", "dn_dt_bias", "dn_norm_w",
                 "rw_mu", "rw_w0", "rw_w2", "rw_a0", "rw_a2", "rw_g2", "rw_k_k", "rw_k_a", "rw_r_k",
                 "rw_ln_w", "rw_ln_b", "w_out", "g_ffn_pre", "g_ffn_post", "w_gate", "w_up", "w_down")


def kernel(x_prompt, x_sample, state_dn, cache_dn_conv, state_rwkv, cache_rwkv_shift,
           g_mix_pre, g_mix_post, w_in, dn_conv_w, dn_a_log, dn_dt_bias, dn_norm_w,
           rw_mu, rw_w0, rw_w2, rw_a0, rw_a2, rw_g2, rw_k_k, rw_k_a, rw_r_k, rw_ln_w, rw_ln_b,
           w_out, g_ffn_pre, g_ffn_post, w_gate, w_up, w_down):
    stacked = (g_mix_pre, g_mix_post, w_in, dn_conv_w, dn_a_log, dn_dt_bias, dn_norm_w,
               rw_mu, rw_w0, rw_w2, rw_a0, rw_a2, rw_g2, rw_k_k, rw_k_a, rw_r_k, rw_ln_w, rw_ln_b,
               w_out, g_ffn_pre, g_ffn_post, w_gate, w_up, w_down)
    depth = w_in.shape[0]
    bp = x_prompt.shape[0]
    dt = x_prompt.dtype
    yp, ys = x_prompt, x_sample
    outs_p, outs_s = [], []
    for l in range(depth):
        w = {n: a[l] for n, a in zip(_WEIGHT_NAMES, stacked)}
        pp = _prepare(w)
        heads, dk, rw_heads = pp["heads"], pp["dk"], pp["rw_heads"]
        hn = LANE // 2
        yp, st_p = _layer(yp,
                          jnp.zeros((bp, heads, dk, dk), dt),
                          jnp.zeros((bp, cache_dn_conv.shape[2], cache_dn_conv.shape[3]), dt),
                          jnp.zeros((bp, rw_heads, hn, hn), dt),
                          jnp.zeros((bp, 1, cache_rwkv_shift.shape[3]), dt), w, pp)
        ys, st_s = _layer(ys, state_dn[l], cache_dn_conv[l], state_rwkv[l], cache_rwkv_shift[l], w, pp)
        outs_p.append(st_p)
        outs_s.append(st_s)
    stack = lambda outs, i: jnp.stack([o[i] for o in outs])
    return (yp, ys,
            stack(outs_p, 0), stack(outs_p, 1), stack(outs_p, 2), stack(outs_p, 3),
            stack(outs_s, 0), stack(outs_s, 1), stack(outs_s, 2), stack(outs_s, 3))
```

```python
import functools
import math

import jax
import jax.numpy as jnp
from jax import lax
from jax.experimental import pallas as pl
from jax.experimental.pallas import tpu as pltpu

F32 = jnp.float32
BF16 = jnp.bfloat16

LANE = 128
SUBLANE = 8
VMEM_LIMIT_BYTES = 56 * 2**20
NORM_EPS = 1e-6
L2_EPS = 1e-6
RW_GN_EPS = 64e-5
IN_TN = 512
IN_TM_MAX = 1024
FFN_ROW_BLOCK = 512
RW_STORE = BF16
REC_CHUNKS = 4
REC_CHUNK_TICKS = 3
DN_CHUNK_TICKS = 2


def _cparams(sem):
    return pltpu.CompilerParams(dimension_semantics=sem, vmem_limit_bytes=VMEM_LIMIT_BYTES)


def _round_up(x, m):
    return -(-x // m) * m


def _pick(n, cands):
    for c in cands:
        if n % c == 0:
            return c
    return n


def _dot(a, b):
    return jnp.dot(a.astype(BF16), b.astype(BF16), preferred_element_type=F32)


def _dot_nt(a, b):
    return lax.dot_general(a.astype(BF16), b.astype(BF16), (((1,), (1,)), ((), ())),
                           preferred_element_type=F32)


def _dot_tn(a, b):
    return lax.dot_general(a.astype(BF16), b.astype(BF16), (((0,), (0,)), ((), ())),
                           preferred_element_type=F32)


def _cumsum_rows(x):
    c = x.shape[0]
    tri = (_iota2((c, c), 0) >= _iota2((c, c), 1)).astype(BF16)
    hi = x.astype(BF16)
    r1 = x - hi.astype(F32)
    mid = r1.astype(BF16)
    lo = (r1 - mid.astype(F32)).astype(BF16)
    dot = lambda p: jnp.dot(tri, p, preferred_element_type=F32)
    return dot(hi) + (dot(mid) + dot(lo))


def _sigmoid(x):
    return 0.5 * jnp.tanh(0.5 * x) + 0.5


def _softplus(x):
    return jnp.maximum(x, 0.0) + jnp.log(1.0 + jnp.exp(-jnp.abs(x)))


def _iota2(shape, dim):
    return lax.broadcasted_iota(jnp.int32, shape, dim)


def _run_staggered(programs):
    live = list(programs)
    tick = 0
    while live:
        still = []
        for start, prog in live:
            if tick >= start:
                try:
                    next(prog)
                except StopIteration:
                    continue
            still.append((start, prog))
        live = still
        tick += 1


def _rms(x, g):
    return x * lax.rsqrt(jnp.mean(x * x, axis=-1, keepdims=True) + NORM_EPS) * g


def _rms_kernel(x_ref, g_ref, o_ref):
    o_ref[...] = _rms(x_ref[...], g_ref[...]).astype(o_ref.dtype)


def _rms_bf16(x, g):
    m, d = x.shape
    tm = _pick(m, (256, 128))
    return pl.pallas_call(
        _rms_kernel,
        grid=(m // tm,),
        in_specs=[pl.BlockSpec((tm, d), lambda i: (i, 0)), pl.BlockSpec((1, d), lambda i: (0, 0))],
        out_specs=pl.BlockSpec((tm, d), lambda i: (i, 0)),
        out_shape=jax.ShapeDtypeStruct((m, d), BF16),
        compiler_params=_cparams(("parallel",)),
        name="rms_pre",
    )(x, g.reshape(1, d))


def _mix_residual(x_ref, y_ref, gp_ref):
    return x_ref[...] + _rms(y_ref[...].astype(F32), gp_ref[...])


def _resid_rms_kernel(x_ref, y_ref, gp_ref, gn_ref, h_ref):
    h_ref[...] = _rms(_mix_residual(x_ref, y_ref, gp_ref), gn_ref[...]).astype(h_ref.dtype)


def _resid_rms(x, y, g_post, g_next):
    m, d = x.shape
    tm = _pick(m, (256, 128))
    row = pl.BlockSpec((tm, d), lambda i: (i, 0))
    par = pl.BlockSpec((1, d), lambda i: (0, 0))
    return pl.pallas_call(
        _resid_rms_kernel,
        grid=(m // tm,),
        in_specs=[row, row, par, par],
        out_specs=row,
        out_shape=jax.ShapeDtypeStruct((m, d), BF16),
        compiler_params=_cparams(("parallel",)),
        name="resid_rms",
    )(x, y, g_post.reshape(1, d), g_next.reshape(1, d))


def _resid_final_kernel(x_ref, y_ref, gp_ref, f_ref, gf_ref, o_ref):
    o_ref[...] = _mix_residual(x_ref, y_ref, gp_ref) + _rms(f_ref[...].astype(F32), gf_ref[...])


def _resid_final(x, y, g_post, f, g_ffn):
    m, d = x.shape
    tm = _pick(m, (256, 128))
    row = pl.BlockSpec((tm, d), lambda i: (i, 0))
    par = pl.BlockSpec((1, d), lambda i: (0, 0))
    return pl.pallas_call(
        _resid_final_kernel,
        grid=(m // tm,),
        in_specs=[row, row, par, row, par],
        out_specs=row,
        out_shape=jax.ShapeDtypeStruct((m, d), F32),
        compiler_params=_cparams(("parallel",)),
        name="resid_final",
    )(x, y, g_post.reshape(1, d), f, g_ffn.reshape(1, d))


def _mm_kernel(x_ref, w_ref, o_ref):
    o_ref[...] = jnp.dot(x_ref[...], w_ref[...], preferred_element_type=F32).astype(o_ref.dtype)


def _matmul(x, w, *, tm, tn, out_dtype, name):
    m, k = x.shape
    n = w.shape[1]
    return pl.pallas_call(
        _mm_kernel,
        grid=(m // tm, n // tn),
        in_specs=[pl.BlockSpec((tm, k), lambda i, j: (i, 0)), pl.BlockSpec((k, tn), lambda i, j: (0, j))],
        out_specs=pl.BlockSpec((tm, tn), lambda i, j: (i, j)),
        out_shape=jax.ShapeDtypeStruct((m, n), out_dtype),
        compiler_params=_cparams(("parallel", "arbitrary")),
        name=name,
    )(x, w)


def _mm_w32t_kernel(x_ref, wt_ref, o_ref, *, n_valid):
    wt = wt_ref[...]
    tn = wt.shape[0]
    if n_valid % tn:
        wt = jnp.where(pl.program_id(1) * tn + _iota2(wt.shape, 0) < n_valid, wt, 0.0)
    o_ref[...] = lax.dot_general(x_ref[...], wt.astype(BF16), (((1,), (1,)), ((), ())),
                                 preferred_element_type=F32)


def _matmul_w32t(x, wt, *, tm, tn, name):
    m, k = x.shape
    n = wt.shape[0]
    nt = pl.cdiv(n, tn)
    return pl.pallas_call(
        functools.partial(_mm_w32t_kernel, n_valid=n),
        grid=(m // tm, nt),
        in_specs=[pl.BlockSpec((tm, k), lambda i, j: (i, 0)), pl.BlockSpec((tn, k), lambda i, j: (j, 0))],
        out_specs=pl.BlockSpec((tm, tn), lambda i, j: (i, j)),
        out_shape=jax.ShapeDtypeStruct((m, nt * tn), F32),
        compiler_params=_cparams(("parallel", "arbitrary")),
        name=name,
    )(x, wt)


def _mm2_kernel(xa_ref, xb_ref, wa_ref, wb_ref, o_ref):
    acc = jnp.dot(xa_ref[...], wa_ref[...].astype(BF16), preferred_element_type=F32)
    acc = acc + jnp.dot(xb_ref[...], wb_ref[...].astype(BF16), preferred_element_type=F32)
    o_ref[...] = acc.astype(o_ref.dtype)


def _matmul2(xa, xb, w, *, tm, tn, name):
    m, ka = xa.shape
    kb = xb.shape[1]
    n = w.shape[1]
    assert ka == kb and w.shape[0] == ka + kb
    return pl.pallas_call(
        _mm2_kernel,
        grid=(m // tm, n // tn),
        in_specs=[pl.BlockSpec((tm, ka), lambda i, j: (i, 0)), pl.BlockSpec((tm, kb), lambda i, j: (i, 0)),
                  pl.BlockSpec((ka, tn), lambda i, j: (0, j)), pl.BlockSpec((kb, tn), lambda i, j: (1, j))],
        out_specs=pl.BlockSpec((tm, tn), lambda i, j: (i, j)),
        out_shape=jax.ShapeDtypeStruct((m, n), BF16),
        compiler_params=_cparams(("parallel", "arbitrary")),
        name=name,
    )(xa, xb, w, w)


def _swiglu_kernel(x_ref, wg_ref, wu_ref, o_ref):
    wg = wg_ref[...].astype(BF16)
    wu = wu_ref[...].astype(BF16)
    tm = x_ref.shape[0]
    rb = min(tm, FFN_ROW_BLOCK)
    for r0 in range(0, tm, rb):
        x = x_ref[r0:r0 + rb, :]
        g = jnp.dot(x, wg, preferred_element_type=F32)
        u = jnp.dot(x, wu, preferred_element_type=F32)
        o_ref[r0:r0 + rb, :] = (g * _sigmoid(g) * u).astype(o_ref.dtype)


def _swiglu_up(x, wg, wu, *, tm, tn):
    m, k = x.shape
    n = wg.shape[1]
    wspec = pl.BlockSpec((k, tn), lambda i, j: (0, j))
    return pl.pallas_call(
        _swiglu_kernel,
        grid=(m // tm, n // tn),
        in_specs=[pl.BlockSpec((tm, k), lambda i, j: (i, 0)), wspec, wspec],
        out_specs=pl.BlockSpec((tm, tn), lambda i, j: (i, j)),
        out_shape=jax.ShapeDtypeStruct((m, n), BF16),
        compiler_params=_cparams(("parallel", "arbitrary")),
        name="ffn_up",
    )(x, wg, wu)


def _shift_rows(u, halo, s):
    ru = pltpu.roll(u, s, axis=0)
    rh = pltpu.roll(halo, s, axis=0)
    top = jnp.where(_iota2(halo.shape, 0) < s, rh, ru[:SUBLANE])
    if u.shape[0] == SUBLANE:
        return top
    return jnp.concatenate([top, ru[SUBLANE:]], axis=0)


def _conv_silu_strips(load, first_halo, cw_ref, store, *, tm, heads, dk, scale, normalize=None, rows=64):
    rb = min(tm, rows)
    for h in range(heads):
        cols = slice(h * dk, (h + 1) * dk)
        cw = cw_ref[:, cols]
        for r0 in range(0, tm, rb):
            u = load(r0, r0 + rb, cols)
            halo = first_halo(cols) if r0 == 0 else load(r0 - SUBLANE, r0, cols)
            prev = [_shift_rows(u, halo, sh) for sh in (3, 2, 1)]
            conv = prev[0] * cw[0:1]
            conv = conv + prev[1] * cw[1:2]
            conv = conv + prev[2] * cw[2:3]
            conv = conv + u * cw[3:4]
            s = conv * _sigmoid(conv)
            if scale is not None:
                normed = s * (lax.rsqrt(jnp.sum(s * s, axis=-1, keepdims=True) + L2_EPS) * scale)
                s = normed if normalize is None else jnp.where(normalize, normed, s)
            store(r0, r0 + rb, cols, s)


def _dn_pre_kernel(p_ref, halo_ref, cache_ref, cw_ref, qkv_ref, *, heads, dk):
    i = pl.program_id(1)
    j = pl.program_id(2)
    tm = p_ref.shape[1]

    def store(r0, r1, cols, value):
        qkv_ref[0, r0:r1, cols] = value

    def strips(scale):
        _conv_silu_strips(lambda r0, r1, cols: p_ref[0, r0:r1, cols],
                          lambda cols: jnp.where(i == 0, cache_ref[0, :, cols], halo_ref[0, :, cols]),
                          cw_ref, store, tm=tm, heads=heads, dk=dk, scale=scale)

    @pl.when(j < 2)
    def _():
        strips(jnp.where(j == 0, dk ** -0.5, 1.0).astype(F32))

    @pl.when(j == 2)
    def _():
        strips(None)


def _dn_pre(p, cache8, conv_w, *, heads, dk):
    b, t, _ = p.shape
    w = heads * dk
    tm = _pick(t, (256, 128, 64, 32, 16))
    kern = functools.partial(_dn_pre_kernel, heads=heads, dk=dk)
    return pl.pallas_call(
        kern,
        grid=(b, t // tm, 3),
        in_specs=[
            pl.BlockSpec((1, tm, w), lambda bb, i, j: (bb, i, j)),
            pl.BlockSpec((1, SUBLANE, w), lambda bb, i, j: (bb, jnp.maximum(i * (tm // SUBLANE) - 1, 0), j)),
            pl.BlockSpec((1, SUBLANE, w), lambda bb, i, j: (bb, 0, j)),
            pl.BlockSpec((4, w), lambda bb, i, j: (0, j)),
        ],
        out_specs=pl.BlockSpec((1, tm, w), lambda bb, i, j: (bb, i, j)),
        out_shape=jax.ShapeDtypeStruct((b, t, 3 * w), F32),
        compiler_params=_cparams(("parallel", "parallel", "arbitrary")),
        name="dn_pre",
    )(p, p, cache8, conv_w)


def _dn_gates_kernel(ba_ref, alog_ref, dtb_ref, gb_ref, *, heads, hb):
    x = ba_ref[0]
    lane = _iota2(x.shape, 1)
    g = -jnp.exp(alog_ref[...]) * _softplus(x + dtb_ref[...])
    full = jnp.where(lane < heads, _sigmoid(x), g)
    for hg in range(heads // hb):
        gb_ref[0, hg] = full if hg == 0 else pltpu.roll(full, LANE - hg * hb, axis=1)


def _dn_gates(p, alog_row, dtb_row, *, heads, hb, c_ba):
    b, t, _ = p.shape
    tm = _pick(t, (1024, 512, 256, 128, 64, 32, 16))
    groups = heads // hb
    return pl.pallas_call(
        functools.partial(_dn_gates_kernel, heads=heads, hb=hb),
        grid=(b, t // tm),
        in_specs=[pl.BlockSpec((1, tm, LANE), lambda bb, i: (bb, i, c_ba // LANE)),
                  pl.BlockSpec((1, LANE), lambda bb, i: (0, 0)),
                  pl.BlockSpec((1, LANE), lambda bb, i: (0, 0))],
        out_specs=pl.BlockSpec((1, groups, tm, LANE), lambda bb, i: (bb, 0, i, 0)),
        out_shape=jax.ShapeDtypeStruct((b, groups, t, LANE), F32),
        compiler_params=_cparams(("parallel", "parallel")),
        name="dn_gates",
    )(p, alog_row, dtb_row)


def _in_proj_dn_kernel(x_ref, wt_ref, cache_ref, cw_ref, p_ref, qkv_ref, raw_scr, halo_scr,
                       *, n_valid, nq, dk, tiles_per_seq):
    i = pl.program_id(0)
    j = pl.program_id(1)
    tm, tn = p_ref.shape
    jj = jnp.clip(j - 1, 0, nq - 1)
    first = (i % tiles_per_seq) == 0
    active = (j >= 1) & (j <= nq)

    @pl.when((i == 0) & (j == 0))
    def _():
        raw_scr[...] = jnp.zeros_like(raw_scr)
        halo_scr[...] = jnp.zeros_like(halo_scr)

    def store(r0, r1, cols, value):
        qkv_ref[r0:r1, cols] = value

    def matmul():
        wt = wt_ref[...]
        if n_valid % tn:
            wt = jnp.where(j * tn + _iota2(wt.shape, 0) < n_valid, wt, 0.0)
        acc = lax.dot_general(x_ref[...], wt.astype(BF16), (((1,), (1,)), ((), ())),
                              preferred_element_type=F32)
        p_ref[...] = acc
        return acc

    def fused_step(write_slot):
        read_slot = 1 - write_slot
        raw_scr[write_slot] = matmul()
        _conv_silu_strips(lambda r0, r1, cols: raw_scr[read_slot, r0:r1, cols],
                          lambda cols: jnp.where(first, cache_ref[0, :, cols], halo_scr[jj, :, cols]),
                          cw_ref, store, tm=tm, heads=tn // dk, dk=dk,
                          scale=jnp.where(jj < nq // 3, dk ** -0.5, 1.0).astype(F32),
                          normalize=jj < 2 * nq // 3)
        halo_scr[jj] = raw_scr[read_slot, tm - SUBLANE:tm, :]

    for parity in range(2):
        pl.when(active & (j % 2 == parity))(functools.partial(fused_step, parity))

    @pl.when(jnp.logical_not(active))
    def _():
        raw_scr[0] = matmul()


def _in_proj_dn(x, wt, cache8, conv_w, *, t, tm, tn, o1, dk):
    m, k = x.shape
    n = wt.shape[0]
    nt = pl.cdiv(n, tn)
    nq = o1 // tn
    tiles_per_seq = t // tm
    assert o1 % (3 * tn) == 0 and t % tm == 0 and tn % dk == 0 and nt > nq
    qcol = lambda i, j: jnp.clip(j - 1, 0, nq - 1)
    kern = functools.partial(_in_proj_dn_kernel, n_valid=n, nq=nq, dk=dk, tiles_per_seq=tiles_per_seq)
    return pl.pallas_call(
        kern,
        grid=(m // tm, nt),
        in_specs=[pl.BlockSpec((tm, k), lambda i, j: (i, 0)),
                  pl.BlockSpec((tn, k), lambda i, j: (j, 0)),
                  pl.BlockSpec((1, SUBLANE, tn), lambda i, j: (i // tiles_per_seq, 0, qcol(i, j))),
                  pl.BlockSpec((4, tn), lambda i, j: (0, qcol(i, j)))],
        out_specs=[pl.BlockSpec((tm, tn), lambda i, j: (i, j)),
                   pl.BlockSpec((tm, tn), lambda i, j: (i, qcol(i, j)))],
        out_shape=[jax.ShapeDtypeStruct((m, nt * tn), F32), jax.ShapeDtypeStruct((m, o1), F32)],
        scratch_shapes=[pltpu.VMEM((2, tm, tn), F32), pltpu.VMEM((nq, SUBLANE, tn), F32)],
        compiler_params=_cparams(("arbitrary", "arbitrary")),
        name="in_proj",
    )(x, wt, cache8, conv_w)


def _dn_kernel(q_ref, k_ref, v_ref, gb_ref, z_ref, nw_ref, s0_ref, y_ref, sout_ref, s_scr,
               *, heads, hb, dk, chunk, nch):
    c = pl.program_id(2)
    nc = pl.num_programs(2)

    @pl.when(c == 0)
    def _():
        s_scr[...] = s0_ref[0]

    row = _iota2((chunk, chunk), 0)
    col = _iota2((chunk, chunk), 1)
    causal = row >= col
    strict = row > col
    nw = nw_ref[...]
    eye = (row == col).astype(F32)
    state = {h: s_scr[h] for h in range(hb)}
    applied = {h: 0 for h in range(hb)}
    per_chunk = {}

    def chunk_gates(j):
        if j not in per_chunk:
            gbt = gb_ref[0, 0, j * chunk:(j + 1) * chunk, :]
            gc_all = _cumsum_rows(gbt)
            per_chunk[j] = (gbt, gc_all, jnp.transpose(gc_all))
        return per_chunk[j]

    def program(hs, j):
        rows = slice(j * chunk, (j + 1) * chunk)
        gbt, gc_all, gc_t = chunk_gates(j)
        sl = {h: slice(h * dk, (h + 1) * dk) for h in hs}
        q = {h: q_ref[0, rows, sl[h]] for h in hs}
        k = {h: k_ref[0, rows, sl[h]] for h in hs}
        v = {h: v_ref[0, rows, sl[h]] for h in hs}
        beta = {h: gbt[:, h:h + 1] for h in hs}
        gcol = {h: gc_all[:, heads + h:heads + h + 1] for h in hs}
        glast = {h: gc_all[chunk - 1:chunk, heads + h:heads + h + 1] for h in hs}
        decay = {h: jnp.where(causal, jnp.exp(gcol[h] - gc_t[heads + h:heads + h + 1, :]), 0.0) for h in hs}
        kb = {h: k[h] * beta[h] for h in hs}
        eg = {h: jnp.exp(gcol[h]) for h in hs}
        yield
        m = {h: jnp.where(strict, -_dot_nt(kb[h], k[h]) * decay[h], 0.0) for h in hs}
        qk = {h: jnp.where(causal, _dot_nt(q[h], k[h]) * decay[h], 0.0) for h in hs}
        yield
        t = {h: eye + m[h] for h in hs}
        m = {h: _dot(m[h], m[h]) for h in hs}
        for _ in range(chunk.bit_length() - 3):
            yield
            res = {h: _dot(jnp.concatenate([m[h], t[h]], axis=0), m[h]) for h in hs}
            m = {h: res[h][:chunk] for h in hs}
            t = {h: t[h] + res[h][chunk:] for h in hs}
        yield
        t = {h: t[h] + _dot(t[h], m[h]) for h in hs}
        yield
        sol = {h: _dot(t[h], jnp.concatenate([v[h] * beta[h], kb[h] * eg[h]], axis=1)) for h in hs}
        yield
        assert all(applied[h] == j for h in hs)
        s = {h: state[h] for h in hs}
        v_new = {h: sol[h][:, :dk] - _dot(sol[h][:, dk:], s[h]) for h in hs}
        os = {h: _dot(q[h] * eg[h], s[h]) for h in hs}
        yield
        o = {h: os[h] + _dot(qk[h], v_new[h]) for h in hs}
        for h in hs:
            state[h] = s[h] * jnp.exp(glast[h]) + _dot_tn(k[h] * jnp.exp(glast[h] - gcol[h]), v_new[h])
            applied[h] = j + 1
        yield
        for h in hs:
            z = z_ref[0, rows, sl[h]]
            y_ref[0, rows, sl[h]] = (_rms(o[h], nw) * (z * _sigmoid(z))).astype(y_ref.dtype)

    gsz = _pick(hb, (4, 2, 1))
    _run_staggered([(j * DN_CHUNK_TICKS, program(range(g0, g0 + gsz), j))
                    for j in range(nch) for g0 in range(0, hb, gsz)])
    for h in range(hb):
        s_scr[h] = state[h]

    @pl.when(c == nc - 1)
    def _():
        sout_ref[0] = s_scr[...]


def _dn_recurrence(qkv, gb, p, norm_w, s0, *, heads, hb, dk, chunk, c_z):
    b, t, _ = qkv.shape
    groups = heads // hb
    wb = hb * dk
    nqk = heads * dk // wb
    nch = _pick(t // chunk, (REC_CHUNKS, 2, 1))
    rows = nch * chunk
    kern = functools.partial(_dn_kernel, heads=heads, hb=hb, dk=dk, chunk=chunk, nch=nch)
    return pl.pallas_call(
        kern,
        grid=(b, groups, t // rows),
        in_specs=[
            pl.BlockSpec((1, rows, wb), lambda bb, g, c: (bb, c, g)),
            pl.BlockSpec((1, rows, wb), lambda bb, g, c: (bb, c, nqk + g)),
            pl.BlockSpec((1, rows, wb), lambda bb, g, c: (bb, c, 2 * nqk + g)),
            pl.BlockSpec((1, 1, rows, LANE), lambda bb, g, c: (bb, g, c, 0)),
            pl.BlockSpec((1, rows, wb), lambda bb, g, c: (bb, c, c_z // wb + g)),
            pl.BlockSpec((1, dk), lambda bb, g, c: (0, 0)),
            pl.BlockSpec((1, hb, dk, dk), lambda bb, g, c: (bb, g, 0, 0)),
        ],
        out_specs=[
            pl.BlockSpec((1, rows, wb), lambda bb, g, c: (bb, c, g)),
            pl.BlockSpec((1, hb, dk, dk), lambda bb, g, c: (bb, g, 0, 0)),
        ],
        out_shape=[jax.ShapeDtypeStruct((b, t, heads * dk), BF16),
                   jax.ShapeDtypeStruct((b, heads, dk, dk), F32)],
        scratch_shapes=[pltpu.VMEM((hb, dk, dk), F32)],
        compiler_params=_cparams(("parallel", "parallel", "arbitrary")),
        name="dn_recurrence",
    )(qkv, qkv, qkv, gb, p, norm_w.reshape(1, dk), s0)


def _pair_sums(x, m0):
    s0 = jnp.sum(jnp.where(m0, x, 0.0), axis=-1, keepdims=True)
    s1 = jnp.sum(jnp.where(m0, 0.0, x), axis=-1, keepdims=True)
    return jnp.where(m0, s0, s1)


def _rw_pre_kernel(*refs, nblk, off, rw, xw_sl, xa_sl, xg_sl):
    main, halos = refs[:nblk], refs[nblk:2 * nblk]
    (cache_ref, mu_ref, w0_ref, w2_ref, a0_ref, a2_ref, g2_ref, kk_ref, ka_ref,
     r_ref, k_ref, v_ref, lw_ref, av_ref, bv_ref, gate_ref) = refs[2 * nblk:]
    u = jnp.concatenate([ref[0] for ref in main], axis=1)
    halo = jnp.concatenate([ref[0] for ref in halos], axis=1)
    halo = jnp.where(pl.program_id(1) == 0, cache_ref[0], halo)
    x = u + mu_ref[...] * (_shift_rows(u, halo, 1) - u)
    x = pltpu.roll(x, x.shape[1] - off, axis=1)
    r = x[:, :rw]
    kr = x[:, rw:2 * rw]
    vr = x[:, 2 * rw:3 * rw]
    z = w0_ref[...] + _dot(jnp.tanh(x[:, xw_sl[0]:xw_sl[1]]), w2_ref[...])
    a = _sigmoid(a0_ref[...] + _dot(x[:, xa_sl[0]:xa_sl[1]], a2_ref[...]))
    gate_ref[0] = _dot(_sigmoid(x[:, xg_sl[0]:xg_sl[1]]), g2_ref[...]).astype(gate_ref.dtype)
    r_ref[0] = r.astype(r_ref.dtype)
    v_ref[0] = vr.astype(v_ref.dtype)
    lw_ref[0] = -math.exp(-0.5) * _sigmoid(z)
    k_ref[0] = (kr * (1.0 + (a - 1.0) * ka_ref[...])).astype(k_ref.dtype)
    kkr = kr * kk_ref[...]
    m0 = _iota2((1, LANE), 1) < LANE // 2
    for jb in range(rw // LANE):
        sl = slice(jb * LANE, (jb + 1) * LANE)
        blk = kkr[:, sl]
        kk = blk * lax.rsqrt(_pair_sums(blk * blk, m0) + L2_EPS)
        av_ref[0, :, sl] = (-kk).astype(av_ref.dtype)
        bv_ref[0, :, sl] = (kk * a[:, sl]).astype(bv_ref.dtype)


def _rw_pre(p, cache8, pp):
    b, t, _ = p.shape
    rw, win0, width, bw = pp["rw"], pp["win0"], pp["win_w"], pp["win_bw"]
    nblk = width // bw
    tm = _pick(t, (128, 64, 32, 16))
    kern = functools.partial(_rw_pre_kernel, nblk=nblk, off=pp["win_off"], rw=rw,
                             xw_sl=pp["xw_sl"], xa_sl=pp["xa_sl"], xg_sl=pp["xg_sl"])
    full = lambda shape: pl.BlockSpec(shape, lambda bb, i: (0,) * len(shape))
    row = pl.BlockSpec((1, tm, rw), lambda bb, i: (bb, i, 0))
    cb = [win0 // bw + n for n in range(nblk)]
    main = [pl.BlockSpec((1, tm, bw), lambda bb, i, c=c: (bb, i, c)) for c in cb]
    halos = [pl.BlockSpec((1, SUBLANE, bw),
                          lambda bb, i, c=c: (bb, jnp.maximum(i * (tm // SUBLANE) - 1, 0), c)) for c in cb]
    small = [pp["mu_win"], pp["w0"], pp["w2p"], pp["a0"], pp["a2p"], pp["g2p"], pp["k_k"], pp["k_a"]]
    return pl.pallas_call(
        kern,
        grid=(b, t // tm),
        in_specs=main + halos + [pl.BlockSpec((1, SUBLANE, width), lambda bb, i: (bb, 0, 0))]
        + [full(a.shape) for a in small],
        out_specs=[row] * 7,
        out_shape=[jax.ShapeDtypeStruct((b, t, rw), F32 if n == 3 else RW_STORE) for n in range(7)],
        compiler_params=_cparams(("parallel", "arbitrary")),
        name="rw_pre",
    )(*([p] * (2 * nblk)), cache8, *small)


def _rw_kernel(r_ref, k_ref, v_ref, lw_ref, av_ref, bv_ref, gate_ref, rk_ref, lnw_ref, lnb_ref, s0_ref,
               y_ref, sout_ref, s_scr, *, pb, chunk, nch):
    c = pl.program_id(2)
    nc = pl.num_programs(2)
    hn = LANE // 2

    @pl.when(c == 0)
    def _():
        s_scr[...] = s0_ref[0]

    c2 = 2 * chunk
    m0 = _iota2((1, LANE), 1) < hn
    m1 = jnp.logical_not(m0)
    blockmask = (_iota2((LANE, LANE), 0) < hn) == (_iota2((LANE, LANE), 1) < hn)
    row2 = _iota2((chunk, c2), 0)
    col2 = _iota2((chunk, c2), 1) & (chunk - 1)
    strict2 = row2 > col2
    eye2 = (row2 == col2).astype(F32)
    incl4 = _iota2((chunk, 2 * c2), 0) >= (_iota2((chunk, 2 * c2), 1) & (chunk - 1))

    def by_head(x):
        return jnp.concatenate([jnp.where(m0, x, 0.0), jnp.where(m1, x, 0.0)], axis=0)

    bd = (_iota2((c2, c2), 0) < chunk) == (_iota2((c2, c2), 1) < chunk)

    def blockdiag(p2):
        return jnp.where(bd, jnp.concatenate([p2, p2], axis=0), 0.0)

    state = {p: s_scr[p] for p in range(pb)}
    applied = {p: 0 for p in range(pb)}

    def program(ps, j):
        rows = slice(j * chunk, (j + 1) * chunk)
        lanes = slice(ps[0] * LANE, (ps[-1] + 1) * LANE)
        sl = {p: slice(p * LANE, (p + 1) * LANE) for p in ps}
        loc = {p: slice((p - ps[0]) * LANE, (p - ps[0] + 1) * LANE) for p in ps}
        r = {p: r_ref[0, rows, sl[p]].astype(F32) for p in ps}
        k = {p: k_ref[0, rows, sl[p]].astype(F32) for p in ps}
        v = {p: v_ref[0, rows, sl[p]].astype(F32) for p in ps}
        lw = {p: lw_ref[0, rows, sl[p]] for p in ps}
        bv = {p: bv_ref[0, rows, sl[p]].astype(F32) for p in ps}
        cw_all = _cumsum_rows(lw_ref[0, rows, lanes])
        yield
        cw = {p: cw_all[:, loc[p]] for p in ps}
        tot = {p: cw[p][chunk - 1:chunk, :] for p in ps}
        e_neg = {p: jnp.exp(-cw[p]) for p in ps}
        e_end = {p: jnp.exp(tot[p] - cw[p]) for p in ps}
        lhs = {p: jnp.concatenate([av_ref[0, rows, sl[p]].astype(F32) * jnp.exp(cw[p] - lw[p]), r[p] * jnp.exp(cw[p])],
                                  axis=0) for p in ps}
        rhs_g = {p: jnp.concatenate([by_head(bv[p] * e_neg[p]), by_head(k[p] * e_neg[p])], axis=0) for p in ps}
        rhs_s = {p: jnp.concatenate([bv[p] * e_end[p], k[p] * e_end[p]], axis=0) for p in ps}
        vh = {p: by_head(v[p]) for p in ps}
        g = {p: _dot_nt(lhs[p], rhs_g[p]) for p in ps}
        yield
        avs = {p: _dot(jnp.where(strict2, g[p][:chunk, c2:], 0.0), vh[p]) for p in ps}
        m = {p: jnp.where(strict2, g[p][:chunk, :c2], 0.0) for p in ps}
        t2 = {p: eye2 + m[p] for p in ps}
        m = {p: _dot(m[p], blockdiag(m[p])) for p in ps}
        for _ in range(chunk.bit_length() - 3):
            yield
            res = {p: _dot(jnp.concatenate([m[p], t2[p]], axis=0), blockdiag(m[p])) for p in ps}
            m = {p: res[p][:chunk] for p in ps}
            t2 = {p: t2[p] + res[p][chunk:] for p in ps}
        yield
        t2 = {p: t2[p] + _dot(t2[p], blockdiag(m[p])) for p in ps}
        assert all(applied[p] == j for p in ps)
        s = {p: state[p] for p in ps}
        sr = {p: _dot_nt(lhs[p], s[p]) for p in ps}
        yield
        u = {p: _dot(t2[p], by_head(sr[p][:chunk] + avs[p])) for p in ps}
        yield
        yr = {p: _dot(jnp.where(incl4, g[p][chunk:], 0.0), jnp.concatenate([by_head(u[p]), vh[p]], axis=0))
              for p in ps}
        for p in ps:
            state[p] = jnp.where(blockmask, s[p] * jnp.exp(tot[p])
                                 + _dot_tn(jnp.concatenate([u[p], v[p]], axis=0), rhs_s[p]), 0.0)
            applied[p] = j + 1
        yield
        for p in ps:
            y = sr[p][chunk:] + yr[p]
            mean = _pair_sums(y, m0) * (1.0 / hn)
            d = y - mean
            var = _pair_sums(d * d, m0) * (1.0 / hn)
            yn = d * lax.rsqrt(var + RW_GN_EPS) * lnw_ref[:, sl[p]] + lnb_ref[:, sl[p]]
            bonus = _pair_sums(r[p] * k[p] * rk_ref[:, sl[p]], m0) * v[p]
            y_ref[0, rows, sl[p]] = ((yn + bonus) * gate_ref[0, rows, sl[p]].astype(F32)).astype(y_ref.dtype)

    gsz = _pick(pb, (4, 2, 1))
    _run_staggered([(j * REC_CHUNK_TICKS, program(range(g0, g0 + gsz), j))
                    for j in range(nch) for g0 in range(0, pb, gsz)])
    for p in range(pb):
        s_scr[p] = state[p]

    @pl.when(c == nc - 1)
    def _():
        sout_ref[0] = s_scr[...]


def _rw_recurrence(r, k, v, lw, av, bv, gate, r_k, ln_w, ln_b, s0p, *, pb, chunk):
    b, t, rw = r.shape
    pairs = rw // LANE
    groups = pairs // pb
    wb = pb * LANE
    nch = _pick(t // chunk, (REC_CHUNKS, 2, 1))
    kern = functools.partial(_rw_kernel, pb=pb, chunk=chunk, nch=nch)
    tile = pl.BlockSpec((1, nch * chunk, wb), lambda bb, g, c: (bb, c, g))
    par = pl.BlockSpec((1, wb), lambda bb, g, c: (0, g))
    st = pl.BlockSpec((1, pb, LANE, LANE), lambda bb, g, c: (bb, g, 0, 0))
    return pl.pallas_call(
        kern,
        grid=(b, groups, t // (nch * chunk)),
        in_specs=[tile] * 7 + [par] * 3 + [st],
        out_specs=[tile, st],
        out_shape=[jax.ShapeDtypeStruct((b, t, rw), BF16),
                   jax.ShapeDtypeStruct((b, pairs, LANE, LANE), F32)],
        scratch_shapes=[pltpu.VMEM((pb, LANE, LANE), F32)],
        compiler_params=_cparams(("parallel", "parallel", "arbitrary")),
        name="rw_recurrence",
    )(r, k, v, lw, av, bv, gate, r_k, ln_w, ln_b, s0p)


def _prepare(w):
    heads = w["dn_a_log"].shape[-1]
    dk = w["dn_norm_w"].shape[-1]
    qkv_w = w["dn_conv_w"].shape[-1]
    v_w = heads * dk
    assert qkv_w == 3 * v_w and dk == LANE
    rw_heads, rw_head = w["rw_r_k"].shape
    assert rw_head == LANE // 2
    rw = rw_heads * rw_head
    lw_n, la_n, lg_n = w["rw_w2"].shape[0], w["rw_a2"].shape[0], w["rw_g2"].shape[0]
    o1 = qkv_w
    o2 = o1 + v_w
    o4 = o2 + 2 * heads
    shift_w = 3 * rw + lw_n + la_n + lg_n
    assert o2 % LANE == 0 and 2 * heads <= LANE and w["w_in"].shape[1] == o4 + shift_w
    win0 = o4 // LANE * LANE
    win_off = o4 - win0
    win_w = _round_up(win_off + shift_w, LANE)
    win_bw = math.gcd(math.gcd(win0, win_w), 8 * LANE)
    assert win0 + win_w <= _round_up(o4 + shift_w, IN_TN)

    def lora_block(start, n, weight):
        lo, hi = start // LANE * LANE, _round_up(start + n, LANE)
        padded = jnp.pad(weight, ((start - lo, hi - start - n), (0, 0))).astype(BF16)
        return (lo, hi), padded

    xw_sl, w2p = lora_block(3 * rw, lw_n, w["rw_w2"])
    xa_sl, a2p = lora_block(3 * rw + lw_n, la_n, w["rw_a2"])
    xg_sl, g2p = lora_block(3 * rw + lw_n + la_n, lg_n, w["rw_g2"])
    lane_pad = lambda a: jnp.pad(a, (heads, LANE - 2 * heads)).reshape(1, LANE)
    in_window = lambda a: jnp.pad(a, [(0, 0)] * (a.ndim - 1) + [(win_off, win_w - win_off - shift_w)])
    return dict(
        heads=heads, dk=dk, rw=rw, rw_heads=rw_heads, o1=o1, o2=o2, o4=o4, shift_w=shift_w,
        win0=win0, win_off=win_off, win_w=win_w, win_bw=win_bw, in_window=in_window,
        xw_sl=xw_sl, xa_sl=xa_sl, xg_sl=xg_sl, w2p=w2p, a2p=a2p, g2p=g2p,
        alog_row=lane_pad(w["dn_a_log"]), dtb_row=lane_pad(w["dn_dt_bias"]),
        mu_win=in_window(w["rw_mu"].reshape(1, shift_w)),
        w0=w["rw_w0"].reshape(1, rw), a0=w["rw_a0"].reshape(1, rw),
        k_k=w["rw_k_k"].reshape(1, rw), k_a=w["rw_k_a"].reshape(1, rw),
        w_in_t=jnp.swapaxes(w["w_in"], 0, 1),
        w_down=w["w_down"].astype(BF16),
    )


def _layer(x, dn_state, dn_conv, rw_state, rw_shift, w, pp):
    b, t, d = x.shape
    m = b * t
    heads, dk, rw, rw_heads = pp["heads"], pp["dk"], pp["rw"], pp["rw_heads"]
    o1, o2, o4, shift_w = pp["o1"], pp["o2"], pp["o4"], pp["shift_w"]
    chunk = 64 if t % 64 == 0 else t
    assert chunk & (chunk - 1) == 0 and chunk >= 2 * SUBLANE
    hb = 16 if heads % 16 == 0 else heads
    pairs = rw // LANE
    pb = 16 if pairs % 16 == 0 else pairs
    tm = _pick(m, (1024, 512, 256, 128))

    xf = x.reshape(m, d)
    h = _rms_bf16(xf, w["g_mix_pre"])
    cache8 = jnp.pad(dn_conv.astype(F32), ((0, 0), (SUBLANE - dn_conv.shape[1], 0), (0, 0)))
    tm_in = min(tm, IN_TM_MAX)
    if t % tm_in == 0 and o1 % (3 * IN_TN) == 0:
        p, qkv = _in_proj_dn(h, pp["w_in_t"], cache8, w["dn_conv_w"], t=t, tm=tm_in, tn=IN_TN, o1=o1, dk=dk)
        p = p.reshape(b, t, p.shape[1])
        qkv = qkv.reshape(b, t, o1)
    else:
        p = _matmul_w32t(h, pp["w_in_t"], tm=tm, tn=IN_TN, name="in_proj")
        p = p.reshape(b, t, p.shape[1])
        qkv = _dn_pre(p, cache8, w["dn_conv_w"], heads=heads, dk=dk)

    gb = _dn_gates(p, pp["alog_row"], pp["dtb_row"], heads=heads, hb=hb, c_ba=o2)
    y_a, new_dn_state = _dn_recurrence(qkv, gb, p, w["dn_norm_w"], dn_state.astype(F32),
                                       heads=heads, hb=hb, dk=dk, chunk=chunk, c_z=o1)
    new_dn_conv = p[:, t - dn_conv.shape[1]:, :o1]

    shift8 = jnp.pad(pp["in_window"](rw_shift.astype(F32)), ((0, 0), (SUBLANE - 1, 0), (0, 0)))
    r, k, v, lw, av, bv, gate = _rw_pre(p, shift8, pp)
    hn = LANE // 2
    s4 = rw_state.astype(F32).reshape(b, pairs, 2, hn, hn)
    zeros = jnp.zeros_like(s4[:, :, 0])
    s0p = jnp.concatenate([jnp.concatenate([s4[:, :, 0], zeros], axis=-1),
                           jnp.concatenate([zeros, s4[:, :, 1]], axis=-1)], axis=-2)
    y_b, sp = _rw_recurrence(r, k, v, lw, av, bv, gate, w["rw_r_k"].reshape(1, rw),
                             w["rw_ln_w"].reshape(1, rw), w["rw_ln_b"].reshape(1, rw), s0p,
                             pb=pb, chunk=chunk)
    new_rw_state = jnp.stack([sp[:, :, :hn, :hn], sp[:, :, hn:, hn:]], axis=2).reshape(b, rw_heads, hn, hn)
    new_rw_shift = p[:, t - 1:, o4:o4 + shift_w]

    mixo = _matmul2(y_a.reshape(m, heads * dk), y_b.reshape(m, rw), w["w_out"],
                    tm=tm, tn=_pick(d, (512, 256, 128)), name="out_proj")
    h2 = _resid_rms(xf, mixo, w["g_mix_post"], w["g_ffn_pre"])
    dff = w["w_gate"].shape[1]
    f = _swiglu_up(h2, w["w_gate"], w["w_up"], tm=_pick(m, (2048, 1024, 512, 256, 128)),
                   tn=_pick(dff, (256, 128)))
    fo = _matmul(f, pp["w_down"], tm=_pick(m, (512, 256, 128)), tn=_pick(d, (512, 256, 128)),
                 out_dtype=BF16, name="ffn_down")
    out = _resid_final(xf, mixo, w["g_mix_post"], fo, w["g_ffn_post"]).reshape(b, t, d)
    return out, (new_dn_state, new_dn_conv, new_rw_state, new_rw_shift)


_WEIGHT_NAMES = ("g_mix_pre", "g_mix_post", "w_in", "dn_conv_w", "dn_a_log", "dn_dt_bias", "dn_norm_w",
                 "rw_mu", "rw_w0", "rw_w2", "rw_a0", "rw_a2", "rw_g2", "rw_k_k", "rw_k_a", "rw_r_k",
                 "rw_ln_w", "rw_ln_b", "w_out", "g_ffn_pre", "g_ffn_post", "w_gate", "w_up", "w_down")


def kernel(x_prompt, x_sample, state_dn, cache_dn_conv, state_rwkv, cache_rwkv_shift,
           g_mix_pre, g_mix_post, w_in, dn_conv_w, dn_a_log, dn_dt_bias, dn_norm_w,
           rw_mu, rw_w0, rw_w2, rw_a0, rw_a2, rw_g2, rw_k_k, rw_k_a, rw_r_k, rw_ln_w, rw_ln_b,
           w_out, g_ffn_pre, g_ffn_post, w_gate, w_up, w_down):
    stacked = (g_mix_pre, g_mix_post, w_in, dn_conv_w, dn_a_log, dn_dt_bias, dn_norm_w,
               rw_mu, rw_w0, rw_w2, rw_a0, rw_a2, rw_g2, rw_k_k, rw_k_a, rw_r_k, rw_ln_w, rw_ln_b,
               w_out, g_ffn_pre, g_ffn_post, w_gate, w_up, w_down)
    depth = w_in.shape[0]
    bp = x_prompt.shape[0]
    dt = x_prompt.dtype
    yp, ys = x_prompt, x_sample
    outs_p, outs_s = [], []
    for l in range(depth):
        w = {n: a[l] for n, a in zip(_WEIGHT_NAMES, stacked)}
        pp = _prepare(w)
        heads, dk, rw_heads = pp["heads"], pp["dk"], pp["rw_heads"]
        hn = LANE // 2
        yp, st_p = _layer(yp,
                          jnp.zeros((bp, heads, dk, dk), dt),
                          jnp.zeros((bp, cache_dn_conv.shape[2], cache_dn_conv.shape[3]), dt),
                          jnp.zeros((bp, rw_heads, hn, hn), dt),
                          jnp.zeros((bp, 1, cache_rwkv_shift.shape[3]), dt), w, pp)
        ys, st_s = _layer(ys, state_dn[l], cache_dn_conv[l], state_rwkv[l], cache_rwkv_shift[l], w, pp)
        outs_p.append(st_p)
        outs_s.append(st_s)
    stack = lambda outs, i: jnp.stack([o[i] for o in outs])
    return (yp, ys,
            stack(outs_p, 0), stack(outs_p, 1), stack(outs_p, 2), stack(outs_p, 3),
            stack(outs_s, 0), stack(outs_s, 1), stack(outs_s, 2), stack(outs_s, 3))
```
